```python
import jax, jax.numpy as jnp
from jax import lax
import numpy as np

D_MODEL = 1024
BATCH = 8
SEQ = 2048
DEPTH = 1

CHUNK = 64
D_RNN = D_MODEL // 2
RNN_BLOCKS = 8
RNN_BLOCK_W = D_RNN // RNN_BLOCKS
RNN_CONV_W = 4
RG_C = 8.0
ATT_HEAD_DIM = 64
D_ATT = D_MODEL // 2
N_ATT_HEADS = D_ATT // ATT_HEAD_DIM
LOOKBACK = 8
BAND = (LOOKBACK + 1) * CHUNK
REL_CLIP = 128
D_MIX = D_RNN + D_ATT
D_IN = 2 * D_RNN + 3 * D_ATT
D_FF = 2816
FFN_CONV_W = 3
EPS = 1e-6
ADA_SCALE = 0.5
NEG_INF = -1e30

kernel_name = "hybrid_rglru_chunkattn_convffn_adaln"


def rmsnorm(x, g):
    xf = x.astype(jnp.float32)
    y = xf * lax.rsqrt(jnp.mean(xf * xf, axis=-1, keepdims=True) + EPS)
    return (y * g.astype(jnp.float32)).astype(x.dtype)


def modulate(h, shift, scale):
    return h * (1.0 + scale[:, None, :]) + shift[:, None, :]


def causal_dwconv(x, w, b):
    width = w.shape[0]
    seq = x.shape[1]
    xp = jnp.pad(x, ((0, 0), (width - 1, 0), (0, 0)))
    y = xp[:, 0:seq] * w[0]
    for k in range(1, width):
        y = y + xp[:, k:k + seq] * w[k]
    return y + b


def rg_lru_group(xr, gr, conv_w, conv_b, wa, ba, wx, bx, lam):
    bsz, seq, _ = xr.shape
    xc = causal_dwconv(xr, conv_w, conv_b)
    xb = xc.reshape(bsz, seq, RNN_BLOCKS, RNN_BLOCK_W)
    r = jax.nn.sigmoid(jnp.einsum('bsnc,ncd->bsnd', xb, wa).reshape(bsz, seq, D_RNN) + ba)
    i = jax.nn.sigmoid(jnp.einsum('bsnc,ncd->bsnd', xb, wx).reshape(bsz, seq, D_RNN) + bx)
    log_a = RG_C * r.astype(jnp.float32) * jax.nn.log_sigmoid(lam.astype(jnp.float32))
    a = jnp.exp(log_a)
    mult = jnp.sqrt(-jnp.expm1(2.0 * log_a))
    bterm = mult * (i * xc).astype(jnp.float32)

    def combine(left, right):
        a1, b1 = left
        a2, b2 = right
        return a1 * a2, a2 * b1 + b2

    _, h = lax.associative_scan(combine, (a, bterm), axis=1)
    return h.astype(xr.dtype) * jax.nn.gelu(gr)


def chunk_attention_group(q, k, v, rel_bias):
    bsz, seq, _ = q.shape
    nc = seq // CHUNK
    shp = (bsz, nc, CHUNK, N_ATT_HEADS, ATT_HEAD_DIM)
    q = q.reshape(shp)
    k = k.reshape(shp)
    v = v.reshape(shp)
    pad = ((0, 0), (LOOKBACK, 0), (0, 0), (0, 0), (0, 0))
    kp = jnp.pad(k, pad)
    vp = jnp.pad(v, pad)
    band_idx = jnp.arange(nc)[:, None] + jnp.arange(LOOKBACK + 1)[None, :]
    kb = kp[:, band_idx].reshape(bsz, nc, BAND, N_ATT_HEADS, ATT_HEAD_DIM)
    vb = vp[:, band_idx].reshape(bsz, nc, BAND, N_ATT_HEADS, ATT_HEAD_DIM)
    qi = jnp.arange(CHUNK)
    kj = jnp.arange(BAND)
    rel = LOOKBACK * CHUNK + qi[:, None] - kj[None, :]
    bias = rel_bias[:, jnp.clip(rel, -REL_CLIP, REL_CLIP) + REL_CLIP]
    valid = (jnp.arange(nc)[:, None] - LOOKBACK + kj[None, :] // CHUNK) >= 0
    scale = ATT_HEAD_DIM ** -0.5
    s = jnp.einsum('bnqhd,bnkhd->bnhqk', q, kb).astype(jnp.float32) * scale
    s = s + bias.astype(jnp.float32)[None, None]
    s = jnp.where(valid[None, :, None, None, :], s, NEG_INF)
    p = jax.nn.softmax(s, axis=-1).astype(v.dtype)
    o = jnp.einsum('bnhqk,bnkhd->bnqhd', p, vb)
    return o.reshape(bsz, seq, D_ATT)


def setup_inputs(seed: int = 0) -> dict:
    key = jax.random.key(seed)
    ks = jax.random.split(key, 24)
    f32 = jnp.float32
    nrm = lambda k, shape, s: jax.random.normal(k, shape, f32) * s
    u = jax.random.uniform(ks[12], (DEPTH, D_RNN), f32, 0.9, 0.999)
    return {
        "x": nrm(ks[0], (BATCH, SEQ, D_MODEL), 1.0),
        "c": nrm(ks[1], (BATCH, D_MODEL), 1.0),
        "ada_w": nrm(ks[2], (DEPTH, D_MODEL, 6 * D_MODEL), ADA_SCALE * D_MODEL ** -0.5),
        "ada_b": nrm(ks[3], (DEPTH, 6 * D_MODEL), 0.02),
        "norm1_g": 1.0 + nrm(ks[4], (DEPTH, D_MODEL), 0.02),
        "w_in": nrm(ks[5], (DEPTH, D_MODEL, D_IN), D_MODEL ** -0.5),
        "rnn_conv_w": nrm(ks[6], (DEPTH, RNN_CONV_W, D_RNN), RNN_CONV_W ** -0.5),
        "rnn_conv_b": nrm(ks[7], (DEPTH, D_RNN), 0.02),
        "rg_wa": nrm(ks[8], (DEPTH, RNN_BLOCKS, RNN_BLOCK_W, RNN_BLOCK_W), RNN_BLOCK_W ** -0.5),
        "rg_ba": nrm(ks[9], (DEPTH, D_RNN), 0.02),
        "rg_wx": nrm(ks[10], (DEPTH, RNN_BLOCKS, RNN_BLOCK_W, RNN_BLOCK_W), RNN_BLOCK_W ** -0.5),
        "rg_bx": nrm(ks[11], (DEPTH, D_RNN), 0.02),
        "rg_lambda": jnp.log(u) - jnp.log1p(-u),
        "rel_bias": nrm(ks[13], (DEPTH, N_ATT_HEADS, 2 * REL_CLIP + 1), 0.5),
        "w_out": nrm(ks[14], (DEPTH, D_MIX, D_MODEL), D_MIX ** -0.5),
        "norm2_g": 1.0 + nrm(ks[15], (DEPTH, D_MODEL), 0.02),
        "w_up": nrm(ks[16], (DEPTH, D_MODEL, 2 * D_FF), D_MODEL ** -0.5),
        "ffn_conv_w": nrm(ks[17], (DEPTH, FFN_CONV_W, 2 * D_FF), FFN_CONV_W ** -0.5),
        "ffn_conv_b": nrm(ks[18], (DEPTH, 2 * D_FF), 0.02),
        "w_down": nrm(ks[19], (DEPTH, D_FF, D_MODEL), D_FF ** -0.5),
        "final_g": 1.0 + nrm(ks[20], (D_MODEL,), 0.02),
    }


def reference(x, c, ada_w, ada_b, norm1_g, w_in, rnn_conv_w, rnn_conv_b, rg_wa, rg_ba, rg_wx, rg_bx,
              rg_lambda, rel_bias, w_out, norm2_g, w_up, ffn_conv_w, ffn_conv_b, w_down, final_g):
    splits = [D_RNN, 2 * D_RNN, 2 * D_RNN + D_ATT, 2 * D_RNN + 2 * D_ATT]
    for l in range(DEPTH):
        mod = jax.nn.silu(c) @ ada_w[l] + ada_b[l]
        sh1, sc1, g1, sh2, sc2, g2 = jnp.split(mod, 6, axis=-1)
        h = modulate(rmsnorm(x, norm1_g[l]), sh1, sc1)
        proj = h @ w_in[l]
        xr, gr, q, k, v = jnp.split(proj, splits, axis=-1)
        y_rnn = rg_lru_group(xr, gr, rnn_conv_w[l], rnn_conv_b[l], rg_wa[l], rg_ba[l],
                             rg_wx[l], rg_bx[l], rg_lambda[l])
        y_att = chunk_attention_group(q, k, v, rel_bias[l])
        y = jnp.concatenate([y_rnn, y_att], axis=-1) @ w_out[l]
        x = x + g1[:, None, :] * y
        h = modulate(rmsnorm(x, norm2_g[l]), sh2, sc2)
        up = causal_dwconv(h @ w_up[l], ffn_conv_w[l], ffn_conv_b[l])
        ug, uv = jnp.split(up, 2, axis=-1)
        x = x + g2[:, None, :] * ((jax.nn.silu(ug) * uv) @ w_down[l])
    return rmsnorm(x, final_g)
```

```python
import functools

import jax
import jax.numpy as jnp
from jax import lax
from jax.experimental import pallas as pl
from jax.experimental.pallas import tpu as pltpu
from jax.scipy.linalg import block_diag

F32 = jnp.float32
BF16 = jnp.bfloat16

CHUNK = 64
LOOKBACK = 8
REL_CLIP = 128
HEAD_DIM = 64
RG_C = 8.0
EPS = 1e-6
NEG_INF = -1e30

SUBLANES = 8
LANES = 128
MXU_DIM = 256
VMEM_LIMIT_BYTES = 56 * 1024 * 1024

TT = 64
CHUNKS_PER_TILE = 4
TQ = CHUNKS_PER_TILE * CHUNK
HIST = LOOKBACK * CHUNK
NBUF = HIST + TQ
WIN = (LOOKBACK + 2) * CHUNK
GPAD = 768
RBPAD = 384
FF_SPLITS = ((0, 1024), (1024, 1024), (2048, 768))
FF_W = 1024


def _single(block_shape, index_map):
    return pl.BlockSpec(block_shape, index_map, pipeline_mode=pl.Buffered(1))


def _ada_kernel(c_ref, w_ref, b_ref, o_ref):
    c = c_ref[...]
    sc = c * jax.nn.sigmoid(c)
    o_ref[...] = jnp.dot(sc, w_ref[...], precision=lax.Precision.HIGHEST,
                         preferred_element_type=F32) + b_ref[...]


def _ada(c, ada_w, ada_b):
    bsz, d = c.shape
    n = ada_w.shape[1]
    bn = 1024
    return pl.pallas_call(
        _ada_kernel,
        grid=(n // bn,),
        in_specs=[pl.BlockSpec((bsz, d), lambda j: (0, 0)),
                  pl.BlockSpec((d, bn), lambda j: (0, j)),
                  pl.BlockSpec((1, bn), lambda j: (0, j))],
        out_specs=pl.BlockSpec((bsz, bn), lambda j: (0, j)),
        out_shape=jax.ShapeDtypeStruct((bsz, n), F32),
        name="ada_mod",
    )(c, ada_w, ada_b.reshape(1, n))


def _inproj_rglru_kernel(x_ref, mod_ref, n1g_ref, win_ref, cw_ref, cb_ref, wg0_ref, wg1_ref,
                         ba_ref, bx_ref, lam_ref, qkv_ref, yr_ref,
                         xrbuf, a_s, b_s, h_s, hc, *, d_model, d_rnn, d_att, conv_w):
    i = pl.program_id(0)
    rows = TT * SUBLANES
    halo = (conv_w - 1) * SUBLANES

    @pl.when(i == 0)
    def _():
        xrbuf[0:halo, :] = jnp.zeros((halo, d_rnn), F32)
        hc[...] = jnp.zeros_like(hc)

    xt = jnp.swapaxes(x_ref[...], 0, 1)
    ms = jnp.mean(xt * xt, axis=-1, keepdims=True)
    y = xt * lax.rsqrt(ms + EPS) * n1g_ref[...]
    sh1 = mod_ref[:, 0:d_model]
    sc1 = mod_ref[:, d_model:2 * d_model]
    h = y * (1.0 + sc1)[None] + sh1[None]
    hb = h.reshape(rows, d_model).astype(BF16)
    proj = jnp.dot(hb, win_ref[...], preferred_element_type=F32)

    o_q = 2 * d_rnn
    qkv_ref[:, 0:d_att] = (proj[:, o_q:o_q + d_att] * (HEAD_DIM ** -0.5)).astype(BF16)
    qkv_ref[:, d_att:3 * d_att] = proj[:, o_q + d_att:o_q + 3 * d_att].astype(BF16)

    gr = proj[:, d_rnn:2 * d_rnn]
    xrbuf[halo:halo + rows, :] = proj[:, 0:d_rnn]
    xc = cb_ref[...]
    for k in range(conv_w):
        xc = xc + xrbuf[k * SUBLANES:k * SUBLANES + rows, :] * cw_ref[k:k + 1, :]
    xrbuf[0:halo, :] = xrbuf[rows:rows + halo, :]

    xcb = xc.astype(BF16)
    half = d_rnn // 2
    g0 = jnp.dot(xcb[:, 0:half], wg0_ref[...], preferred_element_type=F32)
    g1 = jnp.dot(xcb[:, half:d_rnn], wg1_ref[...], preferred_element_type=F32)
    r = jax.nn.sigmoid(jnp.concatenate([g0[:, 0:half], g1[:, 0:half]], axis=1) + ba_ref[...])
    ig = jax.nn.sigmoid(jnp.concatenate([g0[:, half:], g1[:, half:]], axis=1) + bx_ref[...])
    lam = lam_ref[...]
    log_sig = jnp.minimum(lam, 0.0) - jnp.log1p(jnp.exp(-jnp.abs(lam)))
    log_a = RG_C * r * log_sig
    a = jnp.exp(log_a)
    mult = jnp.sqrt(jnp.tanh(-log_a) * (a * a + 1.0))
    bterm = mult * (ig * xc)
    a_s[...] = a.reshape(TT, SUBLANES, d_rnn)
    b_s[...] = bterm.reshape(TT, SUBLANES, d_rnn)

    def step(t, hprev):
        hnew = a_s[t] * hprev + b_s[t]
        h_s[t] = hnew
        return hnew

    hc[...] = lax.fori_loop(0, TT, step, hc[...], unroll=8)
    hs = h_s[...].reshape(rows, d_rnn)
    yr_ref[...] = (hs * jax.nn.gelu(gr)).astype(BF16)


def _inproj_rglru(x, mod, n1g, w_in_b, conv_w, conv_b, wg0, wg1, ba, bx, lam):
    bsz, seq, d = x.shape
    assert bsz == SUBLANES and seq % TT == 0
    d_in = w_in_b.shape[1]
    d_rnn = conv_w.shape[1]
    d_att = (d_in - 2 * d_rnn) // 3
    cw = conv_w.shape[0]
    rows = TT * SUBLANES
    const = lambda i: (0, 0)
    kern = functools.partial(_inproj_rglru_kernel, d_model=d, d_rnn=d_rnn, d_att=d_att, conv_w=cw)
    return pl.pallas_call(
        kern,
        grid=(seq // TT,),
        in_specs=[
            pl.BlockSpec((bsz, TT, d), lambda i: (0, i, 0)),
            _single(mod.shape, const),
            _single((1, d), const),
            _single(w_in_b.shape, const),
            _single(conv_w.shape, const),
            _single((1, d_rnn), const),
            _single(wg0.shape, const),
            _single(wg1.shape, const),
            _single((1, d_rnn), const),
            _single((1, d_rnn), const),
            _single((1, d_rnn), const),
        ],
        out_specs=[pl.BlockSpec((rows, 3 * d_att), lambda i: (i, 0)),
                   pl.BlockSpec((rows, d_rnn), lambda i: (i, 0))],
        out_shape=[jax.ShapeDtypeStruct((seq * bsz, 3 * d_att), BF16),
                   jax.ShapeDtypeStruct((seq * bsz, d_rnn), BF16)],
        scratch_shapes=[
            pltpu.VMEM((rows + (cw - 1) * SUBLANES, d_rnn), F32),
            pltpu.VMEM((TT, SUBLANES, d_rnn), F32),
            pltpu.VMEM((TT, SUBLANES, d_rnn), F32),
            pltpu.VMEM((TT, SUBLANES, d_rnn), F32),
            pltpu.VMEM((SUBLANES, d_rnn), F32),
        ],
        compiler_params=pltpu.CompilerParams(dimension_semantics=("arbitrary",),
                                             vmem_limit_bytes=VMEM_LIMIT_BYTES),
        name="inproj_rglru",
    )(x, mod, n1g.reshape(1, d), w_in_b, conv_w, conv_b.reshape(1, d_rnn), wg0, wg1,
      ba.reshape(1, d_rnn), bx.reshape(1, d_rnn), lam.reshape(1, d_rnn))


def _build_bias_tables(rbp_ref, bias_s, n_heads):
    r_io = lax.broadcasted_iota(jnp.int32, (RBPAD, GPAD), 0)
    j_io = lax.broadcasted_iota(jnp.int32, (RBPAD, GPAD), 1)
    idx = jnp.clip(HIST + CHUNK - 1 - j_io, -REL_CLIP, REL_CLIP) + REL_CLIP
    sel = (r_io == idx).astype(F32)
    g = jnp.dot(rbp_ref[...], sel, precision=lax.Precision.HIGHEST, preferred_element_type=F32)
    col = lax.broadcasted_iota(jnp.int32, (CHUNK, WIN), 1)
    band = (LOOKBACK + 1) * CHUNK
    for h in range(n_heads):
        gh = jnp.broadcast_to(g[h:h + 1, :], (CHUNK, GPAD))
        even = pltpu.roll(gh, GPAD - (CHUNK - 1), 1, stride=1, stride_axis=0)[:, 0:WIN]
        odd = pltpu.roll(gh, 1, 1, stride=1, stride_axis=0)[:, 0:WIN]
        bias_s[h, 0] = jnp.where(col < band, even, NEG_INF)
        bias_s[h, 1] = jnp.where(col >= CHUNK, odd, NEG_INF)


def _attn_kernel(q_ref, k_ref, v_ref, rbp_ref, o_ref, kbuf, vbuf, s_s, p_s, bias_s, *, n_heads):
    b = pl.program_id(0)
    t = pl.program_id(1)
    d_att = n_heads * HEAD_DIM

    @pl.when(jnp.logical_and(b == 0, t == 0))
    def _():
        _build_bias_tables(rbp_ref, bias_s, n_heads)
        p_s[...] = jnp.zeros_like(p_s)

    @pl.when(t == 0)
    def _():
        kbuf[0:HIST, :] = jnp.zeros((HIST, d_att), BF16)
        vbuf[0:HIST, :] = jnp.zeros((HIST, d_att), BF16)

    kbuf[HIST:NBUF, :] = k_ref[...]
    vbuf[HIST:NBUF, :] = v_ref[...]

    first_valid = CHUNK * (LOOKBACK - CHUNKS_PER_TILE * t)
    lane = lax.broadcasted_iota(jnp.int32, (1, LANES), 1)
    heads_per_vreg = LANES // HEAD_DIM
    for pair in range(n_heads // heads_per_vreg):
        lanes = slice(pair * LANES, (pair + 1) * LANES)
        qp = q_ref[:, lanes]
        kp = kbuf[:, lanes]
        vp = vbuf[:, lanes]
        o_pair = None
        for hh in range(heads_per_vreg):
            h = pair * heads_per_vreg + hh
            in_head = jnp.logical_and(lane >= hh * HEAD_DIM, lane < (hh + 1) * HEAD_DIM)
            qm = jnp.where(in_head, qp, jnp.zeros_like(qp))
            s_s[...] = lax.dot_general(qm, kp, (((1,), (1,)), ((), ())),
                                       preferred_element_type=F32)
            inv_l = []
            for c in range(CHUNKS_PER_TILE):
                w0 = (c // 2) * LANES
                rs = slice(c * CHUNK, (c + 1) * CHUNK)
                s = s_s[rs, w0:w0 + WIN] + bias_s[h, c % 2]
                col = lax.broadcasted_iota(jnp.int32, (1, WIN), 1) + w0
                s = jnp.where(col >= first_valid, s, NEG_INF)
                m = jnp.max(s, axis=-1, keepdims=True)
                p = jnp.exp(s - m)
                inv_l.append(1.0 / jnp.sum(p, axis=-1, keepdims=True))
                p_s[rs, w0:w0 + WIN] = p.astype(BF16)
            o = jnp.dot(p_s[...], vp, preferred_element_type=F32)
            o = o * jnp.concatenate(inv_l, axis=0)
            o_pair = o if o_pair is None else jnp.where(in_head, o, o_pair)
        o_ref[:, lanes] = o_pair.astype(BF16)

    kbuf[0:HIST, :] = kbuf[TQ:NBUF, :]
    vbuf[0:HIST, :] = vbuf[TQ:NBUF, :]


def _attention(qkv2, rbp, bsz, seq, d_att):
    n_heads = d_att // HEAD_DIM
    assert seq % TQ == 0 and HIST % TQ == 0
    kern = functools.partial(_attn_kernel, n_heads=n_heads)
    return pl.pallas_call(
        kern,
        grid=(bsz, seq // TQ),
        in_specs=[
            pl.BlockSpec((TQ, d_att), lambda b, t: (t, 3 * b)),
            pl.BlockSpec((TQ, d_att), lambda b, t: (t, 3 * b + 1)),
            pl.BlockSpec((TQ, d_att), lambda b, t: (t, 3 * b + 2)),
            _single(rbp.shape, lambda b, t: (0, 0)),
        ],
        out_specs=pl.BlockSpec((TQ, d_att), lambda b, t: (t, b)),
        out_shape=jax.ShapeDtypeStruct((seq, bsz * d_att), BF16),
        scratch_shapes=[
            pltpu.VMEM((NBUF, d_att), BF16),
            pltpu.VMEM((NBUF, d_att), BF16),
            pltpu.VMEM((TQ, NBUF), F32),
            pltpu.VMEM((TQ, NBUF), BF16),
            pltpu.VMEM((n_heads, 2, CHUNK, WIN), F32),
        ],
        compiler_params=pltpu.CompilerParams(dimension_semantics=("arbitrary", "arbitrary"),
                                             vmem_limit_bytes=VMEM_LIMIT_BYTES),
        name="chunk_attn",
    )(qkv2, qkv2, qkv2, rbp)


def _outproj_ffn_kernel(x_ref, yr_ref, ya_ref, mod_ref, n2g_ref, fg_ref, woa_ref, wob_ref,
                        wu_ref, fcw_ref, fcb_ref, wd_ref, o_ref, ub, halo_s, z_s,
                        *, d_model, d_ff, conv_w):
    i = pl.program_id(0)
    rows = TT * SUBLANES
    halo = (conv_w - 1) * SUBLANES

    @pl.when(i == 0)
    def _():
        halo_s[...] = jnp.zeros_like(halo_s)

    xt = jnp.swapaxes(x_ref[...], 0, 1)
    y = (jnp.dot(yr_ref[...], woa_ref[...], preferred_element_type=F32)
         + jnp.dot(ya_ref[...], wob_ref[...], preferred_element_type=F32))
    g1 = mod_ref[:, 2 * d_model:3 * d_model]
    sh2 = mod_ref[:, 3 * d_model:4 * d_model]
    sc2 = mod_ref[:, 4 * d_model:5 * d_model]
    g2 = mod_ref[:, 5 * d_model:6 * d_model]
    x1 = xt + g1[None] * y.reshape(TT, SUBLANES, d_model)
    ms = jnp.mean(x1 * x1, axis=-1, keepdims=True)
    hn = x1 * lax.rsqrt(ms + EPS) * n2g_ref[...]
    hn = hn * (1.0 + sc2)[None] + sh2[None]
    hb = hn.reshape(rows, d_model).astype(BF16)

    first = True
    for c0, w in FF_SPLITS:
        act = None
        for part, off in ((0, 0), (1, d_ff)):
            lo = part * FF_W
            cols = slice(off + c0, off + c0 + w)
            ub[halo:halo + rows, lo:lo + w] = jnp.dot(hb, wu_ref[:, cols],
                                                      preferred_element_type=F32)
            ub[0:halo, lo:lo + w] = halo_s[:, cols]
            halo_s[:, cols] = ub[rows:rows + halo, lo:lo + w]
            u = fcb_ref[:, cols]
            for k in range(conv_w):
                u = u + ub[k * SUBLANES:k * SUBLANES + rows, lo:lo + w] * fcw_ref[k:k + 1, cols]
            act = u * jax.nn.sigmoid(u) if part == 0 else act * u
        zc = jnp.dot(act.astype(BF16), wd_ref[c0:c0 + w, :], preferred_element_type=F32)
        if first:
            z_s[...] = zc
            first = False
        else:
            z_s[...] += zc

    x2 = x1 + g2[None] * z_s[...].reshape(TT, SUBLANES, d_model)
    ms2 = jnp.mean(x2 * x2, axis=-1, keepdims=True)
    out = x2 * lax.rsqrt(ms2 + EPS) * fg_ref[...]
    o_ref[...] = jnp.swapaxes(out, 0, 1)


def _outproj_ffn(x, yr, ya, mod, n2g, fg, wo_a, wo_b, w_up_b, fcw, fcb, w_down_b):
    bsz, seq, d = x.shape
    d_ff = w_down_b.shape[0]
    cw = fcw.shape[0]
    d_rnn = yr.shape[1]
    d_att = ya.shape[1]
    rows = TT * SUBLANES
    assert sum(w for _, w in FF_SPLITS) == d_ff
    const = lambda i: (0, 0)
    kern = functools.partial(_outproj_ffn_kernel, d_model=d, d_ff=d_ff, conv_w=cw)
    return pl.pallas_call(
        kern,
        grid=(seq // TT,),
        in_specs=[
            pl.BlockSpec((bsz, TT, d), lambda i: (0, i, 0)),
            pl.BlockSpec((rows, d_rnn), lambda i: (i, 0)),
            pl.BlockSpec((rows, d_att), lambda i: (i, 0)),
            _single(mod.shape, const),
            _single((1, d), const),
            _single((1, d), const),
            _single(wo_a.shape, const),
            _single(wo_b.shape, const),
            _single(w_up_b.shape, const),
            _single(fcw.shape, const),
            _single((1, 2 * d_ff), const),
            _single(w_down_b.shape, const),
        ],
        out_specs=pl.BlockSpec((bsz, TT, d), lambda i: (0, i, 0)),
        out_shape=jax.ShapeDtypeStruct((bsz, seq, d), F32),
        scratch_shapes=[
            pltpu.VMEM((rows + (cw - 1) * SUBLANES, 2 * FF_W), F32),
            pltpu.VMEM(((cw - 1) * SUBLANES, 2 * d_ff), F32),
            pltpu.VMEM((rows, d), F32),
        ],
        compiler_params=pltpu.CompilerParams(dimension_semantics=("arbitrary",),
                                             vmem_limit_bytes=VMEM_LIMIT_BYTES),
        name="outproj_ffn",
    )(x, yr, ya, mod, n2g.reshape(1, d), fg.reshape(1, d), wo_a, wo_b, w_up_b, fcw,
      fcb.reshape(1, 2 * d_ff), w_down_b)


def _gate_weights(wa, wx):
    nb = wa.shape[0]
    hb = nb // 2
    halves = []
    for s in (slice(0, hb), slice(hb, nb)):
        halves.append(jnp.concatenate([block_diag(*wa[s]), block_diag(*wx[s])], axis=1).astype(BF16))
    return halves


def kernel(x, c, ada_w, ada_b, norm1_g, w_in, rnn_conv_w, rnn_conv_b, rg_wa, rg_ba, rg_wx, rg_bx, rg_lambda, rel_bias, w_out, norm2_g, w_up, ffn_conv_w, ffn_conv_b, w_down, final_g):
    bsz, seq, d = x.shape
    assert ada_w.shape[0] == 1
    l = 0
    d_rnn = rnn_conv_w.shape[2]
    d_att = w_out.shape[1] - d_rnn
    mod = _ada(c, ada_w[l], ada_b[l])
    wg0, wg1 = _gate_weights(rg_wa[l], rg_wx[l])
    qkv, yr = _inproj_rglru(x, mod, norm1_g[l], w_in[l].astype(BF16), rnn_conv_w[l],
                            rnn_conv_b[l], wg0, wg1, rg_ba[l], rg_bx[l], rg_lambda[l])
    rbp = jnp.pad(rel_bias[l], ((0, 0), (0, RBPAD - rel_bias.shape[2])))
    ya = _attention(qkv.reshape(seq, bsz * 3 * d_att), rbp, bsz, seq, d_att)
    w_out_b = w_out[l].astype(BF16)
    return _outproj_ffn(x, yr, ya.reshape(seq * bsz, d_att), mod, norm2_g[l], final_g,
                        w_out_b[0:d_rnn], w_out_b[d_rnn:], w_up[l].astype(BF16), ffn_conv_w[l],
                        ffn_conv_b[l], w_down[l].astype(BF16))
```

```python
import functools

import jax
import jax.numpy as jnp
from jax import lax
from jax.experimental import pallas as pl
from jax.experimental.pallas import tpu as pltpu
from jax.scipy.linalg import block_diag

F32 = jnp.float32
BF16 = jnp.bfloat16

CHUNK = 64
LOOKBACK = 8
REL_CLIP = 128
HEAD_DIM = 64
RG_C = 8.0
EPS = 1e-6
NEG_INF = -1e30

SUBLANES = 8
LANES = 128
MXU_DIM = 256
VMEM_LIMIT_BYTES = 56 * 1024 * 1024

TT = 64
CHUNKS_PER_TILE = 4
TQ = CHUNKS_PER_TILE * CHUNK
HIST = LOOKBACK * CHUNK
NBUF = HIST + TQ
WIN = (LOOKBACK + 2) * CHUNK
GPAD = 768
RBPAD = 384
FF_SPLITS = ((0, 1024), (1024, 1024), (2048, 768))
FF_W = 1024


def _single(block_shape, index_map):
    return pl.BlockSpec(block_shape, index_map, pipeline_mode=pl.Buffered(1))


def _ada_kernel(c_ref, w_ref, b_ref, o_ref):
    c = c_ref[...]
    sc = c * jax.nn.sigmoid(c)
    o_ref[...] = jnp.dot(sc, w_ref[...], precision=lax.Precision.HIGHEST,
                         preferred_element_type=F32) + b_ref[...]


def _ada(c, ada_w, ada_b):
    bsz, d = c.shape
    n = ada_w.shape[1]
    bn = 1024
    return pl.pallas_call(
        _ada_kernel,
        grid=(n // bn,),
        in_specs=[pl.BlockSpec((bsz, d), lambda j: (0, 0)),
                  pl.BlockSpec((d, bn), lambda j: (0, j)),
                  pl.BlockSpec((1, bn), lambda j: (0, j))],
        out_specs=pl.BlockSpec((bsz, bn), lambda j: (0, j)),
        out_shape=jax.ShapeDtypeStruct((bsz, n), F32),
        name="ada_mod",
    )(c, ada_w, ada_b.reshape(1, n))


def _inproj_rglru_kernel(x_ref, mod_ref, n1g_ref, win_ref, cw_ref, cb_ref, wg0_ref, wg1_ref,
                         ba_ref, bx_ref, lam_ref, qkv_ref, yr_ref,
                         xrbuf, a_s, b_s, h_s, hc, *, d_model, d_rnn, d_att, conv_w):
    i = pl.program_id(0)
    rows = TT * SUBLANES
    halo = (conv_w - 1) * SUBLANES

    @pl.when(i == 0)
    def _():
        xrbuf[0:halo, :] = jnp.zeros((halo, d_rnn), F32)
        hc[...] = jnp.zeros_like(hc)

    x = x_ref[...]
    ms = jnp.mean(x * x, axis=-1, keepdims=True)
    y = x * lax.rsqrt(ms + EPS) * n1g_ref[...]
    sh1 = mod_ref[:, :, 0:d_model]
    sc1 = mod_ref[:, :, d_model:2 * d_model]
    h = y * (1.0 + sc1) + sh1
    hb = h.reshape(rows, d_model).astype(BF16)
    proj = jnp.dot(hb, win_ref[...], preferred_element_type=F32)

    o_q = 2 * d_rnn
    q = proj[:, o_q:o_q + d_att] * (HEAD_DIM ** -0.5)
    qkv_ref[:, :, 0:d_att] = q.reshape(SUBLANES, TT, d_att).astype(BF16)
    kv = proj[:, o_q + d_att:o_q + 3 * d_att]
    qkv_ref[:, :, d_att:3 * d_att] = kv.reshape(SUBLANES, TT, 2 * d_att).astype(BF16)

    rg = proj[:, 0:2 * d_rnn].reshape(SUBLANES, TT, 2 * d_rnn)
    rg = jnp.swapaxes(rg, 0, 1).reshape(rows, 2 * d_rnn)
    gr = rg[:, d_rnn:2 * d_rnn]
    xrbuf[halo:halo + rows, :] = rg[:, 0:d_rnn]
    xc = cb_ref[...]
    for k in range(conv_w):
        xc = xc + xrbuf[k * SUBLANES:k * SUBLANES + rows, :] * cw_ref[k:k + 1, :]
    xrbuf[0:halo, :] = xrbuf[rows:rows + halo, :]

    xcb = xc.astype(BF16)
    half = d_rnn // 2
    g0 = jnp.dot(xcb[:, 0:half], wg0_ref[...], preferred_element_type=F32)
    g1 = jnp.dot(xcb[:, half:d_rnn], wg1_ref[...], preferred_element_type=F32)
    r = jax.nn.sigmoid(jnp.concatenate([g0[:, 0:half], g1[:, 0:half]], axis=1) + ba_ref[...])
    ig = jax.nn.sigmoid(jnp.concatenate([g0[:, half:], g1[:, half:]], axis=1) + bx_ref[...])
    lam = lam_ref[...]
    log_sig = jnp.minimum(lam, 0.0) - jnp.log1p(jnp.exp(-jnp.abs(lam)))
    log_a = RG_C * r * log_sig
    a = jnp.exp(log_a)
    mult = jnp.sqrt(jnp.tanh(-log_a) * (a * a + 1.0))
    bterm = mult * (ig * xc)
    a_s[...] = a.reshape(TT, SUBLANES, d_rnn)
    b_s[...] = bterm.reshape(TT, SUBLANES, d_rnn)

    def step(t, hprev):
        hnew = a_s[t] * hprev + b_s[t]
        h_s[t] = hnew
        return hnew

    hc[...] = lax.fori_loop(0, TT, step, hc[...], unroll=8)
    yr = h_s[...] * jax.nn.gelu(gr).reshape(TT, SUBLANES, d_rnn)
    yr_ref[...] = jnp.swapaxes(yr, 0, 1).astype(BF16)


def _inproj_rglru(x, mod, n1g, w_in_b, conv_w, conv_b, wg0, wg1, ba, bx, lam):
    bsz, seq, d = x.shape
    assert bsz == SUBLANES and seq % TT == 0
    d_in = w_in_b.shape[1]
    d_rnn = conv_w.shape[1]
    d_att = (d_in - 2 * d_rnn) // 3
    cw = conv_w.shape[0]
    rows = TT * SUBLANES
    const = lambda i: (0, 0)
    kern = functools.partial(_inproj_rglru_kernel, d_model=d, d_rnn=d_rnn, d_att=d_att, conv_w=cw)
    return pl.pallas_call(
        kern,
        grid=(seq // TT,),
        in_specs=[
            pl.BlockSpec((bsz, TT, d), lambda i: (0, i, 0)),
            _single(mod.shape, lambda i: (0, 0, 0)),
            _single((1, d), const),
            _single(w_in_b.shape, const),
            _single(conv_w.shape, const),
            _single((1, d_rnn), const),
            _single(wg0.shape, const),
            _single(wg1.shape, const),
            _single((1, d_rnn), const),
            _single((1, d_rnn), const),
            _single((1, d_rnn), const),
        ],
        out_specs=[pl.BlockSpec((bsz, TT, 3 * d_att), lambda i: (0, i, 0)),
                   pl.BlockSpec((bsz, TT, d_rnn), lambda i: (0, i, 0))],
        out_shape=[jax.ShapeDtypeStruct((bsz, seq, 3 * d_att), BF16),
                   jax.ShapeDtypeStruct((bsz, seq, d_rnn), BF16)],
        scratch_shapes=[
            pltpu.VMEM((rows + (cw - 1) * SUBLANES, d_rnn), F32),
            pltpu.VMEM((TT, SUBLANES, d_rnn), F32),
            pltpu.VMEM((TT, SUBLANES, d_rnn), F32),
            pltpu.VMEM((TT, SUBLANES, d_rnn), F32),
            pltpu.VMEM((SUBLANES, d_rnn), F32),
        ],
        compiler_params=pltpu.CompilerParams(dimension_semantics=("arbitrary",),
                                             vmem_limit_bytes=VMEM_LIMIT_BYTES),
        name="inproj_rglru",
    )(x, mod, n1g.reshape(1, d), w_in_b, conv_w, conv_b.reshape(1, d_rnn), wg0, wg1,
      ba.reshape(1, d_rnn), bx.reshape(1, d_rnn), lam.reshape(1, d_rnn))


def _build_bias_tables(rbp_ref, bias_s, n_heads):
    r_io = lax.broadcasted_iota(jnp.int32, (RBPAD, GPAD), 0)
    j_io = lax.broadcasted_iota(jnp.int32, (RBPAD, GPAD), 1)
    idx = jnp.clip(HIST + CHUNK - 1 - j_io, -REL_CLIP, REL_CLIP) + REL_CLIP
    sel = (r_io == idx).astype(F32)
    g = jnp.dot(rbp_ref[...], sel, precision=lax.Precision.HIGHEST, preferred_element_type=F32)
    col = lax.broadcasted_iota(jnp.int32, (CHUNK, WIN), 1)
    band = (LOOKBACK + 1) * CHUNK
    for h in range(n_heads):
        gh = jnp.broadcast_to(g[h:h + 1, :], (CHUNK, GPAD))
        even = pltpu.roll(gh, GPAD - (CHUNK - 1), 1, stride=1, stride_axis=0)[:, 0:WIN]
        odd = pltpu.roll(gh, 1, 1, stride=1, stride_axis=0)[:, 0:WIN]
        bias_s[h, 0] = jnp.where(col < band, even, NEG_INF)
        bias_s[h, 1] = jnp.where(col >= CHUNK, odd, NEG_INF)


def _attn_tile(q_ref, o_ref, kbuf, vbuf, bias_s, first_valid, n_heads):
    lane = lax.broadcasted_iota(jnp.int32, (1, LANES), 1)
    heads_per_vreg = LANES // HEAD_DIM
    pad = jnp.zeros((CHUNK, NBUF - WIN), BF16)
    for pair in range(n_heads // heads_per_vreg):
        lanes = slice(pair * LANES, (pair + 1) * LANES)
        qp = q_ref[:, lanes]
        kp = kbuf[:, lanes]
        vp = vbuf[:, lanes]
        o_pair = None
        for hh in range(heads_per_vreg):
            h = pair * heads_per_vreg + hh
            in_head = jnp.logical_and(lane >= hh * HEAD_DIM, lane < (hh + 1) * HEAD_DIM)
            qm = jnp.where(in_head, qp, jnp.zeros_like(qp))
            s_all = lax.dot_general(qm, kp, (((1,), (1,)), ((), ())),
                                    preferred_element_type=F32)
            inv_l = []
            p_rows = []
            for c in range(CHUNKS_PER_TILE):
                w0 = (c // 2) * LANES
                s = s_all[c * CHUNK:(c + 1) * CHUNK, w0:w0 + WIN] + bias_s[h, c % 2]
                if first_valid is not None:
                    col = lax.broadcasted_iota(jnp.int32, (1, WIN), 1) + w0
                    s = jnp.where(col >= first_valid, s, NEG_INF)
                m = jnp.max(s, axis=-1, keepdims=True)
                p = jnp.exp(s - m)
                inv_l.append(1.0 / jnp.sum(p, axis=-1, keepdims=True))
                pb = p.astype(BF16)
                p_rows.append(jnp.concatenate([pb, pad] if w0 == 0 else [pad, pb], axis=1))
            pmat = jnp.concatenate(p_rows, axis=0)
            o = jnp.dot(pmat, vp, preferred_element_type=F32) * jnp.concatenate(inv_l, axis=0)
            o_pair = o if o_pair is None else jnp.where(in_head, o, o_pair)
        o_ref[:, lanes] = o_pair.astype(BF16)


def _attn_kernel(q_ref, k_ref, v_ref, rbp_ref, o_ref, kbuf, vbuf, bias_s, *, n_heads):
    b = pl.program_id(0)
    t = pl.program_id(1)
    d_att = n_heads * HEAD_DIM

    @pl.when(jnp.logical_and(b == 0, t == 0))
    def _():
        _build_bias_tables(rbp_ref, bias_s, n_heads)

    @pl.when(t == 0)
    def _():
        kbuf[0:HIST, :] = jnp.zeros((HIST, d_att), BF16)
        vbuf[0:HIST, :] = jnp.zeros((HIST, d_att), BF16)

    kbuf[HIST:NBUF, :] = k_ref[...]
    vbuf[HIST:NBUF, :] = v_ref[...]

    @pl.when(t < HIST // TQ)
    def _():
        first_valid = CHUNK * (LOOKBACK - CHUNKS_PER_TILE * t)
        _attn_tile(q_ref, o_ref, kbuf, vbuf, bias_s, first_valid, n_heads)

    @pl.when(t >= HIST // TQ)
    def _():
        _attn_tile(q_ref, o_ref, kbuf, vbuf, bias_s, None, n_heads)

    kbuf[0:HIST, :] = kbuf[TQ:NBUF, :]
    vbuf[0:HIST, :] = vbuf[TQ:NBUF, :]


def _attention(qkv, rbp):
    bsz, seq, d3 = qkv.shape
    d_att = d3 // 3
    n_heads = d_att // HEAD_DIM
    assert seq % TQ == 0 and HIST % TQ == 0
    kern = functools.partial(_attn_kernel, n_heads=n_heads)
    return pl.pallas_call(
        kern,
        grid=(bsz, seq // TQ),
        in_specs=[
            pl.BlockSpec((None, TQ, d_att), lambda b, t: (b, t, 0)),
            pl.BlockSpec((None, TQ, d_att), lambda b, t: (b, t, 1)),
            pl.BlockSpec((None, TQ, d_att), lambda b, t: (b, t, 2)),
            _single(rbp.shape, lambda b, t: (0, 0)),
        ],
        out_specs=pl.BlockSpec((None, TQ, d_att), lambda b, t: (b, t, 0)),
        out_shape=jax.ShapeDtypeStruct((bsz, seq, d_att), BF16),
        scratch_shapes=[
            pltpu.VMEM((NBUF, d_att), BF16),
            pltpu.VMEM((NBUF, d_att), BF16),
            pltpu.VMEM((n_heads, 2, CHUNK, WIN), F32),
        ],
        compiler_params=pltpu.CompilerParams(dimension_semantics=("arbitrary", "arbitrary"),
                                             vmem_limit_bytes=VMEM_LIMIT_BYTES),
        name="chunk_attn",
    )(qkv, qkv, qkv, rbp)


def _outproj_ffn_kernel(x_ref, yr_ref, ya_ref, mod_ref, n2g_ref, fg_ref, woa_ref, wob_ref,
                        wu_ref, fcw_ref, fcb_ref, wd_ref, o_ref, ub, halo_s, z_s, x1_s,
                        *, d_model, d_ff, conv_w):
    i = pl.program_id(0)
    rows = TT * SUBLANES
    halo = (conv_w - 1) * SUBLANES

    @pl.when(i == 0)
    def _():
        halo_s[...] = jnp.zeros_like(halo_s)

    yr = yr_ref[...].reshape(rows, yr_ref.shape[2])
    ya = ya_ref[...].reshape(rows, ya_ref.shape[2])
    y = (jnp.dot(yr, woa_ref[...], preferred_element_type=F32)
         + jnp.dot(ya, wob_ref[...], preferred_element_type=F32))
    g1 = mod_ref[:, :, 2 * d_model:3 * d_model]
    sh2 = mod_ref[:, :, 3 * d_model:4 * d_model]
    sc2 = mod_ref[:, :, 4 * d_model:5 * d_model]
    g2 = mod_ref[:, :, 5 * d_model:6 * d_model]
    x1 = x_ref[...] + g1 * y.reshape(SUBLANES, TT, d_model)
    x1_s[...] = x1
    ms = jnp.mean(x1 * x1, axis=-1, keepdims=True)
    hn = x1 * lax.rsqrt(ms + EPS) * n2g_ref[...]
    hn = hn * (1.0 + sc2) + sh2
    hb = jnp.swapaxes(hn, 0, 1).reshape(rows, d_model).astype(BF16)

    first = True
    for c0, w in FF_SPLITS:
        act = None
        for part, off in ((0, 0), (1, d_ff)):
            lo = part * FF_W
            cols = slice(off + c0, off + c0 + w)
            ub[halo:halo + rows, lo:lo + w] = jnp.dot(hb, wu_ref[:, cols],
                                                      preferred_element_type=F32)
            ub[0:halo, lo:lo + w] = halo_s[:, cols]
            halo_s[:, cols] = ub[rows:rows + halo, lo:lo + w]
            u = fcb_ref[:, cols]
            for k in range(conv_w):
                u = u + ub[k * SUBLANES:k * SUBLANES + rows, lo:lo + w] * fcw_ref[k:k + 1, cols]
            act = u * jax.nn.sigmoid(u) if part == 0 else act * u
        zc = jnp.dot(act.astype(BF16), wd_ref[c0:c0 + w, :], preferred_element_type=F32)
        if first:
            z_s[...] = zc
            first = False
        else:
            z_s[...] += zc

    z = jnp.swapaxes(z_s[...].reshape(TT, SUBLANES, d_model), 0, 1)
    x2 = x1_s[...] + g2 * z
    ms2 = jnp.mean(x2 * x2, axis=-1, keepdims=True)
    o_ref[...] = x2 * lax.rsqrt(ms2 + EPS) * fg_ref[...]


def _outproj_ffn(x, yr, ya, mod, n2g, fg, wo_a, wo_b, w_up_b, fcw, fcb, w_down_b):
    bsz, seq, d = x.shape
    d_ff = w_down_b.shape[0]
    cw = fcw.shape[0]
    d_rnn = yr.shape[2]
    d_att = ya.shape[2]
    rows = TT * SUBLANES
    assert sum(w for _, w in FF_SPLITS) == d_ff
    const = lambda i: (0, 0)
    kern = functools.partial(_outproj_ffn_kernel, d_model=d, d_ff=d_ff, conv_w=cw)
    return pl.pallas_call(
        kern,
        grid=(seq // TT,),
        in_specs=[
            pl.BlockSpec((bsz, TT, d), lambda i: (0, i, 0)),
            pl.BlockSpec((bsz, TT, d_rnn), lambda i: (0, i, 0)),
            pl.BlockSpec((bsz, TT, d_att), lambda i: (0, i, 0)),
            _single(mod.shape, lambda i: (0, 0, 0)),
            _single((1, d), const),
            _single((1, d), const),
            _single(wo_a.shape, const),
            _single(wo_b.shape, const),
            _single(w_up_b.shape, const),
            _single(fcw.shape, const),
            _single((1, 2 * d_ff), const),
            _single(w_down_b.shape, const),
        ],
        out_specs=pl.BlockSpec((bsz, TT, d), lambda i: (0, i, 0)),
        out_shape=jax.ShapeDtypeStruct((bsz, seq, d), F32),
        scratch_shapes=[
            pltpu.VMEM((rows + (cw - 1) * SUBLANES, 2 * FF_W), F32),
            pltpu.VMEM(((cw - 1) * SUBLANES, 2 * d_ff), F32),
            pltpu.VMEM((rows, d), F32),
            pltpu.VMEM((bsz, TT, d), F32),
        ],
        compiler_params=pltpu.CompilerParams(dimension_semantics=("arbitrary",),
                                             vmem_limit_bytes=VMEM_LIMIT_BYTES),
        name="outproj_ffn",
    )(x, yr, ya, mod, n2g.reshape(1, d), fg.reshape(1, d), wo_a, wo_b, w_up_b, fcw,
      fcb.reshape(1, 2 * d_ff), w_down_b)


def _gate_weights(wa, wx):
    nb = wa.shape[0]
    hb = nb // 2
    halves = []
    for s in (slice(0, hb), slice(hb, nb)):
        halves.append(jnp.concatenate([block_diag(*wa[s]), block_diag(*wx[s])], axis=1).astype(BF16))
    return halves


def kernel(x, c, ada_w, ada_b, norm1_g, w_in, rnn_conv_w, rnn_conv_b, rg_wa, rg_ba, rg_wx, rg_bx, rg_lambda, rel_bias, w_out, norm2_g, w_up, ffn_conv_w, ffn_conv_b, w_down, final_g):
    bsz, seq, d = x.shape
    assert ada_w.shape[0] == 1
    l = 0
    d_rnn = rnn_conv_w.shape[2]
    d_att = w_out.shape[1] - d_rnn
    mod = _ada(c, ada_w[l], ada_b[l]).reshape(bsz, 1, ada_w.shape[2])
    wg0, wg1 = _gate_weights(rg_wa[l], rg_wx[l])
    qkv, yr = _inproj_rglru(x, mod, norm1_g[l], w_in[l].astype(BF16), rnn_conv_w[l],
                            rnn_conv_b[l], wg0, wg1, rg_ba[l], rg_bx[l], rg_lambda[l])
    rbp = jnp.pad(rel_bias[l], ((0, 0), (0, RBPAD - rel_bias.shape[2])))
    ya = _attention(qkv, rbp)
    w_out_b = w_out[l].astype(BF16)
    return _outproj_ffn(x, yr, ya, mod, norm2_g[l], final_g,
                        w_out_b[0:d_rnn], w_out_b[d_rnn:], w_up[l].astype(BF16), ffn_conv_w[l],
                        ffn_conv_b[l], w_down[l].astype(BF16))
```

```python
import functools

import jax
import jax.numpy as jnp
from jax import lax
from jax.experimental import pallas as pl
from jax.experimental.pallas import tpu as pltpu
from jax.scipy.linalg import block_diag

F32 = jnp.float32
BF16 = jnp.bfloat16

CHUNK = 64
LOOKBACK = 8
REL_CLIP = 128
HEAD_DIM = 64
RG_C = 8.0
EPS = 1e-6
NEG_INF = -1e30
LOG2E = 1.4426950408889634

SUBLANES = 8
LANES = 128
MXU_DIM = 256
VMEM_LIMIT_BYTES = 56 * 1024 * 1024

TT = 64
CHUNKS_PER_TILE = 4
TQ = CHUNKS_PER_TILE * CHUNK
HEADS_PER_VREG = LANES // HEAD_DIM
HIST = LOOKBACK * CHUNK
NBUF = HIST + TQ
WIN = (LOOKBACK + 2) * CHUNK
GPAD = 768
RBPAD = 384
FF_SPLITS = ((0, 1024), (1024, 1024), (2048, 768))
FF_W = 1024


def _single(block_shape, index_map):
    return pl.BlockSpec(block_shape, index_map, pipeline_mode=pl.Buffered(1))


def _ada_kernel(c_ref, w_ref, b_ref, o_ref):
    c = c_ref[...]
    sc = c * jax.nn.sigmoid(c)
    o_ref[...] = jnp.dot(sc, w_ref[...], precision=lax.Precision.HIGHEST,
                         preferred_element_type=F32) + b_ref[...]


def _ada(c, ada_w, ada_b):
    bsz, d = c.shape
    n = ada_w.shape[1]
    bn = 2048
    return pl.pallas_call(
        _ada_kernel,
        grid=(n // bn,),
        in_specs=[pl.BlockSpec((bsz, d), lambda j: (0, 0)),
                  pl.BlockSpec((d, bn), lambda j: (0, j)),
                  pl.BlockSpec((1, bn), lambda j: (0, j))],
        out_specs=pl.BlockSpec((bsz, bn), lambda j: (0, j)),
        out_shape=jax.ShapeDtypeStruct((bsz, n), F32),
        name="ada_mod",
    )(c, ada_w, ada_b.reshape(1, n))


def _inproj_rglru_kernel(x_ref, mod_ref, n1g_ref, win_ref, cw_ref, cb_ref, wg0_ref, wg1_ref,
                         ba_ref, bx_ref, lam_ref, qkv_ref, yr_ref,
                         xrbuf, a_s, b_s, h_s, hc, *, d_model, d_rnn, d_att, conv_w):
    i = pl.program_id(0)
    rows = TT * SUBLANES
    halo = (conv_w - 1) * SUBLANES

    @pl.when(i == 0)
    def _():
        xrbuf[0:halo, :] = jnp.zeros((halo, d_rnn), F32)
        hc[...] = jnp.zeros_like(hc)

    x = x_ref[...]
    ms = jnp.mean(x * x, axis=-1, keepdims=True)
    y = x * lax.rsqrt(ms + EPS) * n1g_ref[...]
    sh1 = mod_ref[:, :, 0:d_model]
    sc1 = mod_ref[:, :, d_model:2 * d_model]
    h = y * (1.0 + sc1) + sh1
    hb = h.reshape(rows, d_model).astype(BF16)
    proj = jnp.dot(hb, win_ref[...], preferred_element_type=F32)

    o_q = 2 * d_rnn
    q = proj[:, o_q:o_q + d_att] * (LOG2E * HEAD_DIM ** -0.5)
    qkv_ref[:, :, 0:d_att] = q.reshape(SUBLANES, TT, d_att).astype(BF16)
    kv = proj[:, o_q + d_att:o_q + 3 * d_att]
    qkv_ref[:, :, d_att:3 * d_att] = kv.reshape(SUBLANES, TT, 2 * d_att).astype(BF16)

    rg = proj[:, 0:2 * d_rnn].reshape(SUBLANES, TT, 2 * d_rnn)
    rg = jnp.swapaxes(rg, 0, 1).reshape(rows, 2 * d_rnn)
    gr = rg[:, d_rnn:2 * d_rnn]
    xrbuf[halo:halo + rows, :] = rg[:, 0:d_rnn]
    xc = cb_ref[...]
    for k in range(conv_w):
        xc = xc + xrbuf[k * SUBLANES:k * SUBLANES + rows, :] * cw_ref[k:k + 1, :]
    xrbuf[0:halo, :] = xrbuf[rows:rows + halo, :]

    xcb = xc.astype(BF16)
    half = d_rnn // 2
    g0 = jnp.dot(xcb[:, 0:half], wg0_ref[...], preferred_element_type=F32)
    g1 = jnp.dot(xcb[:, half:d_rnn], wg1_ref[...], preferred_element_type=F32)
    r = jax.nn.sigmoid(jnp.concatenate([g0[:, 0:half], g1[:, 0:half]], axis=1) + ba_ref[...])
    ig = jax.nn.sigmoid(jnp.concatenate([g0[:, half:], g1[:, half:]], axis=1) + bx_ref[...])
    lam = lam_ref[...]
    log_sig = jnp.minimum(lam, 0.0) - jnp.log1p(jnp.exp(-jnp.abs(lam)))
    log_a = RG_C * r * log_sig
    a = jnp.exp(log_a)
    mult = jnp.sqrt(jnp.tanh(-log_a) * (a * a + 1.0))
    bterm = mult * (ig * xc)
    a_s[...] = a.reshape(TT, SUBLANES, d_rnn)
    b_s[...] = bterm.reshape(TT, SUBLANES, d_rnn)

    def step(t, hprev):
        hnew = a_s[t] * hprev + b_s[t]
        h_s[t] = hnew
        return hnew

    hc[...] = lax.fori_loop(0, TT, step, hc[...], unroll=8)
    yr = h_s[...] * jax.nn.gelu(gr).reshape(TT, SUBLANES, d_rnn)
    yr_ref[...] = jnp.swapaxes(yr, 0, 1).astype(BF16)


def _inproj_rglru(x, mod, n1g, w_in_b, conv_w, conv_b, wg0, wg1, ba, bx, lam):
    bsz, seq, d = x.shape
    assert bsz == SUBLANES and seq % TT == 0
    d_in = w_in_b.shape[1]
    d_rnn = conv_w.shape[1]
    d_att = (d_in - 2 * d_rnn) // 3
    cw = conv_w.shape[0]
    rows = TT * SUBLANES
    const = lambda i: (0, 0)
    kern = functools.partial(_inproj_rglru_kernel, d_model=d, d_rnn=d_rnn, d_att=d_att, conv_w=cw)
    return pl.pallas_call(
        kern,
        grid=(seq // TT,),
        in_specs=[
            pl.BlockSpec((bsz, TT, d), lambda i: (0, i, 0)),
            _single(mod.shape, lambda i: (0, 0, 0)),
            _single((1, d), const),
            _single(w_in_b.shape, const),
            _single(conv_w.shape, const),
            _single((1, d_rnn), const),
            _single(wg0.shape, const),
            _single(wg1.shape, const),
            _single((1, d_rnn), const),
            _single((1, d_rnn), const),
            _single((1, d_rnn), const),
        ],
        out_specs=[pl.BlockSpec((bsz, TT, 3 * d_att), lambda i: (0, i, 0)),
                   pl.BlockSpec((bsz, TT, d_rnn), lambda i: (0, i, 0))],
        out_shape=[jax.ShapeDtypeStruct((bsz, seq, 3 * d_att), BF16),
                   jax.ShapeDtypeStruct((bsz, seq, d_rnn), BF16)],
        scratch_shapes=[
            pltpu.VMEM((rows + (cw - 1) * SUBLANES, d_rnn), F32),
            pltpu.VMEM((TT, SUBLANES, d_rnn), F32),
            pltpu.VMEM((TT, SUBLANES, d_rnn), F32),
            pltpu.VMEM((TT, SUBLANES, d_rnn), F32),
            pltpu.VMEM((SUBLANES, d_rnn), F32),
        ],
        compiler_params=pltpu.CompilerParams(dimension_semantics=("arbitrary",),
                                             vmem_limit_bytes=VMEM_LIMIT_BYTES),
        name="inproj_rglru",
    )(x, mod, n1g.reshape(1, d), w_in_b, conv_w, conv_b.reshape(1, d_rnn), wg0, wg1,
      ba.reshape(1, d_rnn), bx.reshape(1, d_rnn), lam.reshape(1, d_rnn))


def _build_bias_tables(rbp_ref, bias_s, n_heads):
    r_io = lax.broadcasted_iota(jnp.int32, (RBPAD, GPAD), 0)
    j_io = lax.broadcasted_iota(jnp.int32, (RBPAD, GPAD), 1)
    idx = jnp.clip(HIST + CHUNK - 1 - j_io, -REL_CLIP, REL_CLIP) + REL_CLIP
    sel = (r_io == idx).astype(F32)
    g = jnp.dot(rbp_ref[...], sel, precision=lax.Precision.HIGHEST, preferred_element_type=F32)
    g = (g - g[:, 0:1]) * LOG2E
    col = lax.broadcasted_iota(jnp.int32, (CHUNK, WIN), 1)
    band = (LOOKBACK + 1) * CHUNK
    for h in range(n_heads):
        gh = jnp.broadcast_to(g[h:h + 1, :], (CHUNK, GPAD))
        even = pltpu.roll(gh, GPAD - (CHUNK - 1), 1, stride=1, stride_axis=0)[:, 0:WIN]
        odd = pltpu.roll(gh, 1, 1, stride=1, stride_axis=0)[:, 0:WIN]
        bias_s[h, 0] = jnp.where(col < band, even, NEG_INF)
        bias_s[h, 1] = jnp.where(col >= CHUNK, odd, NEG_INF)


ZERO_BIAS_COLS = ((0, HIST - REL_CLIP), (LANES, HIST - REL_CLIP))


def _attn_tile(q_ref, o_ref, kbuf, vbuf, bias_s, first_valid, n_heads):
    lane = lax.broadcasted_iota(jnp.int32, (1, LANES), 1)
    pad = jnp.zeros((CHUNK, NBUF - WIN), BF16)
    for pair in range(n_heads // HEADS_PER_VREG):
        lanes = slice(pair * LANES, (pair + 1) * LANES)
        qp = q_ref[:, lanes]
        kp = kbuf[:, lanes]
        vp = vbuf[:, lanes]
        o_pair = None
        for hh in range(HEADS_PER_VREG):
            h = pair * HEADS_PER_VREG + hh
            in_head = jnp.logical_and(lane >= hh * HEAD_DIM, lane < (hh + 1) * HEAD_DIM)
            qm = jnp.where(in_head, qp, jnp.zeros_like(qp))
            s_all = lax.dot_general(qm, kp, (((1,), (1,)), ((), ())),
                                    preferred_element_type=F32)
            inv_l = []
            p_rows = []
            for c in range(CHUNKS_PER_TILE):
                w0 = (c // 2) * LANES
                raw = s_all[c * CHUNK:(c + 1) * CHUNK, w0:w0 + WIN]
                z0, z1 = ZERO_BIAS_COLS[c % 2]
                parts = [raw[:, z0:z1], raw[:, z1:WIN] + bias_s[h, c % 2, :, z1:WIN]]
                if z0:
                    parts.insert(0, raw[:, 0:z0] + bias_s[h, c % 2, :, 0:z0])
                s = jnp.concatenate(parts, axis=1)
                if first_valid is not None:
                    col = lax.broadcasted_iota(jnp.int32, (1, WIN), 1) + w0
                    s = jnp.where(col >= first_valid, s, NEG_INF)
                m = jnp.max(s, axis=-1, keepdims=True)
                p = jnp.exp2(s - m)
                inv_l.append(1.0 / jnp.sum(p, axis=-1, keepdims=True))
                pb = p.astype(BF16)
                p_rows.append(jnp.concatenate([pb, pad] if w0 == 0 else [pad, pb], axis=1))
            pmat = jnp.concatenate(p_rows, axis=0)
            o = jnp.dot(pmat, vp, preferred_element_type=F32) * jnp.concatenate(inv_l, axis=0)
            o_pair = o if o_pair is None else jnp.where(in_head, o, o_pair)
        o_ref[:, lanes] = o_pair.astype(BF16)


def _attn_kernel(q_ref, k_ref, v_ref, rbp_ref, o_ref, kbuf, vbuf, bias_s, *, n_heads):
    b = pl.program_id(0)
    t = pl.program_id(1)
    d_att = n_heads * HEAD_DIM

    @pl.when(jnp.logical_and(b == 0, t == 0))
    def _():
        _build_bias_tables(rbp_ref, bias_s, n_heads)

    @pl.when(t == 0)
    def _():
        kbuf[0:HIST, :] = jnp.zeros((HIST, d_att), BF16)
        vbuf[0:HIST, :] = jnp.zeros((HIST, d_att), BF16)

    kbuf[HIST:NBUF, :] = k_ref[...]
    vbuf[HIST:NBUF, :] = v_ref[...]

    @pl.when(t < HIST // TQ)
    def _():
        first_valid = CHUNK * (LOOKBACK - CHUNKS_PER_TILE * t)
        _attn_tile(q_ref, o_ref, kbuf, vbuf, bias_s, first_valid, n_heads)

    @pl.when(t >= HIST // TQ)
    def _():
        _attn_tile(q_ref, o_ref, kbuf, vbuf, bias_s, None, n_heads)

    kbuf[0:HIST, :] = kbuf[TQ:NBUF, :]
    vbuf[0:HIST, :] = vbuf[TQ:NBUF, :]


def _attention(qkv, rbp):
    bsz, seq, d3 = qkv.shape
    d_att = d3 // 3
    n_heads = d_att // HEAD_DIM
    assert seq % TQ == 0 and HIST % TQ == 0
    kern = functools.partial(_attn_kernel, n_heads=n_heads)
    return pl.pallas_call(
        kern,
        grid=(bsz, seq // TQ),
        in_specs=[
            pl.BlockSpec((None, TQ, d_att), lambda b, t: (b, t, 0)),
            pl.BlockSpec((None, TQ, d_att), lambda b, t: (b, t, 1)),
            pl.BlockSpec((None, TQ, d_att), lambda b, t: (b, t, 2)),
            _single(rbp.shape, lambda b, t: (0, 0)),
        ],
        out_specs=pl.BlockSpec((None, TQ, d_att), lambda b, t: (b, t, 0)),
        out_shape=jax.ShapeDtypeStruct((bsz, seq, d_att), BF16),
        scratch_shapes=[
            pltpu.VMEM((NBUF, d_att), BF16),
            pltpu.VMEM((NBUF, d_att), BF16),
            pltpu.VMEM((n_heads, 2, CHUNK, WIN), F32),
        ],
        compiler_params=pltpu.CompilerParams(dimension_semantics=("arbitrary", "arbitrary"),
                                             vmem_limit_bytes=VMEM_LIMIT_BYTES),
        name="chunk_attn",
    )(qkv, qkv, qkv, rbp)


def _outproj_ffn_kernel(x_ref, yr_ref, ya_ref, mod_ref, n2g_ref, fg_ref, woa_ref, wob_ref,
                        wu_ref, fcw_ref, fcb_ref, wd_ref, o_ref, ub, halo_s, z_s, x1_s,
                        *, d_model, d_ff, conv_w):
    i = pl.program_id(0)
    rows = TT * SUBLANES
    halo = (conv_w - 1) * SUBLANES

    @pl.when(i == 0)
    def _():
        halo_s[...] = jnp.zeros_like(halo_s)

    yr = yr_ref[...].reshape(rows, yr_ref.shape[2])
    ya = ya_ref[...].reshape(rows, ya_ref.shape[2])
    y = (jnp.dot(yr, woa_ref[...], preferred_element_type=F32)
         + jnp.dot(ya, wob_ref[...], preferred_element_type=F32))
    g1 = mod_ref[:, :, 2 * d_model:3 * d_model]
    sh2 = mod_ref[:, :, 3 * d_model:4 * d_model]
    sc2 = mod_ref[:, :, 4 * d_model:5 * d_model]
    g2 = mod_ref[:, :, 5 * d_model:6 * d_model]
    x1 = x_ref[...] + g1 * y.reshape(SUBLANES, TT, d_model)
    x1_s[...] = x1
    ms = jnp.mean(x1 * x1, axis=-1, keepdims=True)
    hn = x1 * lax.rsqrt(ms + EPS) * n2g_ref[...]
    hn = hn * (1.0 + sc2) + sh2
    hb = jnp.swapaxes(hn, 0, 1).reshape(rows, d_model).astype(BF16)

    first = True
    for c0, w in FF_SPLITS:
        act = None
        for part, off in ((0, 0), (1, d_ff)):
            lo = part * FF_W
            cols = slice(off + c0, off + c0 + w)
            ub[halo:halo + rows, lo:lo + w] = jnp.dot(hb, wu_ref[:, cols],
                                                      preferred_element_type=F32)
            ub[0:halo, lo:lo + w] = halo_s[:, cols]
            halo_s[:, cols] = ub[rows:rows + halo, lo:lo + w]
            u = fcb_ref[:, cols]
            for k in range(conv_w):
                u = u + ub[k * SUBLANES:k * SUBLANES + rows, lo:lo + w] * fcw_ref[k:k + 1, cols]
            act = u * jax.nn.sigmoid(u) if part == 0 else act * u
        zc = jnp.dot(act.astype(BF16), wd_ref[c0:c0 + w, :], preferred_element_type=F32)
        if first:
            z_s[...] = zc
            first = False
        else:
            z_s[...] += zc

    z = jnp.swapaxes(z_s[...].reshape(TT, SUBLANES, d_model), 0, 1)
    x2 = x1_s[...] + g2 * z
    ms2 = jnp.mean(x2 * x2, axis=-1, keepdims=True)
    o_ref[...] = x2 * lax.rsqrt(ms2 + EPS) * fg_ref[...]


def _outproj_ffn(x, yr, ya, mod, n2g, fg, wo_a, wo_b, w_up_b, fcw, fcb, w_down_b):
    bsz, seq, d = x.shape
    d_ff = w_down_b.shape[0]
    cw = fcw.shape[0]
    d_rnn = yr.shape[2]
    d_att = ya.shape[2]
    rows = TT * SUBLANES
    assert sum(w for _, w in FF_SPLITS) == d_ff
    const = lambda i: (0, 0)
    kern = functools.partial(_outproj_ffn_kernel, d_model=d, d_ff=d_ff, conv_w=cw)
    return pl.pallas_call(
        kern,
        grid=(seq // TT,),
        in_specs=[
            pl.BlockSpec((bsz, TT, d), lambda i: (0, i, 0)),
            pl.BlockSpec((bsz, TT, d_rnn), lambda i: (0, i, 0)),
            pl.BlockSpec((bsz, TT, d_att), lambda i: (0, i, 0)),
            _single(mod.shape, lambda i: (0, 0, 0)),
            _single((1, d), const),
            _single((1, d), const),
            _single(wo_a.shape, const),
            _single(wo_b.shape, const),
            _single(w_up_b.shape, const),
            _single(fcw.shape, const),
            _single((1, 2 * d_ff), const),
            _single(w_down_b.shape, const),
        ],
        out_specs=pl.BlockSpec((bsz, TT, d), lambda i: (0, i, 0)),
        out_shape=jax.ShapeDtypeStruct((bsz, seq, d), F32),
        scratch_shapes=[
            pltpu.VMEM((rows + (cw - 1) * SUBLANES, 2 * FF_W), F32),
            pltpu.VMEM(((cw - 1) * SUBLANES, 2 * d_ff), F32),
            pltpu.VMEM((rows, d), F32),
            pltpu.VMEM((bsz, TT, d), F32),
        ],
        compiler_params=pltpu.CompilerParams(dimension_semantics=("arbitrary",),
                                             vmem_limit_bytes=VMEM_LIMIT_BYTES),
        name="outproj_ffn",
    )(x, yr, ya, mod, n2g.reshape(1, d), fg.reshape(1, d), wo_a, wo_b, w_up_b, fcw,
      fcb.reshape(1, 2 * d_ff), w_down_b)


def _gate_weights(wa, wx):
    nb = wa.shape[0]
    hb = nb // 2
    halves = []
    for s in (slice(0, hb), slice(hb, nb)):
        halves.append(jnp.concatenate([block_diag(*wa[s]), block_diag(*wx[s])], axis=1).astype(BF16))
    return halves


def kernel(x, c, ada_w, ada_b, norm1_g, w_in, rnn_conv_w, rnn_conv_b, rg_wa, rg_ba, rg_wx, rg_bx, rg_lambda, rel_bias, w_out, norm2_g, w_up, ffn_conv_w, ffn_conv_b, w_down, final_g):
    bsz, seq, d = x.shape
    assert ada_w.shape[0] == 1
    l = 0
    d_rnn = rnn_conv_w.shape[2]
    d_att = w_out.shape[1] - d_rnn
    mod = _ada(c, ada_w[l], ada_b[l]).reshape(bsz, 1, ada_w.shape[2])
    wg0, wg1 = _gate_weights(rg_wa[l], rg_wx[l])
    qkv, yr = _inproj_rglru(x, mod, norm1_g[l], w_in[l].astype(BF16), rnn_conv_w[l],
                            rnn_conv_b[l], wg0, wg1, rg_ba[l], rg_bx[l], rg_lambda[l])
    rbp = jnp.pad(rel_bias[l], ((0, 0), (0, RBPAD - rel_bias.shape[2])))
    ya = _attention(qkv, rbp)
    w_out_b = w_out[l].astype(BF16)
    return _outproj_ffn(x, yr, ya, mod, norm2_g[l], final_g,
                        w_out_b[0:d_rnn], w_out_b[d_rnn:], w_up[l].astype(BF16), ffn_conv_w[l],
                        ffn_conv_b[l], w_down[l].astype(BF16))
```

```python
import functools

import jax
import jax.numpy as jnp
from jax import lax
from jax.experimental import pallas as pl
from jax.experimental.pallas import tpu as pltpu
from jax.scipy.linalg import block_diag

F32 = jnp.float32
BF16 = jnp.bfloat16

CHUNK = 64
LOOKBACK = 8
REL_CLIP = 128
HEAD_DIM = 64
RG_C = 8.0
EPS = 1e-6
NEG_INF = -1e30
LOG2E = 1.4426950408889634

SUBLANES = 8
LANES = 128
MXU_DIM = 256
VMEM_LIMIT_BYTES = 56 * 1024 * 1024

TT = 64
RT = 16
CHUNKS_PER_TILE = 4
TQ = CHUNKS_PER_TILE * CHUNK
HEADS_PER_VREG = LANES // HEAD_DIM
HIST = LOOKBACK * CHUNK
NBUF = HIST + TQ
WIN = (LOOKBACK + 2) * CHUNK
GPAD = 768
RBPAD = 384
FF_SPLITS = ((0, 1024), (1024, 1024), (2048, 768))
FF_W = 1024


def _single(block_shape, index_map):
    return pl.BlockSpec(block_shape, index_map, pipeline_mode=pl.Buffered(1))


def _ada_kernel(c_ref, w_ref, b_ref, o_ref):
    c = c_ref[...]
    sc = c * jax.nn.sigmoid(c)
    o_ref[...] = jnp.dot(sc, w_ref[...], precision=lax.Precision.HIGHEST,
                         preferred_element_type=F32) + b_ref[...]


def _ada(c, ada_w, ada_b):
    bsz, d = c.shape
    n = ada_w.shape[1]
    bn = 2048
    return pl.pallas_call(
        _ada_kernel,
        grid=(n // bn,),
        in_specs=[pl.BlockSpec((bsz, d), lambda j: (0, 0)),
                  pl.BlockSpec((d, bn), lambda j: (0, j)),
                  pl.BlockSpec((1, bn), lambda j: (0, j))],
        out_specs=pl.BlockSpec((bsz, bn), lambda j: (0, j)),
        out_shape=jax.ShapeDtypeStruct((bsz, n), F32),
        name="ada_mod",
    )(c, ada_w, ada_b.reshape(1, n))


def _inproj_rglru_kernel(x_ref, mod_ref, n1g_ref, win_ref, cw_ref, cb_ref, wg0_ref, wg1_ref,
                         ba_ref, bx_ref, lam_ref, qkv_ref, yr_ref,
                         rg_even, rg_odd, xrbuf, hc, *, d_model, d_rnn, d_att, conv_w):
    i = pl.program_id(0)
    rows = TT * SUBLANES
    halo = (conv_w - 1) * SUBLANES

    @pl.when(i == 0)
    def _():
        rg_odd[...] = jnp.zeros_like(rg_odd)

    @pl.when(i <= 1)
    def _():
        xrbuf[0:halo, :] = jnp.zeros((halo, d_rnn), F32)
        hc[...] = jnp.zeros_like(hc)

    def normed_lhs():
        x = x_ref[...]
        ms = jnp.mean(x * x, axis=-1, keepdims=True)
        y = x * lax.rsqrt(ms + EPS) * n1g_ref[...]
        sh1 = mod_ref[:, :, 0:d_model]
        sc1 = mod_ref[:, :, d_model:2 * d_model]
        h = y * (1.0 + sc1) + sh1
        return h.reshape(rows, d_model).astype(BF16)

    def project_piece(hb, j, rg_out):
        cols = slice(j * d_rnn, (j + 1) * d_rnn)
        part = jnp.dot(hb, win_ref[:, cols], preferred_element_type=F32)
        if j < 2:
            rg_out[:, :, cols] = part.reshape(SUBLANES, TT, d_rnn)
        else:
            part = part * (LOG2E * HEAD_DIM ** -0.5) if j == 2 else part
            dst = slice((j - 2) * d_att, (j - 1) * d_att)
            qkv_ref[:, :, dst] = part.reshape(SUBLANES, TT, d_att).astype(BF16)

    def recur_piece(rg_in, c):
        ts = slice(c * RT, (c + 1) * RT)
        prow = RT * SUBLANES
        rg = jnp.swapaxes(rg_in[:, ts, :], 0, 1).reshape(prow, 2 * d_rnn)
        gr = rg[:, d_rnn:2 * d_rnn]
        xrbuf[halo:halo + prow, :] = rg[:, 0:d_rnn]
        xc = cb_ref[...]
        for k in range(conv_w):
            xc = xc + xrbuf[k * SUBLANES:k * SUBLANES + prow, :] * cw_ref[k:k + 1, :]
        xrbuf[0:halo, :] = xrbuf[prow:prow + halo, :]

        xcb = xc.astype(BF16)
        half = d_rnn // 2
        g0 = jnp.dot(xcb[:, 0:half], wg0_ref[...], preferred_element_type=F32)
        g1 = jnp.dot(xcb[:, half:d_rnn], wg1_ref[...], preferred_element_type=F32)
        r = jax.nn.sigmoid(jnp.concatenate([g0[:, 0:half], g1[:, 0:half]], axis=1) + ba_ref[...])
        ig = jax.nn.sigmoid(jnp.concatenate([g0[:, half:], g1[:, half:]], axis=1) + bx_ref[...])
        lam = lam_ref[...]
        log_sig = jnp.minimum(lam, 0.0) - jnp.log1p(jnp.exp(-jnp.abs(lam)))
        log_a = RG_C * r * log_sig
        a = jnp.exp(log_a)
        mult = jnp.sqrt(jnp.tanh(-log_a) * (a * a + 1.0))
        a3 = a.reshape(RT, SUBLANES, d_rnn)
        b3 = (mult * (ig * xc)).reshape(RT, SUBLANES, d_rnn)
        hcur = hc[...]
        hs = []
        for t in range(RT):
            hcur = a3[t] * hcur + b3[t]
            hs.append(hcur)
        hc[...] = hcur
        yr = jnp.stack(hs, axis=0) * jax.nn.gelu(gr).reshape(RT, SUBLANES, d_rnn)
        yr_ref[:, ts, :] = jnp.swapaxes(yr, 0, 1).astype(BF16)

    def step_body(rg_out, rg_in):
        hb = normed_lhs()
        n_proj = (2 * d_rnn + 3 * d_att) // d_rnn
        n_rec = TT // RT
        for j in range(max(n_proj, n_rec)):
            if j < n_proj:
                project_piece(hb, j, rg_out)
            if j < n_rec:
                recur_piece(rg_in, j)

    @pl.when(i % 2 == 0)
    def _():
        step_body(rg_even, rg_odd)

    @pl.when(i % 2 == 1)
    def _():
        step_body(rg_odd, rg_even)


def _inproj_rglru(x, mod, n1g, w_in_b, conv_w, conv_b, wg0, wg1, ba, bx, lam):
    bsz, seq, d = x.shape
    assert bsz == SUBLANES and seq % TT == 0
    d_in = w_in_b.shape[1]
    d_rnn = conv_w.shape[1]
    d_att = (d_in - 2 * d_rnn) // 3
    assert d_att == d_rnn and TT % RT == 0
    cw = conv_w.shape[0]
    rows = TT * SUBLANES
    const = lambda i: (0, 0)
    kern = functools.partial(_inproj_rglru_kernel, d_model=d, d_rnn=d_rnn, d_att=d_att, conv_w=cw)
    n_t = seq // TT
    proj_tile = lambda i: (0, jnp.minimum(i, n_t - 1), 0)
    recur_tile = lambda i: (0, jnp.maximum(i - 1, 0), 0)
    return pl.pallas_call(
        kern,
        grid=(n_t + 1,),
        in_specs=[
            pl.BlockSpec((bsz, TT, d), proj_tile),
            _single(mod.shape, lambda i: (0, 0, 0)),
            _single((1, d), const),
            _single(w_in_b.shape, const),
            _single(conv_w.shape, const),
            _single((1, d_rnn), const),
            _single(wg0.shape, const),
            _single(wg1.shape, const),
            _single((1, d_rnn), const),
            _single((1, d_rnn), const),
            _single((1, d_rnn), const),
        ],
        out_specs=[pl.BlockSpec((bsz, TT, 3 * d_att), proj_tile),
                   pl.BlockSpec((bsz, TT, d_rnn), recur_tile)],
        out_shape=[jax.ShapeDtypeStruct((bsz, seq, 3 * d_att), BF16),
                   jax.ShapeDtypeStruct((bsz, seq, d_rnn), BF16)],
        scratch_shapes=[
            pltpu.VMEM((bsz, TT, 2 * d_rnn), F32),
            pltpu.VMEM((bsz, TT, 2 * d_rnn), F32),
            pltpu.VMEM((rows + (cw - 1) * SUBLANES, d_rnn), F32),
            pltpu.VMEM((SUBLANES, d_rnn), F32),
        ],
        compiler_params=pltpu.CompilerParams(dimension_semantics=("arbitrary",),
                                             vmem_limit_bytes=VMEM_LIMIT_BYTES),
        name="inproj_rglru",
    )(x, mod, n1g.reshape(1, d), w_in_b, conv_w, conv_b.reshape(1, d_rnn), wg0, wg1,
      ba.reshape(1, d_rnn), bx.reshape(1, d_rnn), lam.reshape(1, d_rnn))


def _build_bias_tables(rbp_ref, bias_s, n_heads):
    r_io = lax.broadcasted_iota(jnp.int32, (RBPAD, GPAD), 0)
    j_io = lax.broadcasted_iota(jnp.int32, (RBPAD, GPAD), 1)
    idx = jnp.clip(HIST + CHUNK - 1 - j_io, -REL_CLIP, REL_CLIP) + REL_CLIP
    sel = (r_io == idx).astype(F32)
    g = jnp.dot(rbp_ref[...], sel, precision=lax.Precision.HIGHEST, preferred_element_type=F32)
    g = (g - g[:, 0:1]) * LOG2E
    col = lax.broadcasted_iota(jnp.int32, (CHUNK, WIN), 1)
    band = (LOOKBACK + 1) * CHUNK
    for h in range(n_heads):
        gh = jnp.broadcast_to(g[h:h + 1, :], (CHUNK, GPAD))
        even = pltpu.roll(gh, GPAD - (CHUNK - 1), 1, stride=1, stride_axis=0)[:, 0:WIN]
        odd = pltpu.roll(gh, 1, 1, stride=1, stride_axis=0)[:, 0:WIN]
        bias_s[h, 0] = jnp.where(col < band, even, NEG_INF)
        bias_s[h, 1] = jnp.where(col >= CHUNK, odd, NEG_INF)


ZERO_BIAS_COLS = ((0, HIST - REL_CLIP), (LANES, HIST - REL_CLIP))


def _attn_tile(q_ref, o_ref, kbuf, vbuf, bias_s, first_valid, n_heads):
    lane = lax.broadcasted_iota(jnp.int32, (1, LANES), 1)
    pad = jnp.zeros((CHUNK, NBUF - WIN), BF16)
    for pair in range(n_heads // HEADS_PER_VREG):
        lanes = slice(pair * LANES, (pair + 1) * LANES)
        qp = q_ref[:, lanes]
        kp = kbuf[:, lanes]
        vp = vbuf[:, lanes]
        o_pair = None
        for hh in range(HEADS_PER_VREG):
            h = pair * HEADS_PER_VREG + hh
            in_head = jnp.logical_and(lane >= hh * HEAD_DIM, lane < (hh + 1) * HEAD_DIM)
            qm = jnp.where(in_head, qp, jnp.zeros_like(qp))
            s_all = lax.dot_general(qm, kp, (((1,), (1,)), ((), ())),
                                    preferred_element_type=F32)
            inv_l = []
            p_rows = []
            for c in range(CHUNKS_PER_TILE):
                w0 = (c // 2) * LANES
                raw = s_all[c * CHUNK:(c + 1) * CHUNK, w0:w0 + WIN]
                z0, z1 = ZERO_BIAS_COLS[c % 2]
                parts = [raw[:, z0:z1], raw[:, z1:WIN] + bias_s[h, c % 2, :, z1:WIN]]
                if z0:
                    parts.insert(0, raw[:, 0:z0] + bias_s[h, c % 2, :, 0:z0])
                s = jnp.concatenate(parts, axis=1)
                if first_valid is not None:
                    col = lax.broadcasted_iota(jnp.int32, (1, WIN), 1) + w0
                    s = jnp.where(col >= first_valid, s, NEG_INF)
                m = jnp.max(s, axis=-1, keepdims=True)
                p = jnp.exp2(s - m)
                inv_l.append(1.0 / jnp.sum(p, axis=-1, keepdims=True))
                pb = p.astype(BF16)
                p_rows.append(jnp.concatenate([pb, pad] if w0 == 0 else [pad, pb], axis=1))
            pmat = jnp.concatenate(p_rows, axis=0)
            o = jnp.dot(pmat, vp, preferred_element_type=F32) * jnp.concatenate(inv_l, axis=0)
            o_pair = o if o_pair is None else jnp.where(in_head, o, o_pair)
        o_ref[:, lanes] = o_pair.astype(BF16)


def _attn_kernel(q_ref, k_ref, v_ref, rbp_ref, o_ref, kbuf, vbuf, bias_s, *, n_heads):
    b = pl.program_id(0)
    t = pl.program_id(1)
    d_att = n_heads * HEAD_DIM

    @pl.when(jnp.logical_and(b == 0, t == 0))
    def _():
        _build_bias_tables(rbp_ref, bias_s, n_heads)

    @pl.when(t == 0)
    def _():
        kbuf[0:HIST, :] = jnp.zeros((HIST, d_att), BF16)
        vbuf[0:HIST, :] = jnp.zeros((HIST, d_att), BF16)

    kbuf[HIST:NBUF, :] = k_ref[...]
    vbuf[HIST:NBUF, :] = v_ref[...]

    @pl.when(t < HIST // TQ)
    def _():
        first_valid = CHUNK * (LOOKBACK - CHUNKS_PER_TILE * t)
        _attn_tile(q_ref, o_ref, kbuf, vbuf, bias_s, first_valid, n_heads)

    @pl.when(t >= HIST // TQ)
    def _():
        _attn_tile(q_ref, o_ref, kbuf, vbuf, bias_s, None, n_heads)

    kbuf[0:HIST, :] = kbuf[TQ:NBUF, :]
    vbuf[0:HIST, :] = vbuf[TQ:NBUF, :]


def _attention(qkv, rbp):
    bsz, seq, d3 = qkv.shape
    d_att = d3 // 3
    n_heads = d_att // HEAD_DIM
    assert seq % TQ == 0 and HIST % TQ == 0
    kern = functools.partial(_attn_kernel, n_heads=n_heads)
    return pl.pallas_call(
        kern,
        grid=(bsz, seq // TQ),
        in_specs=[
            pl.BlockSpec((None, TQ, d_att), lambda b, t: (b, t, 0)),
            pl.BlockSpec((None, TQ, d_att), lambda b, t: (b, t, 1)),
            pl.BlockSpec((None, TQ, d_att), lambda b, t: (b, t, 2)),
            _single(rbp.shape, lambda b, t: (0, 0)),
        ],
        out_specs=pl.BlockSpec((None, TQ, d_att), lambda b, t: (b, t, 0)),
        out_shape=jax.ShapeDtypeStruct((bsz, seq, d_att), BF16),
        scratch_shapes=[
            pltpu.VMEM((NBUF, d_att), BF16),
            pltpu.VMEM((NBUF, d_att), BF16),
            pltpu.VMEM((n_heads, 2, CHUNK, WIN), F32),
        ],
        compiler_params=pltpu.CompilerParams(dimension_semantics=("arbitrary", "arbitrary"),
                                             vmem_limit_bytes=VMEM_LIMIT_BYTES),
        name="chunk_attn",
    )(qkv, qkv, qkv, rbp)


def _outproj_ffn_kernel(x_ref, yr_ref, ya_ref, mod_ref, n2g_ref, fg_ref, woa_ref, wob_ref,
                        wu_ref, fcw_ref, fcb_ref, wd_ref, o_ref, ub, halo_s, z_s, x1_s,
                        *, d_model, d_ff, conv_w):
    i = pl.program_id(0)
    rows = TT * SUBLANES
    halo = (conv_w - 1) * SUBLANES

    @pl.when(i == 0)
    def _():
        halo_s[...] = jnp.zeros_like(halo_s)

    yr = yr_ref[...].reshape(rows, yr_ref.shape[2])
    ya = ya_ref[...].reshape(rows, ya_ref.shape[2])
    y = (jnp.dot(yr, woa_ref[...], preferred_element_type=F32)
         + jnp.dot(ya, wob_ref[...], preferred_element_type=F32))
    g1 = mod_ref[:, :, 2 * d_model:3 * d_model]
    sh2 = mod_ref[:, :, 3 * d_model:4 * d_model]
    sc2 = mod_ref[:, :, 4 * d_model:5 * d_model]
    g2 = mod_ref[:, :, 5 * d_model:6 * d_model]
    x1 = x_ref[...] + g1 * y.reshape(SUBLANES, TT, d_model)
    x1_s[...] = x1
    ms = jnp.mean(x1 * x1, axis=-1, keepdims=True)
    hn = x1 * lax.rsqrt(ms + EPS) * n2g_ref[...]
    hn = hn * (1.0 + sc2) + sh2
    hb = jnp.swapaxes(hn, 0, 1).reshape(rows, d_model).astype(BF16)

    first = True
    for c0, w in FF_SPLITS:
        act = None
        for part, off in ((0, 0), (1, d_ff)):
            lo = part * FF_W
            cols = slice(off + c0, off + c0 + w)
            ub[halo:halo + rows, lo:lo + w] = jnp.dot(hb, wu_ref[:, cols],
                                                      preferred_element_type=F32)
            ub[0:halo, lo:lo + w] = halo_s[:, cols]
            halo_s[:, cols] = ub[rows:rows + halo, lo:lo + w]
            u = fcb_ref[:, cols]
            for k in range(conv_w):
                u = u + ub[k * SUBLANES:k * SUBLANES + rows, lo:lo + w] * fcw_ref[k:k + 1, cols]
            act = u * jax.nn.sigmoid(u) if part == 0 else act * u
        zc = jnp.dot(act.astype(BF16), wd_ref[c0:c0 + w, :], preferred_element_type=F32)
        if first:
            z_s[...] = zc
            first = False
        else:
            z_s[...] += zc

    z = jnp.swapaxes(z_s[...].reshape(TT, SUBLANES, d_model), 0, 1)
    x2 = x1_s[...] + g2 * z
    ms2 = jnp.mean(x2 * x2, axis=-1, keepdims=True)
    o_ref[...] = x2 * lax.rsqrt(ms2 + EPS) * fg_ref[...]


def _outproj_ffn(x, yr, ya, mod, n2g, fg, wo_a, wo_b, w_up_b, fcw, fcb, w_down_b):
    bsz, seq, d = x.shape
    d_ff = w_down_b.shape[0]
    cw = fcw.shape[0]
    d_rnn = yr.shape[2]
    d_att = ya.shape[2]
    rows = TT * SUBLANES
    assert sum(w for _, w in FF_SPLITS) == d_ff
    const = lambda i: (0, 0)
    kern = functools.partial(_outproj_ffn_kernel, d_model=d, d_ff=d_ff, conv_w=cw)
    return pl.pallas_call(
        kern,
        grid=(seq // TT,),
        in_specs=[
            pl.BlockSpec((bsz, TT, d), lambda i: (0, i, 0)),
            pl.BlockSpec((bsz, TT, d_rnn), lambda i: (0, i, 0)),
            pl.BlockSpec((bsz, TT, d_att), lambda i: (0, i, 0)),
            _single(mod.shape, lambda i: (0, 0, 0)),
            _single((1, d), const),
            _single((1, d), const),
            _single(wo_a.shape, const),
            _single(wo_b.shape, const),
            _single(w_up_b.shape, const),
            _single(fcw.shape, const),
            _single((1, 2 * d_ff), const),
            _single(w_down_b.shape, const),
        ],
        out_specs=pl.BlockSpec((bsz, TT, d), lambda i: (0, i, 0)),
        out_shape=jax.ShapeDtypeStruct((bsz, seq, d), F32),
        scratch_shapes=[
            pltpu.VMEM((rows + (cw - 1) * SUBLANES, 2 * FF_W), F32),
            pltpu.VMEM(((cw - 1) * SUBLANES, 2 * d_ff), F32),
            pltpu.VMEM((rows, d), F32),
            pltpu.VMEM((bsz, TT, d), F32),
        ],
        compiler_params=pltpu.CompilerParams(dimension_semantics=("arbitrary",),
                                             vmem_limit_bytes=VMEM_LIMIT_BYTES),
        name="outproj_ffn",
    )(x, yr, ya, mod, n2g.reshape(1, d), fg.reshape(1, d), wo_a, wo_b, w_up_b, fcw,
      fcb.reshape(1, 2 * d_ff), w_down_b)


def _gate_weights(wa, wx):
    nb = wa.shape[0]
    hb = nb // 2
    halves = []
    for s in (slice(0, hb), slice(hb, nb)):
        halves.append(jnp.concatenate([block_diag(*wa[s]), block_diag(*wx[s])], axis=1).astype(BF16))
    return halves


def kernel(x, c, ada_w, ada_b, norm1_g, w_in, rnn_conv_w, rnn_conv_b, rg_wa, rg_ba, rg_wx, rg_bx, rg_lambda, rel_bias, w_out, norm2_g, w_up, ffn_conv_w, ffn_conv_b, w_down, final_g):
    bsz, seq, d = x.shape
    assert ada_w.shape[0] == 1
    l = 0
    d_rnn = rnn_conv_w.shape[2]
    d_att = w_out.shape[1] - d_rnn
    mod = _ada(c, ada_w[l], ada_b[l]).reshape(bsz, 1, ada_w.shape[2])
    wg0, wg1 = _gate_weights(rg_wa[l], rg_wx[l])
    qkv, yr = _inproj_rglru(x, mod, norm1_g[l], w_in[l].astype(BF16), rnn_conv_w[l],
                            rnn_conv_b[l], wg0, wg1, rg_ba[l], rg_bx[l], rg_lambda[l])
    rbp = jnp.pad(rel_bias[l], ((0, 0), (0, RBPAD - rel_bias.shape[2])))
    ya = _attention(qkv, rbp)
    w_out_b = w_out[l].astype(BF16)
    return _outproj_ffn(x, yr, ya, mod, norm2_g[l], final_g,
                        w_out_b[0:d_rnn], w_out_b[d_rnn:], w_up[l].astype(BF16), ffn_conv_w[l],
                        ffn_conv_b[l], w_down[l].astype(BF16))
```

```python
import functools

import jax
import jax.numpy as jnp
from jax import lax
from jax.experimental import pallas as pl
from jax.experimental.pallas import tpu as pltpu
from jax.scipy.linalg import block_diag

F32 = jnp.float32
BF16 = jnp.bfloat16

CHUNK = 64
LOOKBACK = 8
REL_CLIP = 128
HEAD_DIM = 64
RG_C = 8.0
EPS = 1e-6
NEG_INF = -1e30
LOG2E = 1.4426950408889634

SUBLANES = 8
LANES = 128
MXU_DIM = 256
VMEM_LIMIT_BYTES = 56 * 1024 * 1024

TT = 64
CHUNKS_PER_TILE = 4
TQ = CHUNKS_PER_TILE * CHUNK
HEADS_PER_VREG = LANES // HEAD_DIM
HIST = LOOKBACK * CHUNK
NBUF = HIST + TQ
WIN = (LOOKBACK + 2) * CHUNK
GPAD = 768
RBPAD = 384
FF_SPLITS = ((0, 1024), (1024, 1024), (2048, 768))
FF_W = 1024


def _single(block_shape, index_map):
    return pl.BlockSpec(block_shape, index_map, pipeline_mode=pl.Buffered(1))


def _ada_kernel(c_ref, w_ref, b_ref, o_ref):
    c = c_ref[...]
    sc = c * jax.nn.sigmoid(c)
    o_ref[...] = jnp.dot(sc, w_ref[...], precision=lax.Precision.HIGHEST,
                         preferred_element_type=F32) + b_ref[...]


def _ada(c, ada_w, ada_b):
    bsz, d = c.shape
    n = ada_w.shape[1]
    bn = 2048
    return pl.pallas_call(
        _ada_kernel,
        grid=(n // bn,),
        in_specs=[pl.BlockSpec((bsz, d), lambda j: (0, 0)),
                  pl.BlockSpec((d, bn), lambda j: (0, j)),
                  pl.BlockSpec((1, bn), lambda j: (0, j))],
        out_specs=pl.BlockSpec((bsz, bn), lambda j: (0, j)),
        out_shape=jax.ShapeDtypeStruct((bsz, n), F32),
        name="ada_mod",
    )(c, ada_w, ada_b.reshape(1, n))


def _inproj_rglru_kernel(x_ref, mod_ref, n1g_ref, win_ref, cw_ref, cb_ref, wg0_ref, wg1_ref,
                         ba_ref, bx_ref, lam_ref, qkv_ref, yr_ref,
                         xrbuf, a_s, b_s, h_s, hc, *, d_model, d_rnn, d_att, conv_w):
    i = pl.program_id(0)
    rows = TT * SUBLANES
    halo = (conv_w - 1) * SUBLANES

    @pl.when(i == 0)
    def _():
        xrbuf[0:halo, :] = jnp.zeros((halo, d_rnn), F32)
        hc[...] = jnp.zeros_like(hc)

    x = x_ref[...]
    ms = jnp.mean(x * x, axis=-1, keepdims=True)
    y = x * lax.rsqrt(ms + EPS) * n1g_ref[...]
    sh1 = mod_ref[:, :, 0:d_model]
    sc1 = mod_ref[:, :, d_model:2 * d_model]
    h = y * (1.0 + sc1) + sh1
    hb = h.reshape(rows, d_model).astype(BF16)
    proj = jnp.dot(hb, win_ref[...], preferred_element_type=F32)

    o_q = 2 * d_rnn
    q = proj[:, o_q:o_q + d_att] * (LOG2E * HEAD_DIM ** -0.5)
    qkv_ref[:, :, 0:d_att] = q.reshape(SUBLANES, TT, d_att).astype(BF16)
    kv = proj[:, o_q + d_att:o_q + 3 * d_att]
    qkv_ref[:, :, d_att:3 * d_att] = kv.reshape(SUBLANES, TT, 2 * d_att).astype(BF16)

    rg = proj[:, 0:2 * d_rnn].reshape(SUBLANES, TT, 2 * d_rnn)
    rg = jnp.swapaxes(rg, 0, 1).reshape(rows, 2 * d_rnn)
    gr = rg[:, d_rnn:2 * d_rnn]
    xrbuf[halo:halo + rows, :] = rg[:, 0:d_rnn]
    xc = cb_ref[...]
    for k in range(conv_w):
        xc = xc + xrbuf[k * SUBLANES:k * SUBLANES + rows, :] * cw_ref[k:k + 1, :]
    xrbuf[0:halo, :] = xrbuf[rows:rows + halo, :]

    xcb = xc.astype(BF16)
    half = d_rnn // 2
    g0 = jnp.dot(xcb[:, 0:half], wg0_ref[...], preferred_element_type=F32)
    g1 = jnp.dot(xcb[:, half:d_rnn], wg1_ref[...], preferred_element_type=F32)
    r = jax.nn.sigmoid(jnp.concatenate([g0[:, 0:half], g1[:, 0:half]], axis=1) + ba_ref[...])
    ig = jax.nn.sigmoid(jnp.concatenate([g0[:, half:], g1[:, half:]], axis=1) + bx_ref[...])
    lam = lam_ref[...]
    log_sig = jnp.minimum(lam, 0.0) - jnp.log1p(jnp.exp(-jnp.abs(lam)))
    log_a = RG_C * r * log_sig
    a = jnp.exp(log_a)
    mult = jnp.sqrt(jnp.tanh(-log_a) * (a * a + 1.0))
    bterm = mult * (ig * xc)
    a_s[...] = a.reshape(TT, SUBLANES, d_rnn)
    b_s[...] = bterm.reshape(TT, SUBLANES, d_rnn)

    def step(t, hprev):
        hnew = a_s[t] * hprev + b_s[t]
        h_s[t] = hnew
        return hnew

    hc[...] = lax.fori_loop(0, TT, step, hc[...], unroll=8)
    yr = h_s[...] * jax.nn.gelu(gr).reshape(TT, SUBLANES, d_rnn)
    yr_ref[...] = jnp.swapaxes(yr, 0, 1).astype(BF16)


def _inproj_rglru(x, mod, n1g, w_in_b, conv_w, conv_b, wg0, wg1, ba, bx, lam):
    bsz, seq, d = x.shape
    assert bsz == SUBLANES and seq % TT == 0
    d_in = w_in_b.shape[1]
    d_rnn = conv_w.shape[1]
    d_att = (d_in - 2 * d_rnn) // 3
    cw = conv_w.shape[0]
    rows = TT * SUBLANES
    const = lambda i: (0, 0)
    kern = functools.partial(_inproj_rglru_kernel, d_model=d, d_rnn=d_rnn, d_att=d_att, conv_w=cw)
    return pl.pallas_call(
        kern,
        grid=(seq // TT,),
        in_specs=[
            pl.BlockSpec((bsz, TT, d), lambda i: (0, i, 0)),
            _single(mod.shape, lambda i: (0, 0, 0)),
            _single((1, d), const),
            _single(w_in_b.shape, const),
            _single(conv_w.shape, const),
            _single((1, d_rnn), const),
            _single(wg0.shape, const),
            _single(wg1.shape, const),
            _single((1, d_rnn), const),
            _single((1, d_rnn), const),
            _single((1, d_rnn), const),
        ],
        out_specs=[pl.BlockSpec((bsz, TT, 3 * d_att), lambda i: (0, i, 0)),
                   pl.BlockSpec((bsz, TT, d_rnn), lambda i: (0, i, 0))],
        out_shape=[jax.ShapeDtypeStruct((bsz, seq, 3 * d_att), BF16),
                   jax.ShapeDtypeStruct((bsz, seq, d_rnn), BF16)],
        scratch_shapes=[
            pltpu.VMEM((rows + (cw - 1) * SUBLANES, d_rnn), F32),
            pltpu.VMEM((TT, SUBLANES, d_rnn), F32),
            pltpu.VMEM((TT, SUBLANES, d_rnn), F32),
            pltpu.VMEM((TT, SUBLANES, d_rnn), F32),
            pltpu.VMEM((SUBLANES, d_rnn), F32),
        ],
        compiler_params=pltpu.CompilerParams(dimension_semantics=("arbitrary",),
                                             vmem_limit_bytes=VMEM_LIMIT_BYTES),
        name="inproj_rglru",
    )(x, mod, n1g.reshape(1, d), w_in_b, conv_w, conv_b.reshape(1, d_rnn), wg0, wg1,
      ba.reshape(1, d_rnn), bx.reshape(1, d_rnn), lam.reshape(1, d_rnn))


def _build_bias_tables(rbp_ref, bias_s, n_heads):
    r_io = lax.broadcasted_iota(jnp.int32, (RBPAD, GPAD), 0)
    j_io = lax.broadcasted_iota(jnp.int32, (RBPAD, GPAD), 1)
    idx = jnp.clip(HIST + CHUNK - 1 - j_io, -REL_CLIP, REL_CLIP) + REL_CLIP
    sel = (r_io == idx).astype(F32)
    g = jnp.dot(rbp_ref[...], sel, precision=lax.Precision.HIGHEST, preferred_element_type=F32)
    g = (g - g[:, 0:1]) * LOG2E
    col = lax.broadcasted_iota(jnp.int32, (CHUNK, WIN), 1)
    band = (LOOKBACK + 1) * CHUNK
    for h in range(n_heads):
        gh = jnp.broadcast_to(g[h:h + 1, :], (CHUNK, GPAD))
        even = pltpu.roll(gh, GPAD - (CHUNK - 1), 1, stride=1, stride_axis=0)[:, 0:WIN]
        odd = pltpu.roll(gh, 1, 1, stride=1, stride_axis=0)[:, 0:WIN]
        bias_s[h, 0] = jnp.where(col < band, even, NEG_INF)
        bias_s[h, 1] = jnp.where(col >= CHUNK, odd, NEG_INF)


ZERO_BIAS_COLS = ((0, HIST - REL_CLIP), (LANES, HIST - REL_CLIP))


def _attn_tile(q_ref, o_ref, kbuf, vbuf, bias_s, first_valid, n_heads):
    lane = lax.broadcasted_iota(jnp.int32, (1, LANES), 1)
    pad = jnp.zeros((CHUNK, NBUF - WIN), BF16)
    n_kt = NBUF // MXU_DIM

    def masked_q(h):
        pair, hh = divmod(h, HEADS_PER_VREG)
        qp = q_ref[:, pair * LANES:(pair + 1) * LANES]
        in_head = jnp.logical_and(lane >= hh * HEAD_DIM, lane < (hh + 1) * HEAD_DIM)
        return jnp.where(in_head, qp, jnp.zeros_like(qp))

    def score_piece(h, qm, j):
        pair = h // HEADS_PER_VREG
        ks = slice(j * MXU_DIM, (j + 1) * MXU_DIM)
        return lax.dot_general(qm, kbuf[ks, pair * LANES:(pair + 1) * LANES],
                               (((1,), (1,)), ((), ())), preferred_element_type=F32)

    def softmax_piece(h, s_all, c):
        w0 = (c // 2) * LANES
        raw = s_all[c * CHUNK:(c + 1) * CHUNK, w0:w0 + WIN]
        z0, z1 = ZERO_BIAS_COLS[c % 2]
        parts = [raw[:, z0:z1], raw[:, z1:WIN] + bias_s[h, c % 2, :, z1:WIN]]
        if z0:
            parts.insert(0, raw[:, 0:z0] + bias_s[h, c % 2, :, 0:z0])
        s = jnp.concatenate(parts, axis=1)
        if first_valid is not None:
            col = lax.broadcasted_iota(jnp.int32, (1, WIN), 1) + w0
            s = jnp.where(col >= first_valid, s, NEG_INF)
        m = jnp.max(s, axis=-1, keepdims=True)
        p = jnp.exp2(s - m)
        inv = 1.0 / jnp.sum(p, axis=-1, keepdims=True)
        pb = p.astype(BF16)
        return inv, jnp.concatenate([pb, pad] if w0 == 0 else [pad, pb], axis=1)

    assert n_kt <= CHUNKS_PER_TILE
    qm = masked_q(0)
    s_next = jnp.concatenate([score_piece(0, qm, j) for j in range(n_kt)], axis=1)
    o_pair = None
    for h in range(n_heads):
        pair, hh = divmod(h, HEADS_PER_VREG)
        s_cur = s_next
        nxt = h + 1 < n_heads
        if nxt:
            qm = masked_q(h + 1)
        pieces = []
        invs, prow = [], []
        for c in range(CHUNKS_PER_TILE):
            if nxt and c < n_kt:
                pieces.append(score_piece(h + 1, qm, c))
            inv, pr = softmax_piece(h, s_cur, c)
            invs.append(inv)
            prow.append(pr)
        if nxt:
            s_next = jnp.concatenate(pieces, axis=1)
        lanes = slice(pair * LANES, (pair + 1) * LANES)
        o = jnp.dot(jnp.concatenate(prow, axis=0), vbuf[:, lanes], preferred_element_type=F32)
        o = o * jnp.concatenate(invs, axis=0)
        if hh == 0:
            o_pair = o
        else:
            in_head = jnp.logical_and(lane >= hh * HEAD_DIM, lane < (hh + 1) * HEAD_DIM)
            o_ref[:, lanes] = jnp.where(in_head, o, o_pair).astype(BF16)


def _attn_kernel(q_ref, k_ref, v_ref, rbp_ref, o_ref, kbuf, vbuf, bias_s, *, n_heads):
    b = pl.program_id(0)
    t = pl.program_id(1)
    d_att = n_heads * HEAD_DIM

    @pl.when(jnp.logical_and(b == 0, t == 0))
    def _():
        _build_bias_tables(rbp_ref, bias_s, n_heads)

    @pl.when(t == 0)
    def _():
        kbuf[0:HIST, :] = jnp.zeros((HIST, d_att), BF16)
        vbuf[0:HIST, :] = jnp.zeros((HIST, d_att), BF16)

    kbuf[HIST:NBUF, :] = k_ref[...]
    vbuf[HIST:NBUF, :] = v_ref[...]

    @pl.when(t < HIST // TQ)
    def _():
        first_valid = CHUNK * (LOOKBACK - CHUNKS_PER_TILE * t)
        _attn_tile(q_ref, o_ref, kbuf, vbuf, bias_s, first_valid, n_heads)

    @pl.when(t >= HIST // TQ)
    def _():
        _attn_tile(q_ref, o_ref, kbuf, vbuf, bias_s, None, n_heads)

    kbuf[0:HIST, :] = kbuf[TQ:NBUF, :]
    vbuf[0:HIST, :] = vbuf[TQ:NBUF, :]


def _attention(qkv, rbp):
    bsz, seq, d3 = qkv.shape
    d_att = d3 // 3
    n_heads = d_att // HEAD_DIM
    assert seq % TQ == 0 and HIST % TQ == 0 and n_heads % HEADS_PER_VREG == 0
    kern = functools.partial(_attn_kernel, n_heads=n_heads)
    return pl.pallas_call(
        kern,
        grid=(bsz, seq // TQ),
        in_specs=[
            pl.BlockSpec((None, TQ, d_att), lambda b, t: (b, t, 0)),
            pl.BlockSpec((None, TQ, d_att), lambda b, t: (b, t, 1)),
            pl.BlockSpec((None, TQ, d_att), lambda b, t: (b, t, 2)),
            _single(rbp.shape, lambda b, t: (0, 0)),
        ],
        out_specs=pl.BlockSpec((None, TQ, d_att), lambda b, t: (b, t, 0)),
        out_shape=jax.ShapeDtypeStruct((bsz, seq, d_att), BF16),
        scratch_shapes=[
            pltpu.VMEM((NBUF, d_att), BF16),
            pltpu.VMEM((NBUF, d_att), BF16),
            pltpu.VMEM((n_heads, 2, CHUNK, WIN), F32),
        ],
        compiler_params=pltpu.CompilerParams(dimension_semantics=("arbitrary", "arbitrary"),
                                             vmem_limit_bytes=VMEM_LIMIT_BYTES),
        name="chunk_attn",
    )(qkv, qkv, qkv, rbp)


def _outproj_ffn_kernel(x_ref, yr_ref, ya_ref, mod_ref, n2g_ref, fg_ref, woa_ref, wob_ref,
                        wu_ref, fcw_ref, fcb_ref, wd_ref, o_ref, ub, halo_s, z_s, x1_s,
                        *, d_model, d_ff, conv_w):
    i = pl.program_id(0)
    rows = TT * SUBLANES
    halo = (conv_w - 1) * SUBLANES

    @pl.when(i == 0)
    def _():
        halo_s[...] = jnp.zeros_like(halo_s)

    yr = yr_ref[...].reshape(rows, yr_ref.shape[2])
    ya = ya_ref[...].reshape(rows, ya_ref.shape[2])
    y = (jnp.dot(yr, woa_ref[...], preferred_element_type=F32)
         + jnp.dot(ya, wob_ref[...], preferred_element_type=F32))
    g1 = mod_ref[:, :, 2 * d_model:3 * d_model]
    sh2 = mod_ref[:, :, 3 * d_model:4 * d_model]
    sc2 = mod_ref[:, :, 4 * d_model:5 * d_model]
    g2 = mod_ref[:, :, 5 * d_model:6 * d_model]
    x1 = x_ref[...] + g1 * y.reshape(SUBLANES, TT, d_model)
    x1_s[...] = x1
    ms = jnp.mean(x1 * x1, axis=-1, keepdims=True)
    hn = x1 * lax.rsqrt(ms + EPS) * n2g_ref[...]
    hn = hn * (1.0 + sc2) + sh2
    hb = jnp.swapaxes(hn, 0, 1).reshape(rows, d_model).astype(BF16)

    first = True
    for c0, w in FF_SPLITS:
        act = None
        for part, off in ((0, 0), (1, d_ff)):
            lo = part * FF_W
            cols = slice(off + c0, off + c0 + w)
            ub[halo:halo + rows, lo:lo + w] = jnp.dot(hb, wu_ref[:, cols],
                                                      preferred_element_type=F32)
            ub[0:halo, lo:lo + w] = halo_s[:, cols]
            halo_s[:, cols] = ub[rows:rows + halo, lo:lo + w]
            u = fcb_ref[:, cols]
            for k in range(conv_w):
                u = u + ub[k * SUBLANES:k * SUBLANES + rows, lo:lo + w] * fcw_ref[k:k + 1, cols]
            act = u * jax.nn.sigmoid(u) if part == 0 else act * u
        zc = jnp.dot(act.astype(BF16), wd_ref[c0:c0 + w, :], preferred_element_type=F32)
        if first:
            z_s[...] = zc
            first = False
        else:
            z_s[...] += zc

    z = jnp.swapaxes(z_s[...].reshape(TT, SUBLANES, d_model), 0, 1)
    x2 = x1_s[...] + g2 * z
    ms2 = jnp.mean(x2 * x2, axis=-1, keepdims=True)
    o_ref[...] = x2 * lax.rsqrt(ms2 + EPS) * fg_ref[...]


def _outproj_ffn(x, yr, ya, mod, n2g, fg, wo_a, wo_b, w_up_b, fcw, fcb, w_down_b):
    bsz, seq, d = x.shape
    d_ff = w_down_b.shape[0]
    cw = fcw.shape[0]
    d_rnn = yr.shape[2]
    d_att = ya.shape[2]
    rows = TT * SUBLANES
    assert sum(w for _, w in FF_SPLITS) == d_ff
    const = lambda i: (0, 0)
    kern = functools.partial(_outproj_ffn_kernel, d_model=d, d_ff=d_ff, conv_w=cw)
    return pl.pallas_call(
        kern,
        grid=(seq // TT,),
        in_specs=[
            pl.BlockSpec((bsz, TT, d), lambda i: (0, i, 0)),
            pl.BlockSpec((bsz, TT, d_rnn), lambda i: (0, i, 0)),
            pl.BlockSpec((bsz, TT, d_att), lambda i: (0, i, 0)),
            _single(mod.shape, lambda i: (0, 0, 0)),
            _single((1, d), const),
            _single((1, d), const),
            _single(wo_a.shape, const),
            _single(wo_b.shape, const),
            _single(w_up_b.shape, const),
            _single(fcw.shape, const),
            _single((1, 2 * d_ff), const),
            _single(w_down_b.shape, const),
        ],
        out_specs=pl.BlockSpec((bsz, TT, d), lambda i: (0, i, 0)),
        out_shape=jax.ShapeDtypeStruct((bsz, seq, d), F32),
        scratch_shapes=[
            pltpu.VMEM((rows + (cw - 1) * SUBLANES, 2 * FF_W), F32),
            pltpu.VMEM(((cw - 1) * SUBLANES, 2 * d_ff), F32),
            pltpu.VMEM((rows, d), F32),
            pltpu.VMEM((bsz, TT, d), F32),
        ],
        compiler_params=pltpu.CompilerParams(dimension_semantics=("arbitrary",),
                                             vmem_limit_bytes=VMEM_LIMIT_BYTES),
        name="outproj_ffn",
    )(x, yr, ya, mod, n2g.reshape(1, d), fg.reshape(1, d), wo_a, wo_b, w_up_b, fcw,
      fcb.reshape(1, 2 * d_ff), w_down_b)


def _gate_weights(wa, wx):
    nb = wa.shape[0]
    hb = nb // 2
    halves = []
    for s in (slice(0, hb), slice(hb, nb)):
        halves.append(jnp.concatenate([block_diag(*wa[s]), block_diag(*wx[s])], axis=1).astype(BF16))
    return halves


def kernel(x, c, ada_w, ada_b, norm1_g, w_in, rnn_conv_w, rnn_conv_b, rg_wa, rg_ba, rg_wx, rg_bx, rg_lambda, rel_bias, w_out, norm2_g, w_up, ffn_conv_w, ffn_conv_b, w_down, final_g):
    bsz, seq, d = x.shape
    assert ada_w.shape[0] == 1
    l = 0
    d_rnn = rnn_conv_w.shape[2]
    d_att = w_out.shape[1] - d_rnn
    mod = _ada(c, ada_w[l], ada_b[l]).reshape(bsz, 1, ada_w.shape[2])
    wg0, wg1 = _gate_weights(rg_wa[l], rg_wx[l])
    qkv, yr = _inproj_rglru(x, mod, norm1_g[l], w_in[l].astype(BF16), rnn_conv_w[l],
                            rnn_conv_b[l], wg0, wg1, rg_ba[l], rg_bx[l], rg_lambda[l])
    rbp = jnp.pad(rel_bias[l], ((0, 0), (0, RBPAD - rel_bias.shape[2])))
    ya = _attention(qkv, rbp)
    w_out_b = w_out[l].astype(BF16)
    return _outproj_ffn(x, yr, ya, mod, norm2_g[l], final_g,
                        w_out_b[0:d_rnn], w_out_b[d_rnn:], w_up[l].astype(BF16), ffn_conv_w[l],
                        ffn_conv_b[l], w_down[l].astype(BF16))
```

```python
import functools

import jax
import jax.numpy as jnp
from jax import lax
from jax.experimental import pallas as pl
from jax.experimental.pallas import tpu as pltpu
from jax.scipy.linalg import block_diag

F32 = jnp.float32
BF16 = jnp.bfloat16

CHUNK = 64
LOOKBACK = 8
REL_CLIP = 128
HEAD_DIM = 64
RG_C = 8.0
EPS = 1e-6
NEG_INF = -1e30
LOG2E = 1.4426950408889634

SUBLANES = 8
LANES = 128
MXU_DIM = 256
VMEM_LIMIT_BYTES = 56 * 1024 * 1024

TT = 64
RT = 16
CHUNKS_PER_TILE = 4
TQ = CHUNKS_PER_TILE * CHUNK
HEADS_PER_VREG = LANES // HEAD_DIM
HIST = LOOKBACK * CHUNK
NBUF = HIST + TQ
WIN = (LOOKBACK + 2) * CHUNK
GPAD = 768
RBPAD = 384
FF_SPLITS = ((0, 1024), (1024, 1024), (2048, 768))
FF_W = 1024


def _single(block_shape, index_map):
    return pl.BlockSpec(block_shape, index_map, pipeline_mode=pl.Buffered(1))


def _ada_kernel(c_ref, w_ref, b_ref, o_ref):
    c = c_ref[...]
    sc = c * jax.nn.sigmoid(c)
    o_ref[...] = jnp.dot(sc, w_ref[...], precision=lax.Precision.HIGHEST,
                         preferred_element_type=F32) + b_ref[...]


def _ada(c, ada_w, ada_b):
    bsz, d = c.shape
    n = ada_w.shape[1]
    bn = 2048
    return pl.pallas_call(
        _ada_kernel,
        grid=(n // bn,),
        in_specs=[pl.BlockSpec((bsz, d), lambda j: (0, 0)),
                  pl.BlockSpec((d, bn), lambda j: (0, j)),
                  pl.BlockSpec((1, bn), lambda j: (0, j))],
        out_specs=pl.BlockSpec((bsz, bn), lambda j: (0, j)),
        out_shape=jax.ShapeDtypeStruct((bsz, n), F32),
        name="ada_mod",
    )(c, ada_w, ada_b.reshape(1, n))


def _inproj_rglru_kernel(x_ref, mod_ref, n1g_ref, win_ref, cw_ref, cb_ref, wg0_ref, wg1_ref,
                         ba_ref, bx_ref, lam_ref, qkv_ref, yr_ref,
                         tail_s, hc, *, d_model, d_rnn, d_att, conv_w):
    i = pl.program_id(0)
    rows = TT * SUBLANES
    halo = (conv_w - 1) * SUBLANES
    prow = RT * SUBLANES

    @pl.when(i == 0)
    def _():
        tail_s[...] = jnp.zeros_like(tail_s)
        hc[...] = jnp.zeros_like(hc)

    x = x_ref[...]
    ms = jnp.mean(x * x, axis=-1, keepdims=True)
    sh1 = mod_ref[:, :, 0:d_model]
    sc1 = mod_ref[:, :, d_model:2 * d_model]
    h = x * lax.rsqrt(ms + EPS) * (n1g_ref[...] * (1.0 + sc1)) + sh1
    hb = h.reshape(rows, d_model).astype(BF16)

    def project_piece(j):
        part = jnp.dot(hb, win_ref[:, j * d_rnn:(j + 1) * d_rnn], preferred_element_type=F32)
        return part.reshape(SUBLANES, TT, d_rnn)

    def emit_qkv(j):
        part = project_piece(j)
        if j == 2:
            part = part * (LOG2E * HEAD_DIM ** -0.5)
        qkv_ref[:, :, (j - 2) * d_att:(j - 1) * d_att] = part.astype(BF16)

    def recur_piece(xr_b, gr_b, c, tail, hcur):
        ts = slice(c * RT, (c + 1) * RT)
        xr = jnp.swapaxes(xr_b[:, ts, :], 0, 1).reshape(prow, d_rnn)
        gr = jnp.swapaxes(gr_b[:, ts, :], 0, 1).reshape(prow, d_rnn)
        ext = jnp.concatenate([tail, xr], axis=0)
        xc = cb_ref[...]
        for k in range(conv_w):
            xc = xc + ext[k * SUBLANES:k * SUBLANES + prow, :] * cw_ref[k:k + 1, :]
        xcb = xc.astype(BF16)
        half = d_rnn // 2
        g0 = jnp.dot(xcb[:, 0:half], wg0_ref[...], preferred_element_type=F32)
        g1 = jnp.dot(xcb[:, half:d_rnn], wg1_ref[...], preferred_element_type=F32)
        r = jax.nn.sigmoid(jnp.concatenate([g0[:, 0:half], g1[:, 0:half]], axis=1) + ba_ref[...])
        ig = jax.nn.sigmoid(jnp.concatenate([g0[:, half:], g1[:, half:]], axis=1) + bx_ref[...])
        lam = lam_ref[...]
        log_sig = jnp.minimum(lam, 0.0) - jnp.log1p(jnp.exp(-jnp.abs(lam)))
        log_a = RG_C * r * log_sig
        a = jnp.exp(log_a)
        w = jnp.tanh(-log_a) * (a * a + 1.0)
        mult = jnp.where(w > 0.0, w * lax.rsqrt(w), 0.0)
        a3 = a.reshape(RT, SUBLANES, d_rnn)
        b3 = (mult * (ig * xc)).reshape(RT, SUBLANES, d_rnn)
        hs = []
        for t in range(RT):
            hcur = a3[t] * hcur + b3[t]
            hs.append(hcur)
        yr = jnp.stack(hs, axis=0) * jax.nn.gelu(gr).reshape(RT, SUBLANES, d_rnn)
        yr_ref[:, ts, :] = jnp.swapaxes(yr, 0, 1).astype(BF16)
        return ext[prow:prow + halo, :], hcur

    xr_b = project_piece(0)
    gr_b = project_piece(1)
    tail, hcur = tail_s[...], hc[...]
    n_rec = TT // RT
    n_qkv = (3 * d_att) // d_rnn
    for c in range(max(n_rec, n_qkv)):
        if c < n_qkv:
            emit_qkv(2 + c)
        if c < n_rec:
            tail, hcur = recur_piece(xr_b, gr_b, c, tail, hcur)
    tail_s[...] = tail
    hc[...] = hcur


def _inproj_rglru(x, mod, n1g, w_in_b, conv_w, conv_b, wg0, wg1, ba, bx, lam):
    bsz, seq, d = x.shape
    assert bsz == SUBLANES and seq % TT == 0
    d_in = w_in_b.shape[1]
    d_rnn = conv_w.shape[1]
    d_att = (d_in - 2 * d_rnn) // 3
    assert d_att == d_rnn and TT % RT == 0
    cw = conv_w.shape[0]
    const = lambda i: (0, 0)
    kern = functools.partial(_inproj_rglru_kernel, d_model=d, d_rnn=d_rnn, d_att=d_att, conv_w=cw)
    return pl.pallas_call(
        kern,
        grid=(seq // TT,),
        in_specs=[
            pl.BlockSpec((bsz, TT, d), lambda i: (0, i, 0)),
            _single(mod.shape, lambda i: (0, 0, 0)),
            _single((1, d), const),
            _single(w_in_b.shape, const),
            _single(conv_w.shape, const),
            _single((1, d_rnn), const),
            _single(wg0.shape, const),
            _single(wg1.shape, const),
            _single((1, d_rnn), const),
            _single((1, d_rnn), const),
            _single((1, d_rnn), const),
        ],
        out_specs=[pl.BlockSpec((bsz, TT, 3 * d_att), lambda i: (0, i, 0)),
                   pl.BlockSpec((bsz, TT, d_rnn), lambda i: (0, i, 0))],
        out_shape=[jax.ShapeDtypeStruct((bsz, seq, 3 * d_att), BF16),
                   jax.ShapeDtypeStruct((bsz, seq, d_rnn), BF16)],
        scratch_shapes=[
            pltpu.VMEM(((cw - 1) * SUBLANES, d_rnn), F32),
            pltpu.VMEM((SUBLANES, d_rnn), F32),
        ],
        compiler_params=pltpu.CompilerParams(dimension_semantics=("arbitrary",),
                                             vmem_limit_bytes=VMEM_LIMIT_BYTES),
        name="inproj_rglru",
    )(x, mod, n1g.reshape(1, d), w_in_b, conv_w, conv_b.reshape(1, d_rnn), wg0, wg1,
      ba.reshape(1, d_rnn), bx.reshape(1, d_rnn), lam.reshape(1, d_rnn))


def _build_bias_tables(rbp_ref, bias_s, n_heads):
    r_io = lax.broadcasted_iota(jnp.int32, (RBPAD, GPAD), 0)
    j_io = lax.broadcasted_iota(jnp.int32, (RBPAD, GPAD), 1)
    idx = jnp.clip(HIST + CHUNK - 1 - j_io, -REL_CLIP, REL_CLIP) + REL_CLIP
    sel = (r_io == idx).astype(F32)
    g = jnp.dot(rbp_ref[...], sel, precision=lax.Precision.HIGHEST, preferred_element_type=F32)
    g = (g - g[:, 0:1]) * LOG2E
    col = lax.broadcasted_iota(jnp.int32, (CHUNK, WIN), 1)
    band = (LOOKBACK + 1) * CHUNK
    for h in range(n_heads):
        gh = jnp.broadcast_to(g[h:h + 1, :], (CHUNK, GPAD))
        even = pltpu.roll(gh, GPAD - (CHUNK - 1), 1, stride=1, stride_axis=0)[:, 0:WIN]
        odd = pltpu.roll(gh, 1, 1, stride=1, stride_axis=0)[:, 0:WIN]
        bias_s[h, 0] = jnp.where(col < band, even, NEG_INF)
        bias_s[h, 1] = jnp.where(col >= CHUNK, odd, NEG_INF)


ZERO_BIAS_COLS = ((0, HIST - REL_CLIP), (LANES, HIST - REL_CLIP))


def _attn_tile(q_ref, o_ref, kbuf, vbuf, bias_s, first_valid, n_heads):
    lane = lax.broadcasted_iota(jnp.int32, (1, LANES), 1)
    pad = jnp.zeros((CHUNK, NBUF - WIN), BF16)
    n_kt = NBUF // MXU_DIM

    def masked_q(h):
        pair, hh = divmod(h, HEADS_PER_VREG)
        qp = q_ref[:, pair * LANES:(pair + 1) * LANES]
        in_head = jnp.logical_and(lane >= hh * HEAD_DIM, lane < (hh + 1) * HEAD_DIM)
        return jnp.where(in_head, qp, jnp.zeros_like(qp))

    def score_piece(h, qm, j):
        pair = h // HEADS_PER_VREG
        ks = slice(j * MXU_DIM, (j + 1) * MXU_DIM)
        return lax.dot_general(qm, kbuf[ks, pair * LANES:(pair + 1) * LANES],
                               (((1,), (1,)), ((), ())), preferred_element_type=F32)

    def softmax_piece(h, s_all, c):
        w0 = (c // 2) * LANES
        raw = s_all[c * CHUNK:(c + 1) * CHUNK, w0:w0 + WIN]
        z0, z1 = ZERO_BIAS_COLS[c % 2]
        parts = [raw[:, z0:z1], raw[:, z1:WIN] + bias_s[h, c % 2, :, z1:WIN]]
        if z0:
            parts.insert(0, raw[:, 0:z0] + bias_s[h, c % 2, :, 0:z0])
        s = jnp.concatenate(parts, axis=1)
        if first_valid is not None:
            col = lax.broadcasted_iota(jnp.int32, (1, WIN), 1) + w0
            s = jnp.where(col >= first_valid, s, NEG_INF)
        m = jnp.max(s, axis=-1, keepdims=True)
        p = jnp.exp2(s - m)
        inv = 1.0 / jnp.sum(p, axis=-1, keepdims=True)
        pb = p.astype(BF16)
        return inv, jnp.concatenate([pb, pad] if w0 == 0 else [pad, pb], axis=1)

    def value_piece(h, pmat, j):
        pair = h // HEADS_PER_VREG
        ks = slice(j * MXU_DIM, (j + 1) * MXU_DIM)
        return jnp.dot(pmat[:, ks], vbuf[ks, pair * LANES:(pair + 1) * LANES],
                       preferred_element_type=F32)

    def finish(h, o, inv, o_prev):
        pair, hh = divmod(h, HEADS_PER_VREG)
        o = o * inv
        if hh == 0:
            return o
        in_head = jnp.logical_and(lane >= hh * HEAD_DIM, lane < (hh + 1) * HEAD_DIM)
        o_ref[:, pair * LANES:(pair + 1) * LANES] = jnp.where(in_head, o, o_prev).astype(BF16)
        return None

    assert n_kt <= CHUNKS_PER_TILE and HEADS_PER_VREG == 2
    qm = masked_q(0)
    s_next = jnp.concatenate([score_piece(0, qm, j) for j in range(n_kt)], axis=1)
    o_pair = None
    prev = None
    for h in range(n_heads + 1):
        cur = h < n_heads
        nxt = h + 1 < n_heads
        s_cur = s_next
        if nxt:
            qm = masked_q(h + 1)
        pieces, invs, prow, o_acc = [], [], [], None
        for c in range(CHUNKS_PER_TILE):
            if nxt and c < n_kt:
                pieces.append(score_piece(h + 1, qm, c))
            if prev is not None and c < n_kt:
                part = value_piece(prev[0], prev[1], c)
                o_acc = part if o_acc is None else o_acc + part
            if cur:
                inv, pr = softmax_piece(h, s_cur, c)
                invs.append(inv)
                prow.append(pr)
        if prev is not None:
            o_pair = finish(prev[0], o_acc, prev[2], o_pair)
        if nxt:
            s_next = jnp.concatenate(pieces, axis=1)
        prev = (h, jnp.concatenate(prow, axis=0), jnp.concatenate(invs, axis=0)) if cur else None


def _attn_kernel(q_ref, k_ref, v_ref, rbp_ref, o_ref, kbuf, vbuf, bias_s, *, n_heads):
    b = pl.program_id(0)
    t = pl.program_id(1)
    d_att = n_heads * HEAD_DIM

    @pl.when(jnp.logical_and(b == 0, t == 0))
    def _():
        _build_bias_tables(rbp_ref, bias_s, n_heads)

    @pl.when(t == 0)
    def _():
        kbuf[0:HIST, :] = jnp.zeros((HIST, d_att), BF16)
        vbuf[0:HIST, :] = jnp.zeros((HIST, d_att), BF16)

    kbuf[HIST:NBUF, :] = k_ref[...]
    vbuf[HIST:NBUF, :] = v_ref[...]

    @pl.when(t < HIST // TQ)
    def _():
        first_valid = CHUNK * (LOOKBACK - CHUNKS_PER_TILE * t)
        _attn_tile(q_ref, o_ref, kbuf, vbuf, bias_s, first_valid, n_heads)

    @pl.when(t >= HIST // TQ)
    def _():
        _attn_tile(q_ref, o_ref, kbuf, vbuf, bias_s, None, n_heads)

    kbuf[0:HIST, :] = kbuf[TQ:NBUF, :]
    vbuf[0:HIST, :] = vbuf[TQ:NBUF, :]


def _attention(qkv, rbp):
    bsz, seq, d3 = qkv.shape
    d_att = d3 // 3
    n_heads = d_att // HEAD_DIM
    assert seq % TQ == 0 and HIST % TQ == 0 and n_heads % HEADS_PER_VREG == 0
    kern = functools.partial(_attn_kernel, n_heads=n_heads)
    return pl.pallas_call(
        kern,
        grid=(bsz, seq // TQ),
        in_specs=[
            pl.BlockSpec((None, TQ, d_att), lambda b, t: (b, t, 0)),
            pl.BlockSpec((None, TQ, d_att), lambda b, t: (b, t, 1)),
            pl.BlockSpec((None, TQ, d_att), lambda b, t: (b, t, 2)),
            _single(rbp.shape, lambda b, t: (0, 0)),
        ],
        out_specs=pl.BlockSpec((None, TQ, d_att), lambda b, t: (b, t, 0)),
        out_shape=jax.ShapeDtypeStruct((bsz, seq, d_att), BF16),
        scratch_shapes=[
            pltpu.VMEM((NBUF, d_att), BF16),
            pltpu.VMEM((NBUF, d_att), BF16),
            pltpu.VMEM((n_heads, 2, CHUNK, WIN), F32),
        ],
        compiler_params=pltpu.CompilerParams(dimension_semantics=("arbitrary", "arbitrary"),
                                             vmem_limit_bytes=VMEM_LIMIT_BYTES),
        name="chunk_attn",
    )(qkv, qkv, qkv, rbp)


def _outproj_ffn_kernel(x_ref, yr_ref, ya_ref, mod_ref, n2g_ref, fg_ref, woa_ref, wob_ref,
                        wu_ref, fcw_ref, fcb_ref, wd_ref, o_ref, ub, halo_s, z_s, x1_s,
                        *, d_model, d_ff, conv_w):
    i = pl.program_id(0)
    rows = TT * SUBLANES
    halo = (conv_w - 1) * SUBLANES

    @pl.when(i == 0)
    def _():
        halo_s[...] = jnp.zeros_like(halo_s)

    yr = yr_ref[...].reshape(rows, yr_ref.shape[2])
    ya = ya_ref[...].reshape(rows, ya_ref.shape[2])
    y = (jnp.dot(yr, woa_ref[...], preferred_element_type=F32)
         + jnp.dot(ya, wob_ref[...], preferred_element_type=F32))
    g1 = mod_ref[:, :, 2 * d_model:3 * d_model]
    sh2 = mod_ref[:, :, 3 * d_model:4 * d_model]
    sc2 = mod_ref[:, :, 4 * d_model:5 * d_model]
    g2 = mod_ref[:, :, 5 * d_model:6 * d_model]
    x1 = x_ref[...] + g1 * y.reshape(SUBLANES, TT, d_model)
    x1_s[...] = x1
    ms = jnp.mean(x1 * x1, axis=-1, keepdims=True)
    hn = x1 * lax.rsqrt(ms + EPS) * n2g_ref[...]
    hn = hn * (1.0 + sc2) + sh2
    hb = jnp.swapaxes(hn, 0, 1).reshape(rows, d_model).astype(BF16)

    first = True
    for c0, w in FF_SPLITS:
        act = None
        for part, off in ((0, 0), (1, d_ff)):
            lo = part * FF_W
            cols = slice(off + c0, off + c0 + w)
            ub[halo:halo + rows, lo:lo + w] = jnp.dot(hb, wu_ref[:, cols],
                                                      preferred_element_type=F32)
            ub[0:halo, lo:lo + w] = halo_s[:, cols]
            halo_s[:, cols] = ub[rows:rows + halo, lo:lo + w]
            u = fcb_ref[:, cols]
            for k in range(conv_w):
                u = u + ub[k * SUBLANES:k * SUBLANES + rows, lo:lo + w] * fcw_ref[k:k + 1, cols]
            act = u * jax.nn.sigmoid(u) if part == 0 else act * u
        zc = jnp.dot(act.astype(BF16), wd_ref[c0:c0 + w, :], preferred_element_type=F32)
        if first:
            z_s[...] = zc
            first = False
        else:
            z_s[...] += zc

    z = jnp.swapaxes(z_s[...].reshape(TT, SUBLANES, d_model), 0, 1)
    x2 = x1_s[...] + g2 * z
    ms2 = jnp.mean(x2 * x2, axis=-1, keepdims=True)
    o_ref[...] = x2 * lax.rsqrt(ms2 + EPS) * fg_ref[...]


def _outproj_ffn(x, yr, ya, mod, n2g, fg, wo_a, wo_b, w_up_b, fcw, fcb, w_down_b):
    bsz, seq, d = x.shape
    d_ff = w_down_b.shape[0]
    cw = fcw.shape[0]
    d_rnn = yr.shape[2]
    d_att = ya.shape[2]
    rows = TT * SUBLANES
    assert sum(w for _, w in FF_SPLITS) == d_ff
    const = lambda i: (0, 0)
    kern = functools.partial(_outproj_ffn_kernel, d_model=d, d_ff=d_ff, conv_w=cw)
    return pl.pallas_call(
        kern,
        grid=(seq // TT,),
        in_specs=[
            pl.BlockSpec((bsz, TT, d), lambda i: (0, i, 0)),
            pl.BlockSpec((bsz, TT, d_rnn), lambda i: (0, i, 0)),
            pl.BlockSpec((bsz, TT, d_att), lambda i: (0, i, 0)),
            _single(mod.shape, lambda i: (0, 0, 0)),
            _single((1, d), const),
            _single((1, d), const),
            _single(wo_a.shape, const),
            _single(wo_b.shape, const),
            _single(w_up_b.shape, const),
            _single(fcw.shape, const),
            _single((1, 2 * d_ff), const),
            _single(w_down_b.shape, const),
        ],
        out_specs=pl.BlockSpec((bsz, TT, d), lambda i: (0, i, 0)),
        out_shape=jax.ShapeDtypeStruct((bsz, seq, d), F32),
        scratch_shapes=[
            pltpu.VMEM((rows + (cw - 1) * SUBLANES, 2 * FF_W), F32),
            pltpu.VMEM(((cw - 1) * SUBLANES, 2 * d_ff), F32),
            pltpu.VMEM((rows, d), F32),
            pltpu.VMEM((bsz, TT, d), F32),
        ],
        compiler_params=pltpu.CompilerParams(dimension_semantics=("arbitrary",),
                                             vmem_limit_bytes=VMEM_LIMIT_BYTES),
        name="outproj_ffn",
    )(x, yr, ya, mod, n2g.reshape(1, d), fg.reshape(1, d), wo_a, wo_b, w_up_b, fcw,
      fcb.reshape(1, 2 * d_ff), w_down_b)


def _gate_weights(wa, wx):
    nb = wa.shape[0]
    hb = nb // 2
    halves = []
    for s in (slice(0, hb), slice(hb, nb)):
        halves.append(jnp.concatenate([block_diag(*wa[s]), block_diag(*wx[s])], axis=1).astype(BF16))
    return halves


def kernel(x, c, ada_w, ada_b, norm1_g, w_in, rnn_conv_w, rnn_conv_b, rg_wa, rg_ba, rg_wx, rg_bx, rg_lambda, rel_bias, w_out, norm2_g, w_up, ffn_conv_w, ffn_conv_b, w_down, final_g):
    bsz, seq, d = x.shape
    assert ada_w.shape[0] == 1
    l = 0
    d_rnn = rnn_conv_w.shape[2]
    d_att = w_out.shape[1] - d_rnn
    mod = _ada(c, ada_w[l], ada_b[l]).reshape(bsz, 1, ada_w.shape[2])
    wg0, wg1 = _gate_weights(rg_wa[l], rg_wx[l])
    qkv, yr = _inproj_rglru(x, mod, norm1_g[l], w_in[l].astype(BF16), rnn_conv_w[l],
                            rnn_conv_b[l], wg0, wg1, rg_ba[l], rg_bx[l], rg_lambda[l])
    rbp = jnp.pad(rel_bias[l], ((0, 0), (0, RBPAD - rel_bias.shape[2])))
    ya = _attention(qkv, rbp)
    w_out_b = w_out[l].astype(BF16)
    return _outproj_ffn(x, yr, ya, mod, norm2_g[l], final_g,
                        w_out_b[0:d_rnn], w_out_b[d_rnn:], w_up[l].astype(BF16), ffn_conv_w[l],
                        ffn_conv_b[l], w_down[l].astype(BF16))
```

```python
import functools

import jax
import jax.numpy as jnp
from jax import lax
from jax.experimental import pallas as pl
from jax.experimental.pallas import tpu as pltpu
from jax.scipy.linalg import block_diag

F32 = jnp.float32
BF16 = jnp.bfloat16

CHUNK = 64
LOOKBACK = 8
REL_CLIP = 128
HEAD_DIM = 64
RG_C = 8.0
EPS = 1e-6
NEG_INF = -1e30
LOG2E = 1.4426950408889634

SUBLANES = 8
LANES = 128
MXU_DIM = 256
VMEM_LIMIT_BYTES = 56 * 1024 * 1024

TT = 64
RT = 16
CHUNKS_PER_TILE = 4
TQ = CHUNKS_PER_TILE * CHUNK
HEADS_PER_VREG = LANES // HEAD_DIM
HIST = LOOKBACK * CHUNK
NBUF = HIST + TQ
WIN = (LOOKBACK + 2) * CHUNK
GPAD = 768
RBPAD = 384
FF_SPLITS = ((0, 1024), (1024, 1024), (2048, 768))
FF_W = 1024


def _single(block_shape, index_map):
    return pl.BlockSpec(block_shape, index_map, pipeline_mode=pl.Buffered(1))


def _ada_kernel(c_ref, w_ref, b_ref, o_ref):
    c = c_ref[...]
    sc = c * jax.nn.sigmoid(c)
    o_ref[...] = jnp.dot(sc, w_ref[...], precision=lax.Precision.HIGHEST,
                         preferred_element_type=F32) + b_ref[...]


def _ada(c, ada_w, ada_b):
    bsz, d = c.shape
    n = ada_w.shape[1]
    bn = 2048
    return pl.pallas_call(
        _ada_kernel,
        grid=(n // bn,),
        in_specs=[pl.BlockSpec((bsz, d), lambda j: (0, 0)),
                  pl.BlockSpec((d, bn), lambda j: (0, j)),
                  pl.BlockSpec((1, bn), lambda j: (0, j))],
        out_specs=pl.BlockSpec((bsz, bn), lambda j: (0, j)),
        out_shape=jax.ShapeDtypeStruct((bsz, n), F32),
        name="ada_mod",
    )(c, ada_w, ada_b.reshape(1, n))


def _inproj_rglru_kernel(x_ref, mod_ref, n1g_ref, win_ref, cw_ref, cb_ref, wg0_ref, wg1_ref,
                         ba_ref, bx_ref, lam_ref, qkv_ref, yr_ref,
                         tail_s, hc, *, d_model, d_rnn, d_att, conv_w):
    i = pl.program_id(0)
    rows = TT * SUBLANES
    halo = (conv_w - 1) * SUBLANES
    prow = RT * SUBLANES

    @pl.when(i == 0)
    def _():
        tail_s[...] = jnp.zeros_like(tail_s)
        hc[...] = jnp.zeros_like(hc)

    x = x_ref[...]
    ms = jnp.mean(x * x, axis=-1, keepdims=True)
    sh1 = mod_ref[:, :, 0:d_model]
    sc1 = mod_ref[:, :, d_model:2 * d_model]
    h = x * lax.rsqrt(ms + EPS) * (n1g_ref[...] * (1.0 + sc1)) + sh1
    hb = h.reshape(rows, d_model).astype(BF16)

    def project_piece(j):
        part = jnp.dot(hb, win_ref[:, j * d_rnn:(j + 1) * d_rnn], preferred_element_type=F32)
        return part.reshape(SUBLANES, TT, d_rnn)

    def emit_qkv(j):
        part = project_piece(j)
        if j == 2:
            part = part * (LOG2E * HEAD_DIM ** -0.5)
        qkv_ref[:, :, (j - 2) * d_att:(j - 1) * d_att] = part.astype(BF16)

    def recur_piece(xr_b, gr_b, c, tail, hcur):
        ts = slice(c * RT, (c + 1) * RT)
        xr = jnp.swapaxes(xr_b[:, ts, :], 0, 1).reshape(prow, d_rnn)
        gr = jnp.swapaxes(gr_b[:, ts, :], 0, 1).reshape(prow, d_rnn)
        ext = jnp.concatenate([tail, xr], axis=0)
        xc = cb_ref[...]
        for k in range(conv_w):
            xc = xc + ext[k * SUBLANES:k * SUBLANES + prow, :] * cw_ref[k:k + 1, :]
        xcb = xc.astype(BF16)
        half = d_rnn // 2
        g0 = jnp.dot(xcb[:, 0:half], wg0_ref[...], preferred_element_type=F32)
        g1 = jnp.dot(xcb[:, half:d_rnn], wg1_ref[...], preferred_element_type=F32)
        r = jax.nn.sigmoid(jnp.concatenate([g0[:, 0:half], g1[:, 0:half]], axis=1) + ba_ref[...])
        ig = jax.nn.sigmoid(jnp.concatenate([g0[:, half:], g1[:, half:]], axis=1) + bx_ref[...])
        lam = lam_ref[...]
        log_sig = jnp.minimum(lam, 0.0) - jnp.log1p(jnp.exp(-jnp.abs(lam)))
        log_a = RG_C * r * log_sig
        a = jnp.exp(log_a)
        w = jnp.tanh(-log_a) * (a * a + 1.0)
        mult = jnp.where(w > 0.0, w * lax.rsqrt(w), 0.0)
        a3 = a.reshape(RT, SUBLANES, d_rnn)
        b3 = (mult * (ig * xc)).reshape(RT, SUBLANES, d_rnn)
        hs = []
        for t in range(RT):
            hcur = a3[t] * hcur + b3[t]
            hs.append(hcur)
        yr = jnp.stack(hs, axis=0) * jax.nn.gelu(gr).reshape(RT, SUBLANES, d_rnn)
        yr_ref[:, ts, :] = jnp.swapaxes(yr, 0, 1).astype(BF16)
        return ext[prow:prow + halo, :], hcur

    xr_b = project_piece(0)
    gr_b = project_piece(1)
    tail, hcur = tail_s[...], hc[...]
    n_rec = TT // RT
    n_qkv = (3 * d_att) // d_rnn
    for c in range(max(n_rec, n_qkv)):
        if c < n_qkv:
            emit_qkv(2 + c)
        if c < n_rec:
            tail, hcur = recur_piece(xr_b, gr_b, c, tail, hcur)
    tail_s[...] = tail
    hc[...] = hcur


def _inproj_rglru(x, mod, n1g, w_in_b, conv_w, conv_b, wg0, wg1, ba, bx, lam):
    bsz, seq, d = x.shape
    assert bsz == SUBLANES and seq % TT == 0
    d_in = w_in_b.shape[1]
    d_rnn = conv_w.shape[1]
    d_att = (d_in - 2 * d_rnn) // 3
    assert d_att == d_rnn and TT % RT == 0
    cw = conv_w.shape[0]
    const = lambda i: (0, 0)
    kern = functools.partial(_inproj_rglru_kernel, d_model=d, d_rnn=d_rnn, d_att=d_att, conv_w=cw)
    return pl.pallas_call(
        kern,
        grid=(seq // TT,),
        in_specs=[
            pl.BlockSpec((bsz, TT, d), lambda i: (0, i, 0)),
            _single(mod.shape, lambda i: (0, 0, 0)),
            _single((1, d), const),
            _single(w_in_b.shape, const),
            _single(conv_w.shape, const),
            _single((1, d_rnn), const),
            _single(wg0.shape, const),
            _single(wg1.shape, const),
            _single((1, d_rnn), const),
            _single((1, d_rnn), const),
            _single((1, d_rnn), const),
        ],
        out_specs=[pl.BlockSpec((bsz, TT, 3 * d_att), lambda i: (0, i, 0)),
                   pl.BlockSpec((bsz, TT, d_rnn), lambda i: (0, i, 0))],
        out_shape=[jax.ShapeDtypeStruct((bsz, seq, 3 * d_att), BF16),
                   jax.ShapeDtypeStruct((bsz, seq, d_rnn), BF16)],
        scratch_shapes=[
            pltpu.VMEM(((cw - 1) * SUBLANES, d_rnn), F32),
            pltpu.VMEM((SUBLANES, d_rnn), F32),
        ],
        compiler_params=pltpu.CompilerParams(dimension_semantics=("arbitrary",),
                                             vmem_limit_bytes=VMEM_LIMIT_BYTES),
        name="inproj_rglru",
    )(x, mod, n1g.reshape(1, d), w_in_b, conv_w, conv_b.reshape(1, d_rnn), wg0, wg1,
      ba.reshape(1, d_rnn), bx.reshape(1, d_rnn), lam.reshape(1, d_rnn))


def _build_bias_tables(rbp_ref, bias_s, n_heads):
    r_io = lax.broadcasted_iota(jnp.int32, (RBPAD, GPAD), 0)
    j_io = lax.broadcasted_iota(jnp.int32, (RBPAD, GPAD), 1)
    idx = jnp.clip(HIST + CHUNK - 1 - j_io, -REL_CLIP, REL_CLIP) + REL_CLIP
    sel = (r_io == idx).astype(F32)
    g = jnp.dot(rbp_ref[...], sel, precision=lax.Precision.HIGHEST, preferred_element_type=F32)
    g = (g - g[:, 0:1]) * LOG2E
    col = lax.broadcasted_iota(jnp.int32, (CHUNK, WIN), 1)
    band = (LOOKBACK + 1) * CHUNK
    for h in range(n_heads):
        gh = jnp.broadcast_to(g[h:h + 1, :], (CHUNK, GPAD))
        even = pltpu.roll(gh, GPAD - (CHUNK - 1), 1, stride=1, stride_axis=0)[:, 0:WIN]
        odd = pltpu.roll(gh, 1, 1, stride=1, stride_axis=0)[:, 0:WIN]
        bias_s[h, 0] = jnp.where(col < band, even, NEG_INF)
        bias_s[h, 1] = jnp.where(col >= CHUNK, odd, NEG_INF)


ZERO_BIAS_COLS = ((0, HIST - REL_CLIP), (LANES, HIST - REL_CLIP))


def _attn_tile(q_ref, k_refs, v_refs, o_ref, bias_s, first_valid, n_heads):
    lane = lax.broadcasted_iota(jnp.int32, (1, LANES), 1)
    pad = jnp.zeros((CHUNK, NBUF - WIN), BF16)
    n_kt = len(k_refs)

    def masked_q(h):
        pair, hh = divmod(h, HEADS_PER_VREG)
        qp = q_ref[:, pair * LANES:(pair + 1) * LANES]
        in_head = jnp.logical_and(lane >= hh * HEAD_DIM, lane < (hh + 1) * HEAD_DIM)
        return jnp.where(in_head, qp, jnp.zeros_like(qp))

    def score_piece(h, qm, j):
        pair = h // HEADS_PER_VREG
        return lax.dot_general(qm, k_refs[j][:, pair * LANES:(pair + 1) * LANES],
                               (((1,), (1,)), ((), ())), preferred_element_type=F32)

    def softmax_piece(h, s_all, c):
        w0 = (c // 2) * LANES
        raw = s_all[c * CHUNK:(c + 1) * CHUNK, w0:w0 + WIN]
        z0, z1 = ZERO_BIAS_COLS[c % 2]
        parts = [raw[:, z0:z1], raw[:, z1:WIN] + bias_s[h, c % 2, :, z1:WIN]]
        if z0:
            parts.insert(0, raw[:, 0:z0] + bias_s[h, c % 2, :, 0:z0])
        s = jnp.concatenate(parts, axis=1)
        if first_valid is not None:
            col = lax.broadcasted_iota(jnp.int32, (1, WIN), 1) + w0
            s = jnp.where(col >= first_valid, s, NEG_INF)
        m = jnp.max(s, axis=-1, keepdims=True)
        p = jnp.exp2(s - m)
        inv = 1.0 / jnp.sum(p, axis=-1, keepdims=True)
        pb = p.astype(BF16)
        return inv, jnp.concatenate([pb, pad] if w0 == 0 else [pad, pb], axis=1)

    def value_piece(h, pmat, j):
        pair = h // HEADS_PER_VREG
        return jnp.dot(pmat[:, j * TQ:(j + 1) * TQ], v_refs[j][:, pair * LANES:(pair + 1) * LANES],
                       preferred_element_type=F32)

    def finish(h, o, inv, o_prev):
        pair, hh = divmod(h, HEADS_PER_VREG)
        o = o * inv
        if hh == 0:
            return o
        in_head = jnp.logical_and(lane >= hh * HEAD_DIM, lane < (hh + 1) * HEAD_DIM)
        o_ref[:, pair * LANES:(pair + 1) * LANES] = jnp.where(in_head, o, o_prev).astype(BF16)
        return None

    assert n_kt <= CHUNKS_PER_TILE and HEADS_PER_VREG == 2
    qm = masked_q(0)
    s_next = jnp.concatenate([score_piece(0, qm, j) for j in range(n_kt)], axis=1)
    o_pair = None
    prev = None
    for h in range(n_heads + 1):
        cur = h < n_heads
        nxt = h + 1 < n_heads
        s_cur = s_next
        if nxt:
            qm = masked_q(h + 1)
        pieces, invs, prow, o_acc = [], [], [], None
        for c in range(CHUNKS_PER_TILE):
            if nxt and c < n_kt:
                pieces.append(score_piece(h + 1, qm, c))
            if prev is not None and c < n_kt:
                part = value_piece(prev[0], prev[1], c)
                o_acc = part if o_acc is None else o_acc + part
            if cur:
                inv, pr = softmax_piece(h, s_cur, c)
                invs.append(inv)
                prow.append(pr)
        if prev is not None:
            o_pair = finish(prev[0], o_acc, prev[2], o_pair)
        if nxt:
            s_next = jnp.concatenate(pieces, axis=1)
        prev = (h, jnp.concatenate(prow, axis=0), jnp.concatenate(invs, axis=0)) if cur else None


def _attn_kernel(*refs, n_heads):
    n_win = NBUF // TQ
    q_ref = refs[0]
    k_refs = refs[1:1 + n_win]
    v_refs = refs[1 + n_win:1 + 2 * n_win]
    rbp_ref, o_ref, bias_s = refs[1 + 2 * n_win:]
    b = pl.program_id(0)
    t = pl.program_id(1)

    @pl.when(jnp.logical_and(b == 0, t == 0))
    def _():
        _build_bias_tables(rbp_ref, bias_s, n_heads)

    @pl.when(t < n_win - 1)
    def _():
        first_valid = TQ * (n_win - 1 - t)
        _attn_tile(q_ref, k_refs, v_refs, o_ref, bias_s, first_valid, n_heads)

    @pl.when(t >= n_win - 1)
    def _():
        _attn_tile(q_ref, k_refs, v_refs, o_ref, bias_s, None, n_heads)


def _attention(qkv, rbp):
    bsz, seq, d3 = qkv.shape
    d_att = d3 // 3
    n_heads = d_att // HEAD_DIM
    n_win = NBUF // TQ
    assert seq % TQ == 0 and NBUF % TQ == 0 and n_heads % HEADS_PER_VREG == 0
    kern = functools.partial(_attn_kernel, n_heads=n_heads)

    def window(col, j):
        return pl.BlockSpec((None, TQ, d_att),
                            lambda b, t: (b, jnp.maximum(t - (n_win - 1 - j), 0), col))

    return pl.pallas_call(
        kern,
        grid=(bsz, seq // TQ),
        in_specs=([pl.BlockSpec((None, TQ, d_att), lambda b, t: (b, t, 0))]
                  + [window(1, j) for j in range(n_win)]
                  + [window(2, j) for j in range(n_win)]
                  + [_single(rbp.shape, lambda b, t: (0, 0))]),
        out_specs=pl.BlockSpec((None, TQ, d_att), lambda b, t: (b, t, 0)),
        out_shape=jax.ShapeDtypeStruct((bsz, seq, d_att), BF16),
        scratch_shapes=[pltpu.VMEM((n_heads, 2, CHUNK, WIN), F32)],
        compiler_params=pltpu.CompilerParams(dimension_semantics=("arbitrary", "arbitrary"),
                                             vmem_limit_bytes=VMEM_LIMIT_BYTES),
        name="chunk_attn",
    )(*([qkv] * (1 + 2 * n_win)), rbp)


def _outproj_ffn_kernel(x_ref, yr_ref, ya_ref, mod_ref, n2g_ref, fg_ref, woa_ref, wob_ref,
                        wu_ref, fcw_ref, fcb_ref, wd_ref, o_ref, ub, halo_s, z_s, x1_s,
                        *, d_model, d_ff, conv_w):
    i = pl.program_id(0)
    rows = TT * SUBLANES
    halo = (conv_w - 1) * SUBLANES

    @pl.when(i == 0)
    def _():
        halo_s[...] = jnp.zeros_like(halo_s)

    yr = yr_ref[...].reshape(rows, yr_ref.shape[2])
    ya = ya_ref[...].reshape(rows, ya_ref.shape[2])
    y = (jnp.dot(yr, woa_ref[...], preferred_element_type=F32)
         + jnp.dot(ya, wob_ref[...], preferred_element_type=F32))
    g1 = mod_ref[:, :, 2 * d_model:3 * d_model]
    sh2 = mod_ref[:, :, 3 * d_model:4 * d_model]
    sc2 = mod_ref[:, :, 4 * d_model:5 * d_model]
    g2 = mod_ref[:, :, 5 * d_model:6 * d_model]
    x1 = x_ref[...] + g1 * y.reshape(SUBLANES, TT, d_model)
    x1_s[...] = x1
    ms = jnp.mean(x1 * x1, axis=-1, keepdims=True)
    hn = x1 * lax.rsqrt(ms + EPS) * n2g_ref[...]
    hn = hn * (1.0 + sc2) + sh2
    hb = jnp.swapaxes(hn, 0, 1).reshape(rows, d_model).astype(BF16)

    first = True
    for c0, w in FF_SPLITS:
        act = None
        for part, off in ((0, 0), (1, d_ff)):
            lo = part * FF_W
            cols = slice(off + c0, off + c0 + w)
            ub[halo:halo + rows, lo:lo + w] = jnp.dot(hb, wu_ref[:, cols],
                                                      preferred_element_type=F32)
            ub[0:halo, lo:lo + w] = halo_s[:, cols]
            halo_s[:, cols] = ub[rows:rows + halo, lo:lo + w]
            u = fcb_ref[:, cols]
            for k in range(conv_w):
                u = u + ub[k * SUBLANES:k * SUBLANES + rows, lo:lo + w] * fcw_ref[k:k + 1, cols]
            act = u * jax.nn.sigmoid(u) if part == 0 else act * u
        zc = jnp.dot(act.astype(BF16), wd_ref[c0:c0 + w, :], preferred_element_type=F32)
        if first:
            z_s[...] = zc
            first = False
        else:
            z_s[...] += zc

    z = jnp.swapaxes(z_s[...].reshape(TT, SUBLANES, d_model), 0, 1)
    x2 = x1_s[...] + g2 * z
    ms2 = jnp.mean(x2 * x2, axis=-1, keepdims=True)
    o_ref[...] = x2 * lax.rsqrt(ms2 + EPS) * fg_ref[...]


def _outproj_ffn(x, yr, ya, mod, n2g, fg, wo_a, wo_b, w_up_b, fcw, fcb, w_down_b):
    bsz, seq, d = x.shape
    d_ff = w_down_b.shape[0]
    cw = fcw.shape[0]
    d_rnn = yr.shape[2]
    d_att = ya.shape[2]
    rows = TT * SUBLANES
    assert sum(w for _, w in FF_SPLITS) == d_ff
    const = lambda i: (0, 0)
    kern = functools.partial(_outproj_ffn_kernel, d_model=d, d_ff=d_ff, conv_w=cw)
    return pl.pallas_call(
        kern,
        grid=(seq // TT,),
        in_specs=[
            pl.BlockSpec((bsz, TT, d), lambda i: (0, i, 0)),
            pl.BlockSpec((bsz, TT, d_rnn), lambda i: (0, i, 0)),
            pl.BlockSpec((bsz, TT, d_att), lambda i: (0, i, 0)),
            _single(mod.shape, lambda i: (0, 0, 0)),
            _single((1, d), const),
            _single((1, d), const),
            _single(wo_a.shape, const),
            _single(wo_b.shape, const),
            _single(w_up_b.shape, const),
            _single(fcw.shape, const),
            _single((1, 2 * d_ff), const),
            _single(w_down_b.shape, const),
        ],
        out_specs=pl.BlockSpec((bsz, TT, d), lambda i: (0, i, 0)),
        out_shape=jax.ShapeDtypeStruct((bsz, seq, d), F32),
        scratch_shapes=[
            pltpu.VMEM((rows + (cw - 1) * SUBLANES, 2 * FF_W), F32),
            pltpu.VMEM(((cw - 1) * SUBLANES, 2 * d_ff), F32),
            pltpu.VMEM((rows, d), F32),
            pltpu.VMEM((bsz, TT, d), F32),
        ],
        compiler_params=pltpu.CompilerParams(dimension_semantics=("arbitrary",),
                                             vmem_limit_bytes=VMEM_LIMIT_BYTES),
        name="outproj_ffn",
    )(x, yr, ya, mod, n2g.reshape(1, d), fg.reshape(1, d), wo_a, wo_b, w_up_b, fcw,
      fcb.reshape(1, 2 * d_ff), w_down_b)


def _gate_weights(wa, wx):
    nb = wa.shape[0]
    hb = nb // 2
    halves = []
    for s in (slice(0, hb), slice(hb, nb)):
        halves.append(jnp.concatenate([block_diag(*wa[s]), block_diag(*wx[s])], axis=1).astype(BF16))
    return halves


def kernel(x, c, ada_w, ada_b, norm1_g, w_in, rnn_conv_w, rnn_conv_b, rg_wa, rg_ba, rg_wx, rg_bx, rg_lambda, rel_bias, w_out, norm2_g, w_up, ffn_conv_w, ffn_conv_b, w_down, final_g):
    bsz, seq, d = x.shape
    assert ada_w.shape[0] == 1
    l = 0
    d_rnn = rnn_conv_w.shape[2]
    d_att = w_out.shape[1] - d_rnn
    mod = _ada(c, ada_w[l], ada_b[l]).reshape(bsz, 1, ada_w.shape[2])
    wg0, wg1 = _gate_weights(rg_wa[l], rg_wx[l])
    qkv, yr = _inproj_rglru(x, mod, norm1_g[l], w_in[l].astype(BF16), rnn_conv_w[l],
                            rnn_conv_b[l], wg0, wg1, rg_ba[l], rg_bx[l], rg_lambda[l])
    rbp = jnp.pad(rel_bias[l], ((0, 0), (0, RBPAD - rel_bias.shape[2])))
    ya = _attention(qkv, rbp)
    w_out_b = w_out[l].astype(BF16)
    return _outproj_ffn(x, yr, ya, mod, norm2_g[l], final_g,
                        w_out_b[0:d_rnn], w_out_b[d_rnn:], w_up[l].astype(BF16), ffn_conv_w[l],
                        ffn_conv_b[l], w_down[l].astype(BF16))
```

```python
import functools

import jax
import jax.numpy as jnp
from jax import lax
from jax.experimental import pallas as pl
from jax.experimental.pallas import tpu as pltpu
from jax.scipy.linalg import block_diag

F32 = jnp.float32
BF16 = jnp.bfloat16

CHUNK = 64
LOOKBACK = 8
REL_CLIP = 128
HEAD_DIM = 64
RG_C = 8.0
EPS = 1e-6
NEG_INF = -1e30
LOG2E = 1.4426950408889634

SUBLANES = 8
LANES = 128
MXU_DIM = 256
VMEM_LIMIT_BYTES = 56 * 1024 * 1024

TT = 64
RT = 16
T1 = 128
CHUNKS_PER_TILE = 4
TQ = CHUNKS_PER_TILE * CHUNK
HEADS_PER_VREG = LANES // HEAD_DIM
HIST = LOOKBACK * CHUNK
NBUF = HIST + TQ
WIN = (LOOKBACK + 2) * CHUNK
GPAD = 768
RBPAD = 384
FF_SPLITS = ((0, 1024), (1024, 1024), (2048, 768))
K3_TILES = 2
HT = 16


def _single(block_shape, index_map):
    return pl.BlockSpec(block_shape, index_map, pipeline_mode=pl.Buffered(1))


def _ada_kernel(c_ref, w_ref, b_ref, o_ref):
    c = c_ref[...]
    sc = c * jax.nn.sigmoid(c)
    o_ref[...] = jnp.dot(sc, w_ref[...], precision=lax.Precision.HIGHEST,
                         preferred_element_type=F32) + b_ref[...]


def _ada(c, ada_w, ada_b):
    bsz, d = c.shape
    n = ada_w.shape[1]
    bn = 2048
    return pl.pallas_call(
        _ada_kernel,
        grid=(n // bn,),
        in_specs=[pl.BlockSpec((bsz, d), lambda j: (0, 0)),
                  pl.BlockSpec((d, bn), lambda j: (0, j)),
                  pl.BlockSpec((1, bn), lambda j: (0, j))],
        out_specs=pl.BlockSpec((bsz, bn), lambda j: (0, j)),
        out_shape=jax.ShapeDtypeStruct((bsz, n), F32),
        name="ada_mod",
    )(c, ada_w, ada_b.reshape(1, n))


def _inproj_rglru_kernel(x_ref, mod_ref, n1g_ref, win_ref, cw_ref, cb_ref, wg0_ref, wg1_ref,
                         ba_ref, bx_ref, lam_ref, qkv_ref, yr_ref,
                         tail_s, hc, *, d_model, d_rnn, d_att, conv_w):
    i = pl.program_id(0)
    rows = T1 * SUBLANES
    halo = (conv_w - 1) * SUBLANES
    prow = RT * SUBLANES

    @pl.when(i == 0)
    def _():
        tail_s[...] = jnp.zeros_like(tail_s)
        hc[...] = jnp.zeros_like(hc)

    x = x_ref[...]
    ms = jnp.mean(x * x, axis=-1, keepdims=True)
    sh1 = mod_ref[:, :, 0:d_model]
    sc1 = mod_ref[:, :, d_model:2 * d_model]
    h = x * lax.rsqrt(ms + EPS) * (n1g_ref[...] * (1.0 + sc1)) + sh1
    hb = h.reshape(rows, d_model).astype(BF16)

    def project_piece(j):
        part = jnp.dot(hb, win_ref[:, j * d_rnn:(j + 1) * d_rnn], preferred_element_type=F32)
        return part.reshape(SUBLANES, T1, d_rnn)

    def emit_qkv(j):
        part = project_piece(j)
        if j == 2:
            part = part * (LOG2E * HEAD_DIM ** -0.5)
        qkv_ref[:, :, (j - 2) * d_att:(j - 1) * d_att] = part.astype(BF16)

    def recur_piece(xr_b, gr_b, c, tail, hcur):
        ts = slice(c * RT, (c + 1) * RT)
        xr = jnp.swapaxes(xr_b[:, ts, :], 0, 1).reshape(prow, d_rnn)
        gr = jnp.swapaxes(gr_b[:, ts, :], 0, 1).reshape(prow, d_rnn)
        ext = jnp.concatenate([tail, xr], axis=0)
        xc = cb_ref[...]
        for k in range(conv_w):
            xc = xc + ext[k * SUBLANES:k * SUBLANES + prow, :] * cw_ref[k:k + 1, :]
        xcb = xc.astype(BF16)
        half = d_rnn // 2
        g0 = jnp.dot(xcb[:, 0:half], wg0_ref[...], preferred_element_type=F32)
        g1 = jnp.dot(xcb[:, half:d_rnn], wg1_ref[...], preferred_element_type=F32)
        r = jax.nn.sigmoid(jnp.concatenate([g0[:, 0:half], g1[:, 0:half]], axis=1) + ba_ref[...])
        ig = jax.nn.sigmoid(jnp.concatenate([g0[:, half:], g1[:, half:]], axis=1) + bx_ref[...])
        lam = lam_ref[...]
        log_sig = jnp.minimum(lam, 0.0) - jnp.log1p(jnp.exp(-jnp.abs(lam)))
        log_a = RG_C * r * log_sig
        a = jnp.exp(log_a)
        w = jnp.tanh(-log_a) * (a * a + 1.0)
        mult = jnp.where(w > 0.0, w * lax.rsqrt(w), 0.0)
        a3 = a.reshape(RT, SUBLANES, d_rnn)
        b3 = (mult * (ig * xc)).reshape(RT, SUBLANES, d_rnn)
        hs = []
        for t in range(RT):
            hcur = a3[t] * hcur + b3[t]
            hs.append(hcur)
        yr = jnp.stack(hs, axis=0) * jax.nn.gelu(gr).reshape(RT, SUBLANES, d_rnn)
        yr_ref[:, ts, :] = jnp.swapaxes(yr, 0, 1).astype(BF16)
        return ext[prow:prow + halo, :], hcur

    xr_b = project_piece(0)
    gr_b = project_piece(1)
    tail, hcur = tail_s[...], hc[...]
    n_rec = T1 // RT
    n_qkv = (3 * d_att) // d_rnn
    for c in range(max(n_rec, n_qkv)):
        if c < n_qkv:
            emit_qkv(2 + c)
        if c < n_rec:
            tail, hcur = recur_piece(xr_b, gr_b, c, tail, hcur)
    tail_s[...] = tail
    hc[...] = hcur


def _inproj_rglru(x, mod, n1g, w_in_b, conv_w, conv_b, wg0, wg1, ba, bx, lam):
    bsz, seq, d = x.shape
    assert bsz == SUBLANES and seq % T1 == 0
    d_in = w_in_b.shape[1]
    d_rnn = conv_w.shape[1]
    d_att = (d_in - 2 * d_rnn) // 3
    assert d_att == d_rnn and T1 % RT == 0
    cw = conv_w.shape[0]
    const = lambda i: (0, 0)
    kern = functools.partial(_inproj_rglru_kernel, d_model=d, d_rnn=d_rnn, d_att=d_att, conv_w=cw)
    return pl.pallas_call(
        kern,
        grid=(seq // T1,),
        in_specs=[
            pl.BlockSpec((bsz, T1, d), lambda i: (0, i, 0)),
            _single(mod.shape, lambda i: (0, 0, 0)),
            _single((1, d), const),
            _single(w_in_b.shape, const),
            _single(conv_w.shape, const),
            _single((1, d_rnn), const),
            _single(wg0.shape, const),
            _single(wg1.shape, const),
            _single((1, d_rnn), const),
            _single((1, d_rnn), const),
            _single((1, d_rnn), const),
        ],
        out_specs=[pl.BlockSpec((bsz, T1, 3 * d_att), lambda i: (0, i, 0)),
                   pl.BlockSpec((bsz, T1, d_rnn), lambda i: (0, i, 0))],
        out_shape=[jax.ShapeDtypeStruct((bsz, seq, 3 * d_att), BF16),
                   jax.ShapeDtypeStruct((bsz, seq, d_rnn), BF16)],
        scratch_shapes=[
            pltpu.VMEM(((cw - 1) * SUBLANES, d_rnn), F32),
            pltpu.VMEM((SUBLANES, d_rnn), F32),
        ],
        compiler_params=pltpu.CompilerParams(dimension_semantics=("arbitrary",),
                                             vmem_limit_bytes=VMEM_LIMIT_BYTES),
        name="inproj_rglru",
    )(x, mod, n1g.reshape(1, d), w_in_b, conv_w, conv_b.reshape(1, d_rnn), wg0, wg1,
      ba.reshape(1, d_rnn), bx.reshape(1, d_rnn), lam.reshape(1, d_rnn))


def _build_bias_tables(rbp_ref, bias_s, n_heads):
    r_io = lax.broadcasted_iota(jnp.int32, (RBPAD, GPAD), 0)
    j_io = lax.broadcasted_iota(jnp.int32, (RBPAD, GPAD), 1)
    idx = jnp.clip(HIST + CHUNK - 1 - j_io, -REL_CLIP, REL_CLIP) + REL_CLIP
    sel = (r_io == idx).astype(F32)
    g = jnp.dot(rbp_ref[...], sel, precision=lax.Precision.HIGHEST, preferred_element_type=F32)
    g = (g - g[:, 0:1]) * LOG2E
    col = lax.broadcasted_iota(jnp.int32, (CHUNK, WIN), 1)
    band = (LOOKBACK + 1) * CHUNK
    for h in range(n_heads):
        gh = jnp.broadcast_to(g[h:h + 1, :], (CHUNK, GPAD))
        even = pltpu.roll(gh, GPAD - (CHUNK - 1), 1, stride=1, stride_axis=0)[:, 0:WIN]
        odd = pltpu.roll(gh, 1, 1, stride=1, stride_axis=0)[:, 0:WIN]
        bias_s[h, 0] = jnp.where(col < band, even, NEG_INF)
        bias_s[h, 1] = jnp.where(col >= CHUNK, odd, NEG_INF)


ZERO_BIAS_COLS = ((0, HIST - REL_CLIP), (LANES, HIST - REL_CLIP))


def _attn_tile(q_ref, k_refs, v_refs, o_ref, bias_s, first_valid, n_heads):
    lane = lax.broadcasted_iota(jnp.int32, (1, LANES), 1)
    pad = jnp.zeros((CHUNK, NBUF - WIN), BF16)
    n_kt = len(k_refs)

    def masked_q(h):
        pair, hh = divmod(h, HEADS_PER_VREG)
        qp = q_ref[:, pair * LANES:(pair + 1) * LANES]
        in_head = jnp.logical_and(lane >= hh * HEAD_DIM, lane < (hh + 1) * HEAD_DIM)
        return jnp.where(in_head, qp, jnp.zeros_like(qp))

    def score_piece(h, qm, j):
        pair = h // HEADS_PER_VREG
        return lax.dot_general(qm, k_refs[j][:, pair * LANES:(pair + 1) * LANES],
                               (((1,), (1,)), ((), ())), preferred_element_type=F32)

    def softmax_piece(h, s_all, c):
        w0 = (c // 2) * LANES
        raw = s_all[c * CHUNK:(c + 1) * CHUNK, w0:w0 + WIN]
        z0, z1 = ZERO_BIAS_COLS[c % 2]
        parts = [raw[:, z0:z1], raw[:, z1:WIN] + bias_s[h, c % 2, :, z1:WIN]]
        if z0:
            parts.insert(0, raw[:, 0:z0] + bias_s[h, c % 2, :, 0:z0])
        s = jnp.concatenate(parts, axis=1)
        if first_valid is not None:
            col = lax.broadcasted_iota(jnp.int32, (1, WIN), 1) + w0
            s = jnp.where(col >= first_valid, s, NEG_INF)
        m = jnp.max(s, axis=-1, keepdims=True)
        p = jnp.exp2(s - m)
        inv = 1.0 / jnp.sum(p, axis=-1, keepdims=True)
        pb = p.astype(BF16)
        return inv, jnp.concatenate([pb, pad] if w0 == 0 else [pad, pb], axis=1)

    def value_piece(h, pmat, j):
        pair = h // HEADS_PER_VREG
        return jnp.dot(pmat[:, j * TQ:(j + 1) * TQ], v_refs[j][:, pair * LANES:(pair + 1) * LANES],
                       preferred_element_type=F32)

    def finish(h, o, inv, o_prev):
        pair, hh = divmod(h, HEADS_PER_VREG)
        o = o * inv
        if hh == 0:
            return o
        in_head = jnp.logical_and(lane >= hh * HEAD_DIM, lane < (hh + 1) * HEAD_DIM)
        o_ref[:, pair * LANES:(pair + 1) * LANES] = jnp.where(in_head, o, o_prev).astype(BF16)
        return None

    assert n_kt <= CHUNKS_PER_TILE and HEADS_PER_VREG == 2
    qm = masked_q(0)
    s_next = jnp.concatenate([score_piece(0, qm, j) for j in range(n_kt)], axis=1)
    o_pair = None
    prev = None
    for h in range(n_heads + 1):
        cur = h < n_heads
        nxt = h + 1 < n_heads
        s_cur = s_next
        if nxt:
            qm = masked_q(h + 1)
        pieces, invs, prow, o_acc = [], [], [], None
        for c in range(CHUNKS_PER_TILE):
            if nxt and c < n_kt:
                pieces.append(score_piece(h + 1, qm, c))
            if prev is not None and c < n_kt:
                part = value_piece(prev[0], prev[1], c)
                o_acc = part if o_acc is None else o_acc + part
            if cur:
                inv, pr = softmax_piece(h, s_cur, c)
                invs.append(inv)
                prow.append(pr)
        if prev is not None:
            o_pair = finish(prev[0], o_acc, prev[2], o_pair)
        if nxt:
            s_next = jnp.concatenate(pieces, axis=1)
        prev = (h, jnp.concatenate(prow, axis=0), jnp.concatenate(invs, axis=0)) if cur else None


def _attn_kernel(*refs, n_heads):
    n_win = NBUF // TQ
    q_ref = refs[0]
    k_refs = refs[1:1 + n_win]
    v_refs = refs[1 + n_win:1 + 2 * n_win]
    rbp_ref, o_ref, bias_s = refs[1 + 2 * n_win:]
    b = pl.program_id(0)
    t = pl.program_id(1)

    @pl.when(jnp.logical_and(b == 0, t == 0))
    def _():
        _build_bias_tables(rbp_ref, bias_s, n_heads)

    @pl.when(t < n_win - 1)
    def _():
        first_valid = TQ * (n_win - 1 - t)
        _attn_tile(q_ref, k_refs, v_refs, o_ref, bias_s, first_valid, n_heads)

    @pl.when(t >= n_win - 1)
    def _():
        _attn_tile(q_ref, k_refs, v_refs, o_ref, bias_s, None, n_heads)


def _attention(qkv, rbp):
    bsz, seq, d3 = qkv.shape
    d_att = d3 // 3
    n_heads = d_att // HEAD_DIM
    n_win = NBUF // TQ
    assert seq % TQ == 0 and NBUF % TQ == 0 and n_heads % HEADS_PER_VREG == 0
    kern = functools.partial(_attn_kernel, n_heads=n_heads)

    def window(col, j):
        return pl.BlockSpec((None, TQ, d_att),
                            lambda b, t: (b, jnp.maximum(t - (n_win - 1 - j), 0), col))

    return pl.pallas_call(
        kern,
        grid=(bsz, seq // TQ),
        in_specs=([pl.BlockSpec((None, TQ, d_att), lambda b, t: (b, t, 0))]
                  + [window(1, j) for j in range(n_win)]
                  + [window(2, j) for j in range(n_win)]
                  + [_single(rbp.shape, lambda b, t: (0, 0))]),
        out_specs=pl.BlockSpec((None, TQ, d_att), lambda b, t: (b, t, 0)),
        out_shape=jax.ShapeDtypeStruct((bsz, seq, d_att), BF16),
        scratch_shapes=[pltpu.VMEM((n_heads, 2, CHUNK, WIN), F32)],
        compiler_params=pltpu.CompilerParams(dimension_semantics=("arbitrary", "arbitrary"),
                                             vmem_limit_bytes=VMEM_LIMIT_BYTES),
        name="chunk_attn",
    )(*([qkv] * (1 + 2 * n_win)), rbp)


def _outproj_ffn_kernel(x_ref, yr_ref, ya_ref, mod_ref, n2g_ref, fg_ref, woa_ref, wob_ref,
                        wu_ref, fcw_ref, fcb_ref, wd_ref, o_ref, halo_s,
                        *, d_model, d_ff, conv_w):
    i = pl.program_id(0)
    rows = TT * SUBLANES
    halo = (conv_w - 1) * SUBLANES
    n_hp = TT // HT

    @pl.when(i == 0)
    def _():
        halo_s[...] = jnp.zeros_like(halo_s)

    g1 = mod_ref[:, :, 2 * d_model:3 * d_model]
    sh2 = mod_ref[:, :, 3 * d_model:4 * d_model]
    gain2 = n2g_ref[...] * (1.0 + mod_ref[:, :, 4 * d_model:5 * d_model])
    g2 = mod_ref[:, :, 5 * d_model:6 * d_model]

    def out_proj(t):
        ts = slice(t * TT, (t + 1) * TT)
        yr = yr_ref[:, ts, :].reshape(rows, yr_ref.shape[2])
        ya = ya_ref[:, ts, :].reshape(rows, ya_ref.shape[2])
        y = (jnp.dot(yr, woa_ref[...], preferred_element_type=F32)
             + jnp.dot(ya, wob_ref[...], preferred_element_type=F32))
        return y.reshape(SUBLANES, TT, d_model)

    def head_piece(t, c, y3):
        cs = slice(c * HT, (c + 1) * HT)
        x1 = x_ref[:, t * TT + c * HT:t * TT + (c + 1) * HT, :] + g1 * y3[:, cs, :]
        ms = jnp.mean(x1 * x1, axis=-1, keepdims=True)
        hn = x1 * lax.rsqrt(ms + EPS) * gain2 + sh2
        return x1, jnp.swapaxes(hn, 0, 1).reshape(HT * SUBLANES, d_model).astype(BF16)

    def tail_piece(t, c, z, x1):
        zc = z[c * HT * SUBLANES:(c + 1) * HT * SUBLANES, :].reshape(HT, SUBLANES, d_model)
        x2 = x1 + g2 * jnp.swapaxes(zc, 0, 1)
        ms2 = jnp.mean(x2 * x2, axis=-1, keepdims=True)
        o_ref[:, t * TT + c * HT:t * TT + (c + 1) * HT, :] = x2 * lax.rsqrt(ms2 + EPS) * fg_ref[...]

    def up_piece(hb, cols):
        return jnp.dot(hb, wu_ref[:, cols], preferred_element_type=F32)

    def conv_piece(up, tail, cols):
        ext = jnp.concatenate([tail, up], axis=0)
        u = fcb_ref[:, cols]
        for k in range(conv_w):
            u = u + ext[k * SUBLANES:k * SUBLANES + rows, :] * fcw_ref[k:k + 1, cols]
        return u, ext[rows:rows + halo, :]

    def ffn(hb, tails, side):
        side = list(side)
        z = None
        new_tails = []
        for j, (c0, w) in enumerate(FF_SPLITS):
            cg = slice(c0, c0 + w)
            cv = slice(d_ff + c0, d_ff + c0 + w)
            up_g = up_piece(hb, cg)
            if side:
                side.pop(0)()
            up_v = up_piece(hb, cv)
            if side:
                side.pop(0)()
            ug, tg = conv_piece(up_g, tails[j][0], cg)
            uv, tv = conv_piece(up_v, tails[j][1], cv)
            new_tails.append((tg, tv))
            act = (ug * jax.nn.sigmoid(ug) * uv).astype(BF16)
            zc = jnp.dot(act, wd_ref[c0:c0 + w, :], preferred_element_type=F32)
            z = zc if z is None else z + zc
        assert not side
        return z, new_tails

    tails = [(halo_s[:, c0:c0 + w], halo_s[:, d_ff + c0:d_ff + c0 + w]) for c0, w in FF_SPLITS]
    y3 = out_proj(0)
    hp = [head_piece(0, c, y3) for c in range(n_hp)]
    x1_0 = [p[0] for p in hp]
    hb_0 = jnp.concatenate([p[1] for p in hp], axis=0)

    state = {}

    def side_out_proj():
        state["y3"] = out_proj(1)

    def side_head(c):
        def run():
            state.setdefault("hp", []).append(head_piece(1, c, state["y3"]))
        return run

    z_0, tails = ffn(hb_0, tails, [side_out_proj] + [side_head(c) for c in range(n_hp)])
    x1_1 = [p[0] for p in state["hp"]]
    hb_1 = jnp.concatenate([p[1] for p in state["hp"]], axis=0)

    def side_tail(c):
        return lambda: tail_piece(0, c, z_0, x1_0[c])

    z_1, tails = ffn(hb_1, tails, [side_tail(c) for c in range(n_hp)])
    for c in range(n_hp):
        tail_piece(1, c, z_1, x1_1[c])
    for (c0, w), (tg, tv) in zip(FF_SPLITS, tails):
        halo_s[:, c0:c0 + w] = tg
        halo_s[:, d_ff + c0:d_ff + c0 + w] = tv


def _outproj_ffn(x, yr, ya, mod, n2g, fg, wo_a, wo_b, w_up_b, fcw, fcb, w_down_b):
    bsz, seq, d = x.shape
    d_ff = w_down_b.shape[0]
    cw = fcw.shape[0]
    d_rnn = yr.shape[2]
    d_att = ya.shape[2]
    step_t = K3_TILES * TT
    assert sum(w for _, w in FF_SPLITS) == d_ff and seq % step_t == 0 and TT % HT == 0
    const = lambda i: (0, 0)
    kern = functools.partial(_outproj_ffn_kernel, d_model=d, d_ff=d_ff, conv_w=cw)
    return pl.pallas_call(
        kern,
        grid=(seq // step_t,),
        in_specs=[
            pl.BlockSpec((bsz, step_t, d), lambda i: (0, i, 0)),
            pl.BlockSpec((bsz, step_t, d_rnn), lambda i: (0, i, 0)),
            pl.BlockSpec((bsz, step_t, d_att), lambda i: (0, i, 0)),
            _single(mod.shape, lambda i: (0, 0, 0)),
            _single((1, d), const),
            _single((1, d), const),
            _single(wo_a.shape, const),
            _single(wo_b.shape, const),
            _single(w_up_b.shape, const),
            _single(fcw.shape, const),
            _single((1, 2 * d_ff), const),
            _single(w_down_b.shape, const),
        ],
        out_specs=pl.BlockSpec((bsz, step_t, d), lambda i: (0, i, 0)),
        out_shape=jax.ShapeDtypeStruct((bsz, seq, d), F32),
        scratch_shapes=[pltpu.VMEM(((cw - 1) * SUBLANES, 2 * d_ff), F32)],
        compiler_params=pltpu.CompilerParams(dimension_semantics=("arbitrary",),
                                             vmem_limit_bytes=VMEM_LIMIT_BYTES),
        name="outproj_ffn",
    )(x, yr, ya, mod, n2g.reshape(1, d), fg.reshape(1, d), wo_a, wo_b, w_up_b, fcw,
      fcb.reshape(1, 2 * d_ff), w_down_b)


def _gate_weights(wa, wx):
    nb = wa.shape[0]
    hb = nb // 2
    halves = []
    for s in (slice(0, hb), slice(hb, nb)):
        halves.append(jnp.concatenate([block_diag(*wa[s]), block_diag(*wx[s])], axis=1).astype(BF16))
    return halves


def kernel(x, c, ada_w, ada_b, norm1_g, w_in, rnn_conv_w, rnn_conv_b, rg_wa, rg_ba, rg_wx, rg_bx, rg_lambda, rel_bias, w_out, norm2_g, w_up, ffn_conv_w, ffn_conv_b, w_down, final_g):
    bsz, seq, d = x.shape
    assert ada_w.shape[0] == 1
    l = 0
    d_rnn = rnn_conv_w.shape[2]
    d_att = w_out.shape[1] - d_rnn
    mod = _ada(c, ada_w[l], ada_b[l]).reshape(bsz, 1, ada_w.shape[2])
    wg0, wg1 = _gate_weights(rg_wa[l], rg_wx[l])
    qkv, yr = _inproj_rglru(x, mod, norm1_g[l], w_in[l].astype(BF16), rnn_conv_w[l],
                            rnn_conv_b[l], wg0, wg1, rg_ba[l], rg_bx[l], rg_lambda[l])
    rbp = jnp.pad(rel_bias[l], ((0, 0), (0, RBPAD - rel_bias.shape[2])))
    ya = _attention(qkv, rbp)
    w_out_b = w_out[l].astype(BF16)
    return _outproj_ffn(x, yr, ya, mod, norm2_g[l], final_g,
                        w_out_b[0:d_rnn], w_out_b[d_rnn:], w_up[l].astype(BF16), ffn_conv_w[l],
                        ffn_conv_b[l], w_down[l].astype(BF16))
```

```python
import functools

import jax
import jax.numpy as jnp
from jax import lax
from jax.experimental import pallas as pl
from jax.experimental.pallas import tpu as pltpu
from jax.scipy.linalg import block_diag

F32 = jnp.float32
BF16 = jnp.bfloat16

CHUNK = 64
LOOKBACK = 8
REL_CLIP = 128
HEAD_DIM = 64
RG_C = 8.0
EPS = 1e-6
NEG_INF = -1e30
LOG2E = 1.4426950408889634

SUBLANES = 8
LANES = 128
MXU_DIM = 256
VMEM_LIMIT_BYTES = 56 * 1024 * 1024

TT = 64
RT = 16
T1 = 128
CHUNKS_PER_TILE = 4
TQ = CHUNKS_PER_TILE * CHUNK
HEADS_PER_VREG = LANES // HEAD_DIM
HIST = LOOKBACK * CHUNK
NBUF = HIST + TQ
WIN = (LOOKBACK + 2) * CHUNK
GPAD = 768
RBPAD = 384
FF_SPLITS = ((0, 1024), (1024, 1024), (2048, 768))
K3_TILES = 2
HT = 16


def _single(block_shape, index_map):
    return pl.BlockSpec(block_shape, index_map, pipeline_mode=pl.Buffered(1))


def _ada_kernel(c_ref, w_ref, b_ref, o_ref):
    c = c_ref[...]
    sc = c * jax.nn.sigmoid(c)
    w = w_ref[...]
    w_hi = w.astype(BF16)
    w_lo = (w - w_hi.astype(F32)).astype(BF16)
    sc_hi = sc.astype(BF16)
    sc_lo = (sc - sc_hi.astype(F32)).astype(BF16)
    acc = jnp.dot(sc_hi, w_hi, preferred_element_type=F32)
    acc = acc + jnp.dot(sc_lo, w_hi, preferred_element_type=F32)
    acc = acc + jnp.dot(sc_hi, w_lo, preferred_element_type=F32)
    o_ref[...] = (acc + b_ref[...]).reshape(o_ref.shape)


def _ada(c, ada_w, ada_b):
    bsz, d = c.shape
    n = ada_w.shape[1]
    bn = 2048
    return pl.pallas_call(
        _ada_kernel,
        grid=(n // bn,),
        in_specs=[pl.BlockSpec((bsz, d), lambda j: (0, 0)),
                  pl.BlockSpec((d, bn), lambda j: (0, j)),
                  pl.BlockSpec((1, bn), lambda j: (0, j))],
        out_specs=pl.BlockSpec((bsz, 1, bn), lambda j: (0, 0, j)),
        out_shape=jax.ShapeDtypeStruct((bsz, 1, n), F32),
        name="ada_mod",
    )(c, ada_w, ada_b.reshape(1, n))


def _inproj_rglru_kernel(x_ref, mod_ref, n1g_ref, win_ref, cw_ref, cb_ref, wg0_ref, wg1_ref,
                         ba_ref, bx_ref, lam_ref, qkv_ref, yr_ref,
                         tail_s, hc, *, d_model, d_rnn, d_att, conv_w):
    i = pl.program_id(0)
    rows = T1 * SUBLANES
    halo = (conv_w - 1) * SUBLANES
    prow = RT * SUBLANES

    @pl.when(i == 0)
    def _():
        tail_s[...] = jnp.zeros_like(tail_s)
        hc[...] = jnp.zeros_like(hc)

    x = x_ref[...]
    ms = jnp.mean(x * x, axis=-1, keepdims=True)
    sh1 = mod_ref[:, :, 0:d_model]
    sc1 = mod_ref[:, :, d_model:2 * d_model]
    h = x * lax.rsqrt(ms + EPS) * (n1g_ref[...] * (1.0 + sc1)) + sh1
    hb = h.reshape(rows, d_model).astype(BF16)

    def project_piece(j):
        part = jnp.dot(hb, win_ref[:, j * d_rnn:(j + 1) * d_rnn], preferred_element_type=F32)
        return part.reshape(SUBLANES, T1, d_rnn)

    def emit_qkv(j):
        part = project_piece(j)
        if j == 2:
            part = part * (LOG2E * HEAD_DIM ** -0.5)
        qkv_ref[:, :, (j - 2) * d_att:(j - 1) * d_att] = part.astype(BF16)

    def recur_piece(xr_b, gr_b, c, tail, hcur):
        ts = slice(c * RT, (c + 1) * RT)
        xr = jnp.swapaxes(xr_b[:, ts, :], 0, 1).reshape(prow, d_rnn)
        gr = jnp.swapaxes(gr_b[:, ts, :], 0, 1).reshape(prow, d_rnn)
        ext = jnp.concatenate([tail, xr], axis=0)
        xc = cb_ref[...]
        for k in range(conv_w):
            xc = xc + ext[k * SUBLANES:k * SUBLANES + prow, :] * cw_ref[k:k + 1, :]
        xcb = xc.astype(BF16)
        half = d_rnn // 2
        g0 = jnp.dot(xcb[:, 0:half], wg0_ref[...], preferred_element_type=F32)
        g1 = jnp.dot(xcb[:, half:d_rnn], wg1_ref[...], preferred_element_type=F32)
        r = jax.nn.sigmoid(jnp.concatenate([g0[:, 0:half], g1[:, 0:half]], axis=1) + ba_ref[...])
        ig = jax.nn.sigmoid(jnp.concatenate([g0[:, half:], g1[:, half:]], axis=1) + bx_ref[...])
        lam = lam_ref[...]
        log_sig = jnp.minimum(lam, 0.0) - jnp.log1p(jnp.exp(-jnp.abs(lam)))
        log_a = RG_C * r * log_sig
        a = jnp.exp(log_a)
        w = jnp.tanh(-log_a) * (a * a + 1.0)
        mult = jnp.where(w > 0.0, w * lax.rsqrt(w), 0.0)
        a3 = a.reshape(RT, SUBLANES, d_rnn)
        b3 = (mult * (ig * xc)).reshape(RT, SUBLANES, d_rnn)
        hs = []
        for t in range(RT):
            hcur = a3[t] * hcur + b3[t]
            hs.append(hcur)
        yr = jnp.stack(hs, axis=0) * jax.nn.gelu(gr).reshape(RT, SUBLANES, d_rnn)
        yr_ref[:, ts, :] = jnp.swapaxes(yr, 0, 1).astype(BF16)
        return ext[prow:prow + halo, :], hcur

    xr_b = project_piece(0)
    gr_b = project_piece(1)
    tail, hcur = tail_s[...], hc[...]
    n_rec = T1 // RT
    n_qkv = (3 * d_att) // d_rnn
    for c in range(max(n_rec, n_qkv)):
        if c < n_qkv:
            emit_qkv(2 + c)
        if c < n_rec:
            tail, hcur = recur_piece(xr_b, gr_b, c, tail, hcur)
    tail_s[...] = tail
    hc[...] = hcur


def _inproj_rglru(x, mod, n1g, w_in_b, conv_w, conv_b, wg0, wg1, ba, bx, lam):
    bsz, seq, d = x.shape
    assert bsz == SUBLANES and seq % T1 == 0
    d_in = w_in_b.shape[1]
    d_rnn = conv_w.shape[1]
    d_att = (d_in - 2 * d_rnn) // 3
    assert d_att == d_rnn and T1 % RT == 0
    cw = conv_w.shape[0]
    const = lambda i: (0, 0)
    kern = functools.partial(_inproj_rglru_kernel, d_model=d, d_rnn=d_rnn, d_att=d_att, conv_w=cw)
    return pl.pallas_call(
        kern,
        grid=(seq // T1,),
        in_specs=[
            pl.BlockSpec((bsz, T1, d), lambda i: (0, i, 0)),
            _single(mod.shape, lambda i: (0, 0, 0)),
            _single((1, d), const),
            _single(w_in_b.shape, const),
            _single(conv_w.shape, const),
            _single((1, d_rnn), const),
            _single(wg0.shape, const),
            _single(wg1.shape, const),
            _single((1, d_rnn), const),
            _single((1, d_rnn), const),
            _single((1, d_rnn), const),
        ],
        out_specs=[pl.BlockSpec((bsz, T1, 3 * d_att), lambda i: (0, i, 0)),
                   pl.BlockSpec((bsz, T1, d_rnn), lambda i: (0, i, 0))],
        out_shape=[jax.ShapeDtypeStruct((bsz, seq, 3 * d_att), BF16),
                   jax.ShapeDtypeStruct((bsz, seq, d_rnn), BF16)],
        scratch_shapes=[
            pltpu.VMEM(((cw - 1) * SUBLANES, d_rnn), F32),
            pltpu.VMEM((SUBLANES, d_rnn), F32),
        ],
        compiler_params=pltpu.CompilerParams(dimension_semantics=("arbitrary",),
                                             vmem_limit_bytes=VMEM_LIMIT_BYTES),
        name="inproj_rglru",
    )(x, mod, n1g.reshape(1, d), w_in_b, conv_w, conv_b.reshape(1, d_rnn), wg0, wg1,
      ba.reshape(1, d_rnn), bx.reshape(1, d_rnn), lam.reshape(1, d_rnn))


def _build_bias_tables(rbp_ref, bias_s, n_heads):
    r_io = lax.broadcasted_iota(jnp.int32, (RBPAD, GPAD), 0)
    j_io = lax.broadcasted_iota(jnp.int32, (RBPAD, GPAD), 1)
    idx = jnp.clip(HIST + CHUNK - 1 - j_io, -REL_CLIP, REL_CLIP) + REL_CLIP
    sel = (r_io == idx).astype(F32)
    g = jnp.dot(rbp_ref[...], sel, precision=lax.Precision.HIGHEST, preferred_element_type=F32)
    g = (g - g[:, 0:1]) * LOG2E
    col = lax.broadcasted_iota(jnp.int32, (CHUNK, WIN), 1)
    band = (LOOKBACK + 1) * CHUNK
    for h in range(n_heads):
        gh = jnp.broadcast_to(g[h:h + 1, :], (CHUNK, GPAD))
        even = pltpu.roll(gh, GPAD - (CHUNK - 1), 1, stride=1, stride_axis=0)[:, 0:WIN]
        odd = pltpu.roll(gh, 1, 1, stride=1, stride_axis=0)[:, 0:WIN]
        bias_s[h, 0] = jnp.where(col < band, even, NEG_INF)
        bias_s[h, 1] = jnp.where(col >= CHUNK, odd, NEG_INF)


ZERO_BIAS_COLS = ((0, HIST - REL_CLIP), (LANES, HIST - REL_CLIP))


def _attn_tile(q_ref, k_refs, v_refs, o_ref, bias_s, first_valid, n_heads):
    lane = lax.broadcasted_iota(jnp.int32, (1, LANES), 1)
    pad = jnp.zeros((CHUNK, NBUF - WIN), BF16)
    n_kt = len(k_refs)

    def masked_q(h):
        pair, hh = divmod(h, HEADS_PER_VREG)
        qp = q_ref[:, pair * LANES:(pair + 1) * LANES]
        in_head = jnp.logical_and(lane >= hh * HEAD_DIM, lane < (hh + 1) * HEAD_DIM)
        return jnp.where(in_head, qp, jnp.zeros_like(qp))

    def score_piece(h, qm, j):
        pair = h // HEADS_PER_VREG
        return lax.dot_general(qm, k_refs[j][:, pair * LANES:(pair + 1) * LANES],
                               (((1,), (1,)), ((), ())), preferred_element_type=F32)

    def softmax_piece(h, s_all, c):
        w0 = (c // 2) * LANES
        raw = s_all[c * CHUNK:(c + 1) * CHUNK, w0:w0 + WIN]
        z0, z1 = ZERO_BIAS_COLS[c % 2]
        parts = [raw[:, z0:z1], raw[:, z1:WIN] + bias_s[h, c % 2, :, z1:WIN]]
        if z0:
            parts.insert(0, raw[:, 0:z0] + bias_s[h, c % 2, :, 0:z0])
        s = jnp.concatenate(parts, axis=1)
        if first_valid is not None:
            col = lax.broadcasted_iota(jnp.int32, (1, WIN), 1) + w0
            s = jnp.where(col >= first_valid, s, NEG_INF)
        m = jnp.max(s, axis=-1, keepdims=True)
        p = jnp.exp2(s - m)
        inv = 1.0 / jnp.sum(p, axis=-1, keepdims=True)
        pb = p.astype(BF16)
        return inv, jnp.concatenate([pb, pad] if w0 == 0 else [pad, pb], axis=1)

    def value_piece(h, pmat, j):
        pair = h // HEADS_PER_VREG
        return jnp.dot(pmat[:, j * TQ:(j + 1) * TQ], v_refs[j][:, pair * LANES:(pair + 1) * LANES],
                       preferred_element_type=F32)

    def finish(h, o, inv, o_prev):
        pair, hh = divmod(h, HEADS_PER_VREG)
        o = o * inv
        if hh == 0:
            return o
        in_head = jnp.logical_and(lane >= hh * HEAD_DIM, lane < (hh + 1) * HEAD_DIM)
        o_ref[:, pair * LANES:(pair + 1) * LANES] = jnp.where(in_head, o, o_prev).astype(BF16)
        return None

    assert n_kt <= CHUNKS_PER_TILE and HEADS_PER_VREG == 2
    qm = masked_q(0)
    s_next = jnp.concatenate([score_piece(0, qm, j) for j in range(n_kt)], axis=1)
    o_pair = None
    prev = None
    for h in range(n_heads + 1):
        cur = h < n_heads
        nxt = h + 1 < n_heads
        s_cur = s_next
        if nxt:
            qm = masked_q(h + 1)
        pieces, invs, prow, o_acc = [], [], [], None
        for c in range(CHUNKS_PER_TILE):
            if nxt and c < n_kt:
                pieces.append(score_piece(h + 1, qm, c))
            if prev is not None and c < n_kt:
                part = value_piece(prev[0], prev[1], c)
                o_acc = part if o_acc is None else o_acc + part
            if cur:
                inv, pr = softmax_piece(h, s_cur, c)
                invs.append(inv)
                prow.append(pr)
        if prev is not None:
            o_pair = finish(prev[0], o_acc, prev[2], o_pair)
        if nxt:
            s_next = jnp.concatenate(pieces, axis=1)
        prev = (h, jnp.concatenate(prow, axis=0), jnp.concatenate(invs, axis=0)) if cur else None


def _attn_kernel(*refs, n_heads):
    n_win = NBUF // TQ
    q_ref = refs[0]
    k_refs = refs[1:1 + n_win]
    v_refs = refs[1 + n_win:1 + 2 * n_win]
    rbp_ref, o_ref, bias_s = refs[1 + 2 * n_win:]
    b = pl.program_id(0)
    t = pl.program_id(1)

    @pl.when(jnp.logical_and(b == 0, t == 0))
    def _():
        _build_bias_tables(rbp_ref, bias_s, n_heads)

    @pl.when(t < n_win - 1)
    def _():
        first_valid = TQ * (n_win - 1 - t)
        _attn_tile(q_ref, k_refs, v_refs, o_ref, bias_s, first_valid, n_heads)

    @pl.when(t >= n_win - 1)
    def _():
        _attn_tile(q_ref, k_refs, v_refs, o_ref, bias_s, None, n_heads)


def _attention(qkv, rbp):
    bsz, seq, d3 = qkv.shape
    d_att = d3 // 3
    n_heads = d_att // HEAD_DIM
    n_win = NBUF // TQ
    assert seq % TQ == 0 and NBUF % TQ == 0 and n_heads % HEADS_PER_VREG == 0
    kern = functools.partial(_attn_kernel, n_heads=n_heads)

    def window(col, j):
        return pl.BlockSpec((None, TQ, d_att),
                            lambda b, t: (b, jnp.maximum(t - (n_win - 1 - j), 0), col))

    return pl.pallas_call(
        kern,
        grid=(bsz, seq // TQ),
        in_specs=([pl.BlockSpec((None, TQ, d_att), lambda b, t: (b, t, 0))]
                  + [window(1, j) for j in range(n_win)]
                  + [window(2, j) for j in range(n_win)]
                  + [_single(rbp.shape, lambda b, t: (0, 0))]),
        out_specs=pl.BlockSpec((None, TQ, d_att), lambda b, t: (b, t, 0)),
        out_shape=jax.ShapeDtypeStruct((bsz, seq, d_att), BF16),
        scratch_shapes=[pltpu.VMEM((n_heads, 2, CHUNK, WIN), F32)],
        compiler_params=pltpu.CompilerParams(dimension_semantics=("arbitrary", "arbitrary"),
                                             vmem_limit_bytes=VMEM_LIMIT_BYTES),
        name="chunk_attn",
    )(*([qkv] * (1 + 2 * n_win)), rbp)


def _outproj_ffn_kernel(x_ref, yr_ref, ya_ref, mod_ref, n2g_ref, fg_ref, wo_ref,
                        wu_ref, fcw_ref, fcb_ref, wd_ref, o_ref, halo_s,
                        *, d_model, d_ff, conv_w):
    i = pl.program_id(0)
    rows = TT * SUBLANES
    halo = (conv_w - 1) * SUBLANES
    n_hp = TT // HT

    @pl.when(i == 0)
    def _():
        halo_s[...] = jnp.zeros_like(halo_s)

    g1 = mod_ref[:, :, 2 * d_model:3 * d_model]
    sh2 = mod_ref[:, :, 3 * d_model:4 * d_model]
    gain2 = n2g_ref[...] * (1.0 + mod_ref[:, :, 4 * d_model:5 * d_model])
    g2 = mod_ref[:, :, 5 * d_model:6 * d_model]

    def out_proj(t):
        ts = slice(t * TT, (t + 1) * TT)
        yr = yr_ref[:, ts, :].reshape(rows, yr_ref.shape[2])
        ya = ya_ref[:, ts, :].reshape(rows, ya_ref.shape[2])
        d_rnn = yr_ref.shape[2]
        y = (jnp.dot(yr, wo_ref[0:d_rnn, :], preferred_element_type=F32)
             + jnp.dot(ya, wo_ref[d_rnn:, :], preferred_element_type=F32))
        return y.reshape(SUBLANES, TT, d_model)

    def head_piece(t, c, y3):
        cs = slice(c * HT, (c + 1) * HT)
        x1 = x_ref[:, t * TT + c * HT:t * TT + (c + 1) * HT, :] + g1 * y3[:, cs, :]
        ms = jnp.mean(x1 * x1, axis=-1, keepdims=True)
        hn = x1 * lax.rsqrt(ms + EPS) * gain2 + sh2
        return x1, jnp.swapaxes(hn, 0, 1).reshape(HT * SUBLANES, d_model).astype(BF16)

    def tail_piece(t, c, z, x1):
        zc = z[c * HT * SUBLANES:(c + 1) * HT * SUBLANES, :].reshape(HT, SUBLANES, d_model)
        x2 = x1 + g2 * jnp.swapaxes(zc, 0, 1)
        ms2 = jnp.mean(x2 * x2, axis=-1, keepdims=True)
        o_ref[:, t * TT + c * HT:t * TT + (c + 1) * HT, :] = x2 * lax.rsqrt(ms2 + EPS) * fg_ref[...]

    def up_piece(hb, cols):
        return jnp.dot(hb, wu_ref[:, cols], preferred_element_type=F32)

    def conv_piece(up, tail, cols):
        ext = jnp.concatenate([tail, up], axis=0)
        u = fcb_ref[:, cols]
        for k in range(conv_w):
            u = u + ext[k * SUBLANES:k * SUBLANES + rows, :] * fcw_ref[k:k + 1, cols]
        return u, ext[rows:rows + halo, :]

    def ffn(hb, tails, side):
        side = list(side)
        z = None
        new_tails = []
        for j, (c0, w) in enumerate(FF_SPLITS):
            cg = slice(c0, c0 + w)
            cv = slice(d_ff + c0, d_ff + c0 + w)
            up_g = up_piece(hb, cg)
            if side:
                side.pop(0)()
            up_v = up_piece(hb, cv)
            if side:
                side.pop(0)()
            ug, tg = conv_piece(up_g, tails[j][0], cg)
            uv, tv = conv_piece(up_v, tails[j][1], cv)
            new_tails.append((tg, tv))
            act = (ug * jax.nn.sigmoid(ug) * uv).astype(BF16)
            zc = jnp.dot(act, wd_ref[c0:c0 + w, :], preferred_element_type=F32)
            z = zc if z is None else z + zc
        assert not side
        return z, new_tails

    tails = [(halo_s[:, c0:c0 + w], halo_s[:, d_ff + c0:d_ff + c0 + w]) for c0, w in FF_SPLITS]
    y3 = out_proj(0)
    hp = [head_piece(0, c, y3) for c in range(n_hp)]
    x1_0 = [p[0] for p in hp]
    hb_0 = jnp.concatenate([p[1] for p in hp], axis=0)

    state = {}

    def side_out_proj():
        state["y3"] = out_proj(1)

    def side_head(c):
        def run():
            state.setdefault("hp", []).append(head_piece(1, c, state["y3"]))
        return run

    z_0, tails = ffn(hb_0, tails, [side_out_proj] + [side_head(c) for c in range(n_hp)])
    x1_1 = [p[0] for p in state["hp"]]
    hb_1 = jnp.concatenate([p[1] for p in state["hp"]], axis=0)

    def side_tail(c):
        return lambda: tail_piece(0, c, z_0, x1_0[c])

    z_1, tails = ffn(hb_1, tails, [side_tail(c) for c in range(n_hp)])
    for c in range(n_hp):
        tail_piece(1, c, z_1, x1_1[c])
    for (c0, w), (tg, tv) in zip(FF_SPLITS, tails):
        halo_s[:, c0:c0 + w] = tg
        halo_s[:, d_ff + c0:d_ff + c0 + w] = tv


def _outproj_ffn(x, yr, ya, mod, n2g, fg, w_out_b, w_up_b, fcw, fcb, w_down_b):
    bsz, seq, d = x.shape
    d_ff = w_down_b.shape[0]
    cw = fcw.shape[0]
    d_rnn = yr.shape[2]
    d_att = ya.shape[2]
    step_t = K3_TILES * TT
    assert sum(w for _, w in FF_SPLITS) == d_ff and seq % step_t == 0 and TT % HT == 0
    const = lambda i: (0, 0)
    kern = functools.partial(_outproj_ffn_kernel, d_model=d, d_ff=d_ff, conv_w=cw)
    return pl.pallas_call(
        kern,
        grid=(seq // step_t,),
        in_specs=[
            pl.BlockSpec((bsz, step_t, d), lambda i: (0, i, 0)),
            pl.BlockSpec((bsz, step_t, d_rnn), lambda i: (0, i, 0)),
            pl.BlockSpec((bsz, step_t, d_att), lambda i: (0, i, 0)),
            _single(mod.shape, lambda i: (0, 0, 0)),
            _single((1, d), const),
            _single((1, d), const),
            _single(w_out_b.shape, const),
            _single(w_up_b.shape, const),
            _single(fcw.shape, const),
            _single((1, 2 * d_ff), const),
            _single(w_down_b.shape, const),
        ],
        out_specs=pl.BlockSpec((bsz, step_t, d), lambda i: (0, i, 0)),
        out_shape=jax.ShapeDtypeStruct((bsz, seq, d), F32),
        scratch_shapes=[pltpu.VMEM(((cw - 1) * SUBLANES, 2 * d_ff), F32)],
        compiler_params=pltpu.CompilerParams(dimension_semantics=("arbitrary",),
                                             vmem_limit_bytes=VMEM_LIMIT_BYTES),
        name="outproj_ffn",
    )(x, yr, ya, mod, n2g.reshape(1, d), fg.reshape(1, d), w_out_b, w_up_b, fcw,
      fcb.reshape(1, 2 * d_ff), w_down_b)


def _gate_weights(wa, wx):
    nb = wa.shape[0]
    hb = nb // 2
    halves = []
    for s in (slice(0, hb), slice(hb, nb)):
        halves.append(jnp.concatenate([block_diag(*wa[s]), block_diag(*wx[s])], axis=1).astype(BF16))
    return halves


def kernel(x, c, ada_w, ada_b, norm1_g, w_in, rnn_conv_w, rnn_conv_b, rg_wa, rg_ba, rg_wx, rg_bx, rg_lambda, rel_bias, w_out, norm2_g, w_up, ffn_conv_w, ffn_conv_b, w_down, final_g):
    bsz, seq, d = x.shape
    assert ada_w.shape[0] == 1
    l = 0
    d_rnn = rnn_conv_w.shape[2]
    d_att = w_out.shape[1] - d_rnn
    mod = _ada(c, ada_w[l], ada_b[l])
    wg0, wg1 = _gate_weights(rg_wa[l], rg_wx[l])
    qkv, yr = _inproj_rglru(x, mod, norm1_g[l], w_in[l].astype(BF16), rnn_conv_w[l],
                            rnn_conv_b[l], wg0, wg1, rg_ba[l], rg_bx[l], rg_lambda[l])
    rbp = jnp.pad(rel_bias[l], ((0, 0), (0, RBPAD - rel_bias.shape[2])))
    ya = _attention(qkv, rbp)
    return _outproj_ffn(x, yr, ya, mod, norm2_g[l], final_g, w_out[l].astype(BF16),
                        w_up[l].astype(BF16), ffn_conv_w[l], ffn_conv_b[l], w_down[l].astype(BF16))
```

```python
import functools

import jax
import jax.numpy as jnp
from jax import lax
from jax.experimental import pallas as pl
from jax.experimental.pallas import tpu as pltpu

F32 = jnp.float32
BF16 = jnp.bfloat16

CHUNK = 64
LOOKBACK = 8
REL_CLIP = 128
HEAD_DIM = 64
RG_C = 8.0
EPS = 1e-6
NEG_INF = -1e30
LOG2E = 1.4426950408889634

SUBLANES = 8
LANES = 128
MXU_DIM = 256
VMEM_LIMIT_BYTES = 56 * 1024 * 1024

TT = 64
RT = 8
PW = 256
T1 = 128
CHUNKS_PER_TILE = 4
TQ = CHUNKS_PER_TILE * CHUNK
HEADS_PER_VREG = LANES // HEAD_DIM
HIST = LOOKBACK * CHUNK
NBUF = HIST + TQ
WIN = (LOOKBACK + 2) * CHUNK
GPAD = 768
RBPAD = 384
FF_SPLITS = ((0, 1024), (1024, 1024), (2048, 768))
K3_TILES = 2
HT = 16


def _single(block_shape, index_map):
    return pl.BlockSpec(block_shape, index_map, pipeline_mode=pl.Buffered(1))


def _ada_kernel(c_ref, w_ref, b_ref, o_ref):
    c = c_ref[...]
    sc = c * jax.nn.sigmoid(c)
    w = w_ref[...]
    w_hi = w.astype(BF16)
    w_lo = (w - w_hi.astype(F32)).astype(BF16)
    sc_hi = sc.astype(BF16)
    sc_lo = (sc - sc_hi.astype(F32)).astype(BF16)
    acc = jnp.dot(sc_hi, w_hi, preferred_element_type=F32)
    acc = acc + jnp.dot(sc_lo, w_hi, preferred_element_type=F32)
    acc = acc + jnp.dot(sc_hi, w_lo, preferred_element_type=F32)
    o_ref[...] = (acc + b_ref[...]).reshape(o_ref.shape)


def _ada(c, ada_w, ada_b):
    bsz, d = c.shape
    n = ada_w.shape[1]
    bn = 2048
    return pl.pallas_call(
        _ada_kernel,
        grid=(n // bn,),
        in_specs=[pl.BlockSpec((bsz, d), lambda j: (0, 0)),
                  pl.BlockSpec((d, bn), lambda j: (0, j)),
                  pl.BlockSpec((1, bn), lambda j: (0, j))],
        out_specs=pl.BlockSpec((bsz, 1, bn), lambda j: (0, 0, j)),
        out_shape=jax.ShapeDtypeStruct((bsz, 1, n), F32),
        name="ada_mod",
    )(c, ada_w, ada_b.reshape(1, n))


def _inproj_rglru_kernel(x_ref, mod_ref, n1g_ref, win_ref, cw_ref, cb_ref, wa_ref, wx_ref,
                         ba_ref, bx_ref, lam_ref, qkv_ref, yr_ref,
                         tail_s, hc, win_s, wg_s, *, d_model, d_rnn, d_att, conv_w):
    i = pl.program_id(0)
    halo = (conv_w - 1) * SUBLANES
    prow = RT * SUBLANES
    half = d_rnn // 2

    @pl.when(i == 0)
    def _():
        tail_s[...] = jnp.zeros_like(tail_s)
        hc[...] = jnp.zeros_like(hc)
        for c0 in range(0, win_ref.shape[1], PW):
            win_s[:, c0:c0 + PW] = win_ref[:, c0:c0 + PW].astype(BF16)
        wg_s[...] = jnp.zeros_like(wg_s)
        nb, bw = wa_ref.shape[0], wa_ref.shape[1]
        per_half = nb // 2
        for n in range(nb):
            hf, k = divmod(n, per_half)
            rs = slice(k * bw, (k + 1) * bw)
            wg_s[hf, rs, k * bw:(k + 1) * bw] = wa_ref[n].astype(BF16)
            wg_s[hf, rs, half + k * bw:half + (k + 1) * bw] = wx_ref[n].astype(BF16)

    gain1 = n1g_ref[...] * (1.0 + mod_ref[:, :, d_model:2 * d_model])
    sh1 = mod_ref[:, :, 0:d_model]
    trow = TT * SUBLANES
    n_rec = TT // RT

    def normed_lhs(t):
        x = x_ref[:, t * TT:(t + 1) * TT, :]
        ms = jnp.mean(x * x, axis=-1, keepdims=True)
        h = x * lax.rsqrt(ms + EPS) * gain1 + sh1
        return h.reshape(trow, d_model).astype(BF16)

    def project_cols(hb, lo, hi):
        part = jnp.dot(hb, win_s[:, lo:hi], preferred_element_type=F32)
        return part.reshape(SUBLANES, TT, hi - lo)

    def branch(hb, j):
        return [project_cols(hb, j * d_rnn + p, j * d_rnn + p + PW) for p in range(0, d_rnn, PW)]

    def emit_qkv(t, hb, p):
        o_q = 2 * d_rnn
        part = project_cols(hb, o_q + p * PW, o_q + (p + 1) * PW)
        if (p + 1) * PW <= d_att:
            part = part * (LOG2E * HEAD_DIM ** -0.5)
        qkv_ref[:, t * TT:(t + 1) * TT, p * PW:(p + 1) * PW] = part.astype(BF16)

    def recur_piece(t, xr_p, gr_p, c, tail, hcur):
        ts = slice(c * RT, (c + 1) * RT)
        xr = jnp.concatenate([jnp.swapaxes(p[:, ts, :], 0, 1).reshape(prow, PW) for p in xr_p], axis=1)
        gr = jnp.concatenate([jnp.swapaxes(p[:, ts, :], 0, 1).reshape(prow, PW) for p in gr_p], axis=1)
        ext = jnp.concatenate([tail, xr], axis=0)
        xc = cb_ref[...]
        for k in range(conv_w):
            xc = xc + ext[k * SUBLANES:k * SUBLANES + prow, :] * cw_ref[k:k + 1, :]
        xcb = xc.astype(BF16)
        g0 = jnp.dot(xcb[:, 0:half], wg_s[0], preferred_element_type=F32)
        g1 = jnp.dot(xcb[:, half:d_rnn], wg_s[1], preferred_element_type=F32)
        r = jax.nn.sigmoid(jnp.concatenate([g0[:, 0:half], g1[:, 0:half]], axis=1) + ba_ref[...])
        ig = jax.nn.sigmoid(jnp.concatenate([g0[:, half:], g1[:, half:]], axis=1) + bx_ref[...])
        lam = lam_ref[...]
        log_sig = jnp.minimum(lam, 0.0) - jnp.log1p(jnp.exp(-jnp.abs(lam)))
        log_a = RG_C * r * log_sig
        a = jnp.exp(log_a)
        w = jnp.tanh(-log_a) * (a * a + 1.0)
        mult = jnp.where(w > 0.0, w * lax.rsqrt(w), 0.0)
        a3 = a.reshape(RT, SUBLANES, d_rnn)
        b3 = (mult * (ig * xc)).reshape(RT, SUBLANES, d_rnn)
        hs = []
        for s in range(RT):
            hcur = a3[s] * hcur + b3[s]
            hs.append(hcur)
        yr = jnp.stack(hs, axis=0) * jax.nn.gelu(gr).reshape(RT, SUBLANES, d_rnn)
        yr_ref[:, t * TT + c * RT:t * TT + (c + 1) * RT, :] = jnp.swapaxes(yr, 0, 1).astype(BF16)
        return ext[prow:prow + halo, :], hcur

    n_tiles = T1 // TT
    n_qkv = (3 * d_att) // PW
    tail, hcur = tail_s[...], hc[...]
    hb = normed_lhs(0)
    xr_p, gr_p = branch(hb, 0), branch(hb, 1)
    for t in range(n_tiles):
        nxt = t + 1 < n_tiles
        ahead = {}
        for c in range(max(n_rec, n_qkv)):
            if c < n_qkv:
                emit_qkv(t, hb, c)
            if c < n_rec:
                tail, hcur = recur_piece(t, xr_p, gr_p, c, tail, hcur)
            if nxt and c == n_rec - 3:
                ahead["hb"] = normed_lhs(t + 1)
            if nxt and c == n_rec - 2:
                ahead["xr"] = branch(ahead["hb"], 0)
            if nxt and c == n_rec - 1:
                ahead["gr"] = branch(ahead["hb"], 1)
        if nxt:
            hb, xr_p, gr_p = ahead["hb"], ahead["xr"], ahead["gr"]
    tail_s[...] = tail
    hc[...] = hcur


def _inproj_rglru(x, mod, n1g, w_in, conv_w, conv_b, wa, wx, ba, bx, lam):
    bsz, seq, d = x.shape
    assert bsz == SUBLANES and seq % T1 == 0
    d_in = w_in.shape[1]
    d_rnn = conv_w.shape[1]
    d_att = (d_in - 2 * d_rnn) // 3
    assert d_att == d_rnn and T1 % TT == 0 and TT % RT == 0
    cw = conv_w.shape[0]
    const = lambda i: (0, 0)
    kern = functools.partial(_inproj_rglru_kernel, d_model=d, d_rnn=d_rnn, d_att=d_att, conv_w=cw)
    return pl.pallas_call(
        kern,
        grid=(seq // T1,),
        in_specs=[
            pl.BlockSpec((bsz, T1, d), lambda i: (0, i, 0)),
            _single(mod.shape, lambda i: (0, 0, 0)),
            _single((1, d), const),
            _single(w_in.shape, const),
            _single(conv_w.shape, const),
            _single((1, d_rnn), const),
            _single(wa.shape, lambda i: (0, 0, 0)),
            _single(wx.shape, lambda i: (0, 0, 0)),
            _single((1, d_rnn), const),
            _single((1, d_rnn), const),
            _single((1, d_rnn), const),
        ],
        out_specs=[pl.BlockSpec((bsz, T1, 3 * d_att), lambda i: (0, i, 0)),
                   pl.BlockSpec((bsz, T1, d_rnn), lambda i: (0, i, 0))],
        out_shape=[jax.ShapeDtypeStruct((bsz, seq, 3 * d_att), BF16),
                   jax.ShapeDtypeStruct((bsz, seq, d_rnn), BF16)],
        scratch_shapes=[
            pltpu.VMEM(((cw - 1) * SUBLANES, d_rnn), F32),
            pltpu.VMEM((SUBLANES, d_rnn), F32),
            pltpu.VMEM(w_in.shape, BF16),
            pltpu.VMEM((2, d_rnn // 2, d_rnn), BF16),
        ],
        compiler_params=pltpu.CompilerParams(dimension_semantics=("arbitrary",),
                                             vmem_limit_bytes=VMEM_LIMIT_BYTES),
        name="inproj_rglru",
    )(x, mod, n1g.reshape(1, d), w_in, conv_w, conv_b.reshape(1, d_rnn), wa, wx,
      ba.reshape(1, d_rnn), bx.reshape(1, d_rnn), lam.reshape(1, d_rnn))


def _build_bias_tables(rbp_ref, bias_s, n_heads):
    r_io = lax.broadcasted_iota(jnp.int32, (RBPAD, GPAD), 0)
    j_io = lax.broadcasted_iota(jnp.int32, (RBPAD, GPAD), 1)
    idx = jnp.clip(HIST + CHUNK - 1 - j_io, -REL_CLIP, REL_CLIP) + REL_CLIP
    sel = (r_io == idx).astype(F32)
    g = jnp.dot(rbp_ref[...], sel, precision=lax.Precision.HIGHEST, preferred_element_type=F32)
    g = (g - g[:, 0:1]) * LOG2E
    col = lax.broadcasted_iota(jnp.int32, (CHUNK, WIN), 1)
    band = (LOOKBACK + 1) * CHUNK
    for h in range(n_heads):
        gh = jnp.broadcast_to(g[h:h + 1, :], (CHUNK, GPAD))
        even = pltpu.roll(gh, GPAD - (CHUNK - 1), 1, stride=1, stride_axis=0)[:, 0:WIN]
        odd = pltpu.roll(gh, 1, 1, stride=1, stride_axis=0)[:, 0:WIN]
        bias_s[h, 0] = jnp.where(col < band, even, NEG_INF)
        bias_s[h, 1] = jnp.where(col >= CHUNK, odd, NEG_INF)


ZERO_BIAS_COLS = ((0, HIST - REL_CLIP), (LANES, HIST - REL_CLIP))


def _attn_tile(q_ref, k_refs, v_refs, o_ref, bias_s, first_valid, n_heads):
    lane = lax.broadcasted_iota(jnp.int32, (1, LANES), 1)
    pad = jnp.zeros((CHUNK, NBUF - WIN), BF16)
    n_kt = len(k_refs)

    def masked_q(h):
        pair, hh = divmod(h, HEADS_PER_VREG)
        qp = q_ref[:, pair * LANES:(pair + 1) * LANES]
        in_head = jnp.logical_and(lane >= hh * HEAD_DIM, lane < (hh + 1) * HEAD_DIM)
        return jnp.where(in_head, qp, jnp.zeros_like(qp))

    def score_piece(h, qm, j):
        pair = h // HEADS_PER_VREG
        return lax.dot_general(qm, k_refs[j][:, pair * LANES:(pair + 1) * LANES],
                               (((1,), (1,)), ((), ())), preferred_element_type=F32)

    def softmax_piece(h, s_all, c):
        w0 = (c // 2) * LANES
        raw = s_all[c * CHUNK:(c + 1) * CHUNK, w0:w0 + WIN]
        z0, z1 = ZERO_BIAS_COLS[c % 2]
        parts = [raw[:, z0:z1], raw[:, z1:WIN] + bias_s[h, c % 2, :, z1:WIN]]
        if z0:
            parts.insert(0, raw[:, 0:z0] + bias_s[h, c % 2, :, 0:z0])
        s = jnp.concatenate(parts, axis=1)
        if first_valid is not None:
            col = lax.broadcasted_iota(jnp.int32, (1, WIN), 1) + w0
            s = jnp.where(col >= first_valid, s, NEG_INF)
        m = jnp.max(s, axis=-1, keepdims=True)
        p = jnp.exp2(s - m)
        inv = 1.0 / jnp.sum(p, axis=-1, keepdims=True)
        pb = p.astype(BF16)
        return inv, jnp.concatenate([pb, pad] if w0 == 0 else [pad, pb], axis=1)

    def value_piece(h, pmat, j):
        pair = h // HEADS_PER_VREG
        return jnp.dot(pmat[:, j * TQ:(j + 1) * TQ], v_refs[j][:, pair * LANES:(pair + 1) * LANES],
                       preferred_element_type=F32)

    def finish(h, o, inv, o_prev):
        pair, hh = divmod(h, HEADS_PER_VREG)
        o = o * inv
        if hh == 0:
            return o
        in_head = jnp.logical_and(lane >= hh * HEAD_DIM, lane < (hh + 1) * HEAD_DIM)
        o_ref[:, pair * LANES:(pair + 1) * LANES] = jnp.where(in_head, o, o_prev).astype(BF16)
        return None

    assert n_kt <= CHUNKS_PER_TILE and HEADS_PER_VREG == 2
    qm = masked_q(0)
    s_next = jnp.concatenate([score_piece(0, qm, j) for j in range(n_kt)], axis=1)
    o_pair = None
    prev = None
    for h in range(n_heads + 1):
        cur = h < n_heads
        nxt = h + 1 < n_heads
        s_cur = s_next
        if nxt:
            qm = masked_q(h + 1)
        pieces, invs, prow, o_acc = [], [], [], None
        for c in range(CHUNKS_PER_TILE):
            if nxt and c < n_kt:
                pieces.append(score_piece(h + 1, qm, c))
            if prev is not None and c < n_kt:
                part = value_piece(prev[0], prev[1], c)
                o_acc = part if o_acc is None else o_acc + part
            if cur:
                inv, pr = softmax_piece(h, s_cur, c)
                invs.append(inv)
                prow.append(pr)
        if prev is not None:
            o_pair = finish(prev[0], o_acc, prev[2], o_pair)
        if nxt:
            s_next = jnp.concatenate(pieces, axis=1)
        prev = (h, jnp.concatenate(prow, axis=0), jnp.concatenate(invs, axis=0)) if cur else None


def _attn_kernel(*refs, n_heads):
    n_win = NBUF // TQ
    q_ref = refs[0]
    k_refs = refs[1:1 + n_win]
    v_refs = refs[1 + n_win:1 + 2 * n_win]
    rbp_ref, o_ref, bias_s = refs[1 + 2 * n_win:]
    b = pl.program_id(0)
    t = pl.program_id(1)

    @pl.when(jnp.logical_and(b == 0, t == 0))
    def _():
        _build_bias_tables(rbp_ref, bias_s, n_heads)

    @pl.when(t < n_win - 1)
    def _():
        first_valid = TQ * (n_win - 1 - t)
        _attn_tile(q_ref, k_refs, v_refs, o_ref, bias_s, first_valid, n_heads)

    @pl.when(t >= n_win - 1)
    def _():
        _attn_tile(q_ref, k_refs, v_refs, o_ref, bias_s, None, n_heads)


def _attention(qkv, rbp):
    bsz, seq, d3 = qkv.shape
    d_att = d3 // 3
    n_heads = d_att // HEAD_DIM
    n_win = NBUF // TQ
    assert seq % TQ == 0 and NBUF % TQ == 0 and n_heads % HEADS_PER_VREG == 0
    kern = functools.partial(_attn_kernel, n_heads=n_heads)

    def window(col, j):
        return pl.BlockSpec((None, TQ, d_att),
                            lambda b, t: (b, jnp.maximum(t - (n_win - 1 - j), 0), col))

    return pl.pallas_call(
        kern,
        grid=(bsz, seq // TQ),
        in_specs=([pl.BlockSpec((None, TQ, d_att), lambda b, t: (b, t, 0))]
                  + [window(1, j) for j in range(n_win)]
                  + [window(2, j) for j in range(n_win)]
                  + [_single(rbp.shape, lambda b, t: (0, 0))]),
        out_specs=pl.BlockSpec((None, TQ, d_att), lambda b, t: (b, t, 0)),
        out_shape=jax.ShapeDtypeStruct((bsz, seq, d_att), BF16),
        scratch_shapes=[pltpu.VMEM((n_heads, 2, CHUNK, WIN), F32)],
        compiler_params=pltpu.CompilerParams(dimension_semantics=("arbitrary", "arbitrary"),
                                             vmem_limit_bytes=VMEM_LIMIT_BYTES),
        name="chunk_attn",
    )(*([qkv] * (1 + 2 * n_win)), rbp)


def _outproj_ffn_kernel(x_ref, yr_ref, ya_ref, mod_ref, n2g_ref, fg_ref, wo_ref,
                        wu_ref, fcw_ref, fcb_ref, wd_ref, o_ref, halo_s,
                        *, d_model, d_ff, conv_w):
    i = pl.program_id(0)
    rows = TT * SUBLANES
    halo = (conv_w - 1) * SUBLANES
    n_hp = TT // HT

    @pl.when(i == 0)
    def _():
        halo_s[...] = jnp.zeros_like(halo_s)

    g1 = mod_ref[:, :, 2 * d_model:3 * d_model]
    sh2 = mod_ref[:, :, 3 * d_model:4 * d_model]
    gain2 = n2g_ref[...] * (1.0 + mod_ref[:, :, 4 * d_model:5 * d_model])
    g2 = mod_ref[:, :, 5 * d_model:6 * d_model]

    def out_proj(t):
        ts = slice(t * TT, (t + 1) * TT)
        yr = yr_ref[:, ts, :].reshape(rows, yr_ref.shape[2])
        ya = ya_ref[:, ts, :].reshape(rows, ya_ref.shape[2])
        d_rnn = yr_ref.shape[2]
        y = (jnp.dot(yr, wo_ref[0:d_rnn, :], preferred_element_type=F32)
             + jnp.dot(ya, wo_ref[d_rnn:, :], preferred_element_type=F32))
        return y.reshape(SUBLANES, TT, d_model)

    def head_piece(t, c, y3):
        cs = slice(c * HT, (c + 1) * HT)
        x1 = x_ref[:, t * TT + c * HT:t * TT + (c + 1) * HT, :] + g1 * y3[:, cs, :]
        ms = jnp.mean(x1 * x1, axis=-1, keepdims=True)
        hn = x1 * lax.rsqrt(ms + EPS) * gain2 + sh2
        return x1, jnp.swapaxes(hn, 0, 1).reshape(HT * SUBLANES, d_model).astype(BF16)

    def tail_piece(t, c, z, x1):
        zc = z[c * HT * SUBLANES:(c + 1) * HT * SUBLANES, :].reshape(HT, SUBLANES, d_model)
        x2 = x1 + g2 * jnp.swapaxes(zc, 0, 1)
        ms2 = jnp.mean(x2 * x2, axis=-1, keepdims=True)
        o_ref[:, t * TT + c * HT:t * TT + (c + 1) * HT, :] = x2 * lax.rsqrt(ms2 + EPS) * fg_ref[...]

    def up_piece(hb, cols):
        return jnp.dot(hb, wu_ref[:, cols], preferred_element_type=F32)

    def conv_piece(up, tail, cols):
        ext = jnp.concatenate([tail, up], axis=0)
        u = fcb_ref[:, cols]
        for k in range(conv_w):
            u = u + ext[k * SUBLANES:k * SUBLANES + rows, :] * fcw_ref[k:k + 1, cols]
        return u, ext[rows:rows + halo, :]

    def ffn(hb, tails, side):
        side = list(side)
        z = None
        new_tails = []
        for j, (c0, w) in enumerate(FF_SPLITS):
            cg = slice(c0, c0 + w)
            cv = slice(d_ff + c0, d_ff + c0 + w)
            up_g = up_piece(hb, cg)
            if side:
                side.pop(0)()
            up_v = up_piece(hb, cv)
            if side:
                side.pop(0)()
            ug, tg = conv_piece(up_g, tails[j][0], cg)
            uv, tv = conv_piece(up_v, tails[j][1], cv)
            new_tails.append((tg, tv))
            act = (ug * jax.nn.sigmoid(ug) * uv).astype(BF16)
            zc = jnp.dot(act, wd_ref[c0:c0 + w, :], preferred_element_type=F32)
            z = zc if z is None else z + zc
        assert not side
        return z, new_tails

    tails = [(halo_s[:, c0:c0 + w], halo_s[:, d_ff + c0:d_ff + c0 + w]) for c0, w in FF_SPLITS]
    y3 = out_proj(0)
    hp = [head_piece(0, c, y3) for c in range(n_hp)]
    x1_0 = [p[0] for p in hp]
    hb_0 = jnp.concatenate([p[1] for p in hp], axis=0)

    state = {}

    def side_out_proj():
        state["y3"] = out_proj(1)

    def side_head(c):
        def run():
            state.setdefault("hp", []).append(head_piece(1, c, state["y3"]))
        return run

    z_0, tails = ffn(hb_0, tails, [side_out_proj] + [side_head(c) for c in range(n_hp)])
    x1_1 = [p[0] for p in state["hp"]]
    hb_1 = jnp.concatenate([p[1] for p in state["hp"]], axis=0)

    def side_tail(c):
        return lambda: tail_piece(0, c, z_0, x1_0[c])

    z_1, tails = ffn(hb_1, tails, [side_tail(c) for c in range(n_hp)])
    for c in range(n_hp):
        tail_piece(1, c, z_1, x1_1[c])
    for (c0, w), (tg, tv) in zip(FF_SPLITS, tails):
        halo_s[:, c0:c0 + w] = tg
        halo_s[:, d_ff + c0:d_ff + c0 + w] = tv


def _outproj_ffn(x, yr, ya, mod, n2g, fg, w_out_b, w_up_b, fcw, fcb, w_down_b):
    bsz, seq, d = x.shape
    d_ff = w_down_b.shape[0]
    cw = fcw.shape[0]
    d_rnn = yr.shape[2]
    d_att = ya.shape[2]
    step_t = K3_TILES * TT
    assert sum(w for _, w in FF_SPLITS) == d_ff and seq % step_t == 0 and TT % HT == 0
    const = lambda i: (0, 0)
    kern = functools.partial(_outproj_ffn_kernel, d_model=d, d_ff=d_ff, conv_w=cw)
    return pl.pallas_call(
        kern,
        grid=(seq // step_t,),
        in_specs=[
            pl.BlockSpec((bsz, step_t, d), lambda i: (0, i, 0)),
            pl.BlockSpec((bsz, step_t, d_rnn), lambda i: (0, i, 0)),
            pl.BlockSpec((bsz, step_t, d_att), lambda i: (0, i, 0)),
            _single(mod.shape, lambda i: (0, 0, 0)),
            _single((1, d), const),
            _single((1, d), const),
            _single(w_out_b.shape, const),
            _single(w_up_b.shape, const),
            _single(fcw.shape, const),
            _single((1, 2 * d_ff), const),
            _single(w_down_b.shape, const),
        ],
        out_specs=pl.BlockSpec((bsz, step_t, d), lambda i: (0, i, 0)),
        out_shape=jax.ShapeDtypeStruct((bsz, seq, d), F32),
        scratch_shapes=[pltpu.VMEM(((cw - 1) * SUBLANES, 2 * d_ff), F32)],
        compiler_params=pltpu.CompilerParams(dimension_semantics=("arbitrary",),
                                             vmem_limit_bytes=VMEM_LIMIT_BYTES),
        name="outproj_ffn",
    )(x, yr, ya, mod, n2g.reshape(1, d), fg.reshape(1, d), w_out_b, w_up_b, fcw,
      fcb.reshape(1, 2 * d_ff), w_down_b)


def kernel(x, c, ada_w, ada_b, norm1_g, w_in, rnn_conv_w, rnn_conv_b, rg_wa, rg_ba, rg_wx, rg_bx, rg_lambda, rel_bias, w_out, norm2_g, w_up, ffn_conv_w, ffn_conv_b, w_down, final_g):
    bsz, seq, d = x.shape
    assert ada_w.shape[0] == 1
    l = 0
    d_rnn = rnn_conv_w.shape[2]
    d_att = w_out.shape[1] - d_rnn
    mod = _ada(c, ada_w[l], ada_b[l])
    qkv, yr = _inproj_rglru(x, mod, norm1_g[l], w_in[l], rnn_conv_w[l], rnn_conv_b[l],
                            rg_wa[l], rg_wx[l], rg_ba[l], rg_bx[l], rg_lambda[l])
    rbp = jnp.pad(rel_bias[l], ((0, 0), (0, RBPAD - rel_bias.shape[2])))
    ya = _attention(qkv, rbp)
    return _outproj_ffn(x, yr, ya, mod, norm2_g[l], final_g, w_out[l].astype(BF16),
                        w_up[l].astype(BF16), ffn_conv_w[l], ffn_conv_b[l], w_down[l].astype(BF16))
```

```python
import functools

import jax
import jax.numpy as jnp
from jax import lax
from jax.experimental import pallas as pl
from jax.experimental.pallas import tpu as pltpu

F32 = jnp.float32
BF16 = jnp.bfloat16

CHUNK = 64
LOOKBACK = 8
REL_CLIP = 128
HEAD_DIM = 64
RG_C = 8.0
EPS = 1e-6
NEG_INF = -1e30
LOG2E = 1.4426950408889634

SUBLANES = 8
LANES = 128
MXU_DIM = 256
VMEM_LIMIT_BYTES = 56 * 1024 * 1024

TT = 64
RT = 8
PW = 256
T1 = 128
CHUNKS_PER_TILE = 4
TQ = CHUNKS_PER_TILE * CHUNK
HEADS_PER_VREG = LANES // HEAD_DIM
HIST = LOOKBACK * CHUNK
NBUF = HIST + TQ
WIN = (LOOKBACK + 2) * CHUNK
GPAD = 768
RBPAD = 384
FF_SPLITS = ((0, 1024), (1024, 1024), (2048, 768))
K3_TILES = 2
HT = 16


def _single(block_shape, index_map):
    return pl.BlockSpec(block_shape, index_map, pipeline_mode=pl.Buffered(1))


def _ada_kernel(c_ref, w_ref, b_ref, o_ref):
    c = c_ref[...]
    sc = c * jax.nn.sigmoid(c)
    w = w_ref[...]
    w_hi = w.astype(BF16)
    w_lo = (w - w_hi.astype(F32)).astype(BF16)
    sc_hi = sc.astype(BF16)
    sc_lo = (sc - sc_hi.astype(F32)).astype(BF16)
    acc = jnp.dot(sc_hi, w_hi, preferred_element_type=F32)
    acc = acc + jnp.dot(sc_lo, w_hi, preferred_element_type=F32)
    acc = acc + jnp.dot(sc_hi, w_lo, preferred_element_type=F32)
    o_ref[...] = (acc + b_ref[...]).reshape(o_ref.shape)


def _ada(c, ada_w, ada_b):
    bsz, d = c.shape
    n = ada_w.shape[1]
    bn = 2048
    return pl.pallas_call(
        _ada_kernel,
        grid=(n // bn,),
        in_specs=[pl.BlockSpec((bsz, d), lambda j: (0, 0)),
                  pl.BlockSpec((d, bn), lambda j: (0, j)),
                  pl.BlockSpec((1, bn), lambda j: (0, j))],
        out_specs=pl.BlockSpec((bsz, 1, bn), lambda j: (0, 0, j)),
        out_shape=jax.ShapeDtypeStruct((bsz, 1, n), F32),
        name="ada_mod",
    )(c, ada_w, ada_b.reshape(1, n))


def _inproj_rglru_kernel(x_ref, mod_ref, n1g_ref, win_ref, cw_ref, cb_ref, wa_ref, wx_ref,
                         ba_ref, bx_ref, lam_ref, *rest, d_model, d_rnn, d_att, conv_w, n_cast):
    cast_in = rest[:n_cast]
    qkv_ref, yr_ref = rest[n_cast:n_cast + 2]
    cast_out = rest[n_cast + 2:2 * n_cast + 2]
    tail_s, hc, win_s, wg_s = rest[2 * n_cast + 2:]
    i = pl.program_id(0)

    for src, dst in zip(cast_in, cast_out):
        dst[...] = src[...].astype(BF16)
    halo = (conv_w - 1) * SUBLANES
    prow = RT * SUBLANES
    half = d_rnn // 2

    @pl.when(i == 0)
    def _():
        tail_s[...] = jnp.zeros_like(tail_s)
        hc[...] = jnp.zeros_like(hc)
        for c0 in range(0, win_ref.shape[1], PW):
            win_s[:, c0:c0 + PW] = win_ref[:, c0:c0 + PW].astype(BF16)
        wg_s[...] = jnp.zeros_like(wg_s)
        nb, bw = wa_ref.shape[0], wa_ref.shape[1]
        per_half = nb // 2
        for n in range(nb):
            hf, k = divmod(n, per_half)
            rs = slice(k * bw, (k + 1) * bw)
            wg_s[hf, rs, k * bw:(k + 1) * bw] = wa_ref[n].astype(BF16)
            wg_s[hf, rs, half + k * bw:half + (k + 1) * bw] = wx_ref[n].astype(BF16)

    gain1 = n1g_ref[...] * (1.0 + mod_ref[:, :, d_model:2 * d_model])
    sh1 = mod_ref[:, :, 0:d_model]
    trow = TT * SUBLANES
    n_rec = TT // RT

    def normed_lhs(t):
        x = x_ref[:, t * TT:(t + 1) * TT, :]
        ms = jnp.mean(x * x, axis=-1, keepdims=True)
        h = x * lax.rsqrt(ms + EPS) * gain1 + sh1
        return h.reshape(trow, d_model).astype(BF16)

    def project_cols(hb, lo, hi):
        part = jnp.dot(hb, win_s[:, lo:hi], preferred_element_type=F32)
        return part.reshape(SUBLANES, TT, hi - lo)

    def branch(hb, j):
        return [project_cols(hb, j * d_rnn + p, j * d_rnn + p + PW) for p in range(0, d_rnn, PW)]

    def emit_qkv(t, hb, p):
        o_q = 2 * d_rnn
        part = project_cols(hb, o_q + p * PW, o_q + (p + 1) * PW)
        if (p + 1) * PW <= d_att:
            part = part * (LOG2E * HEAD_DIM ** -0.5)
        qkv_ref[:, t * TT:(t + 1) * TT, p * PW:(p + 1) * PW] = part.astype(BF16)

    def recur_piece(t, xr_p, gr_p, c, tail, hcur):
        ts = slice(c * RT, (c + 1) * RT)
        xr = jnp.concatenate([jnp.swapaxes(p[:, ts, :], 0, 1).reshape(prow, PW) for p in xr_p], axis=1)
        gr = jnp.concatenate([jnp.swapaxes(p[:, ts, :], 0, 1).reshape(prow, PW) for p in gr_p], axis=1)
        ext = jnp.concatenate([tail, xr], axis=0)
        xc = cb_ref[...]
        for k in range(conv_w):
            xc = xc + ext[k * SUBLANES:k * SUBLANES + prow, :] * cw_ref[k:k + 1, :]
        xcb = xc.astype(BF16)
        g0 = jnp.dot(xcb[:, 0:half], wg_s[0], preferred_element_type=F32)
        g1 = jnp.dot(xcb[:, half:d_rnn], wg_s[1], preferred_element_type=F32)
        r = jax.nn.sigmoid(jnp.concatenate([g0[:, 0:half], g1[:, 0:half]], axis=1) + ba_ref[...])
        ig = jax.nn.sigmoid(jnp.concatenate([g0[:, half:], g1[:, half:]], axis=1) + bx_ref[...])
        lam = lam_ref[...]
        log_sig = jnp.minimum(lam, 0.0) - jnp.log1p(jnp.exp(-jnp.abs(lam)))
        log_a = RG_C * r * log_sig
        a = jnp.exp(log_a)
        w = jnp.tanh(-log_a) * (a * a + 1.0)
        mult = jnp.where(w > 0.0, w * lax.rsqrt(w), 0.0)
        a3 = a.reshape(RT, SUBLANES, d_rnn)
        b3 = (mult * (ig * xc)).reshape(RT, SUBLANES, d_rnn)
        hs = []
        for s in range(RT):
            hcur = a3[s] * hcur + b3[s]
            hs.append(hcur)
        yr = jnp.stack(hs, axis=0) * jax.nn.gelu(gr).reshape(RT, SUBLANES, d_rnn)
        yr_ref[:, t * TT + c * RT:t * TT + (c + 1) * RT, :] = jnp.swapaxes(yr, 0, 1).astype(BF16)
        return ext[prow:prow + halo, :], hcur

    n_tiles = T1 // TT
    n_qkv = (3 * d_att) // PW
    tail, hcur = tail_s[...], hc[...]
    hb = normed_lhs(0)
    xr_p, gr_p = branch(hb, 0), branch(hb, 1)
    for t in range(n_tiles):
        nxt = t + 1 < n_tiles
        ahead = {}
        for c in range(max(n_rec, n_qkv)):
            if c < n_qkv:
                emit_qkv(t, hb, c)
            if c < n_rec:
                tail, hcur = recur_piece(t, xr_p, gr_p, c, tail, hcur)
            if nxt and c == n_rec - 3:
                ahead["hb"] = normed_lhs(t + 1)
            if nxt and c == n_rec - 2:
                ahead["xr"] = branch(ahead["hb"], 0)
            if nxt and c == n_rec - 1:
                ahead["gr"] = branch(ahead["hb"], 1)
        if nxt:
            hb, xr_p, gr_p = ahead["hb"], ahead["xr"], ahead["gr"]
    tail_s[...] = tail
    hc[...] = hcur


def _inproj_rglru(x, mod, n1g, w_in, conv_w, conv_b, wa, wx, ba, bx, lam, later_weights):
    bsz, seq, d = x.shape
    assert bsz == SUBLANES and seq % T1 == 0
    d_in = w_in.shape[1]
    d_rnn = conv_w.shape[1]
    d_att = (d_in - 2 * d_rnn) // 3
    assert d_att == d_rnn and T1 % TT == 0 and TT % RT == 0
    cw = conv_w.shape[0]
    const = lambda i: (0, 0)
    n_steps = seq // T1
    assert all(w.shape[0] % (2 * SUBLANES * n_steps) == 0 for w in later_weights)
    cast_specs = [pl.BlockSpec((w.shape[0] // n_steps, w.shape[1]), lambda i: (i, 0))
                  for w in later_weights]
    kern = functools.partial(_inproj_rglru_kernel, d_model=d, d_rnn=d_rnn, d_att=d_att, conv_w=cw,
                             n_cast=len(later_weights))
    outs = pl.pallas_call(
        kern,
        grid=(n_steps,),
        in_specs=[
            pl.BlockSpec((bsz, T1, d), lambda i: (0, i, 0)),
            _single(mod.shape, lambda i: (0, 0, 0)),
            _single((1, d), const),
            _single(w_in.shape, const),
            _single(conv_w.shape, const),
            _single((1, d_rnn), const),
            _single(wa.shape, lambda i: (0, 0, 0)),
            _single(wx.shape, lambda i: (0, 0, 0)),
            _single((1, d_rnn), const),
            _single((1, d_rnn), const),
            _single((1, d_rnn), const),
        ] + cast_specs,
        out_specs=[pl.BlockSpec((bsz, T1, 3 * d_att), lambda i: (0, i, 0)),
                   pl.BlockSpec((bsz, T1, d_rnn), lambda i: (0, i, 0))] + cast_specs,
        out_shape=[jax.ShapeDtypeStruct((bsz, seq, 3 * d_att), BF16),
                   jax.ShapeDtypeStruct((bsz, seq, d_rnn), BF16)]
        + [jax.ShapeDtypeStruct(w.shape, BF16) for w in later_weights],
        scratch_shapes=[
            pltpu.VMEM(((cw - 1) * SUBLANES, d_rnn), F32),
            pltpu.VMEM((SUBLANES, d_rnn), F32),
            pltpu.VMEM(w_in.shape, BF16),
            pltpu.VMEM((2, d_rnn // 2, d_rnn), BF16),
        ],
        compiler_params=pltpu.CompilerParams(dimension_semantics=("arbitrary",),
                                             vmem_limit_bytes=VMEM_LIMIT_BYTES),
        name="inproj_rglru",
    )(x, mod, n1g.reshape(1, d), w_in, conv_w, conv_b.reshape(1, d_rnn), wa, wx,
      ba.reshape(1, d_rnn), bx.reshape(1, d_rnn), lam.reshape(1, d_rnn), *later_weights)
    return outs[0], outs[1], outs[2:]


def _build_bias_tables(rbp_ref, bias_s, n_heads):
    r_io = lax.broadcasted_iota(jnp.int32, (RBPAD, GPAD), 0)
    j_io = lax.broadcasted_iota(jnp.int32, (RBPAD, GPAD), 1)
    idx = jnp.clip(HIST + CHUNK - 1 - j_io, -REL_CLIP, REL_CLIP) + REL_CLIP
    sel = (r_io == idx).astype(F32)
    g = jnp.dot(rbp_ref[...], sel, precision=lax.Precision.HIGHEST, preferred_element_type=F32)
    g = (g - g[:, 0:1]) * LOG2E
    col = lax.broadcasted_iota(jnp.int32, (CHUNK, WIN), 1)
    band = (LOOKBACK + 1) * CHUNK
    for h in range(n_heads):
        gh = jnp.broadcast_to(g[h:h + 1, :], (CHUNK, GPAD))
        even = pltpu.roll(gh, GPAD - (CHUNK - 1), 1, stride=1, stride_axis=0)[:, 0:WIN]
        odd = pltpu.roll(gh, 1, 1, stride=1, stride_axis=0)[:, 0:WIN]
        bias_s[h, 0] = jnp.where(col < band, even, NEG_INF)
        bias_s[h, 1] = jnp.where(col >= CHUNK, odd, NEG_INF)


ZERO_BIAS_COLS = ((0, HIST - REL_CLIP), (LANES, HIST - REL_CLIP))


def _attn_tile(q_ref, k_refs, v_refs, o_ref, bias_s, first_valid, n_heads):
    lane = lax.broadcasted_iota(jnp.int32, (1, LANES), 1)
    pad = jnp.zeros((CHUNK, NBUF - WIN), BF16)
    n_kt = len(k_refs)

    def masked_q(h):
        pair, hh = divmod(h, HEADS_PER_VREG)
        qp = q_ref[:, pair * LANES:(pair + 1) * LANES]
        in_head = jnp.logical_and(lane >= hh * HEAD_DIM, lane < (hh + 1) * HEAD_DIM)
        return jnp.where(in_head, qp, jnp.zeros_like(qp))

    def score_piece(h, qm, j):
        pair = h // HEADS_PER_VREG
        return lax.dot_general(qm, k_refs[j][:, pair * LANES:(pair + 1) * LANES],
                               (((1,), (1,)), ((), ())), preferred_element_type=F32)

    def softmax_piece(h, s_all, c):
        w0 = (c // 2) * LANES
        raw = s_all[c * CHUNK:(c + 1) * CHUNK, w0:w0 + WIN]
        z0, z1 = ZERO_BIAS_COLS[c % 2]
        parts = [raw[:, z0:z1], raw[:, z1:WIN] + bias_s[h, c % 2, :, z1:WIN]]
        if z0:
            parts.insert(0, raw[:, 0:z0] + bias_s[h, c % 2, :, 0:z0])
        s = jnp.concatenate(parts, axis=1)
        if first_valid is not None:
            col = lax.broadcasted_iota(jnp.int32, (1, WIN), 1) + w0
            s = jnp.where(col >= first_valid, s, NEG_INF)
        m = jnp.max(s, axis=-1, keepdims=True)
        p = jnp.exp2(s - m)
        inv = 1.0 / jnp.sum(p, axis=-1, keepdims=True)
        pb = p.astype(BF16)
        return inv, jnp.concatenate([pb, pad] if w0 == 0 else [pad, pb], axis=1)

    def value_piece(h, pmat, j):
        pair = h // HEADS_PER_VREG
        return jnp.dot(pmat[:, j * TQ:(j + 1) * TQ], v_refs[j][:, pair * LANES:(pair + 1) * LANES],
                       preferred_element_type=F32)

    def finish(h, o, inv, o_prev):
        pair, hh = divmod(h, HEADS_PER_VREG)
        o = o * inv
        if hh == 0:
            return o
        in_head = jnp.logical_and(lane >= hh * HEAD_DIM, lane < (hh + 1) * HEAD_DIM)
        o_ref[:, pair * LANES:(pair + 1) * LANES] = jnp.where(in_head, o, o_prev).astype(BF16)
        return None

    assert n_kt <= CHUNKS_PER_TILE and HEADS_PER_VREG == 2
    qm = masked_q(0)
    s_next = jnp.concatenate([score_piece(0, qm, j) for j in range(n_kt)], axis=1)
    o_pair = None
    prev = None
    for h in range(n_heads + 1):
        cur = h < n_heads
        nxt = h + 1 < n_heads
        s_cur = s_next
        if nxt:
            qm = masked_q(h + 1)
        pieces, invs, prow, o_acc = [], [], [], None
        for c in range(CHUNKS_PER_TILE):
            if nxt and c < n_kt:
                pieces.append(score_piece(h + 1, qm, c))
            if prev is not None and c < n_kt:
                part = value_piece(prev[0], prev[1], c)
                o_acc = part if o_acc is None else o_acc + part
            if cur:
                inv, pr = softmax_piece(h, s_cur, c)
                invs.append(inv)
                prow.append(pr)
        if prev is not None:
            o_pair = finish(prev[0], o_acc, prev[2], o_pair)
        if nxt:
            s_next = jnp.concatenate(pieces, axis=1)
        prev = (h, jnp.concatenate(prow, axis=0), jnp.concatenate(invs, axis=0)) if cur else None


def _attn_kernel(*refs, n_heads):
    n_win = NBUF // TQ
    q_ref = refs[0]
    k_refs = refs[1:1 + n_win]
    v_refs = refs[1 + n_win:1 + 2 * n_win]
    rbp_ref, o_ref, bias_s = refs[1 + 2 * n_win:]
    b = pl.program_id(0)
    t = pl.program_id(1)

    @pl.when(jnp.logical_and(b == 0, t == 0))
    def _():
        _build_bias_tables(rbp_ref, bias_s, n_heads)

    @pl.when(t < n_win - 1)
    def _():
        first_valid = TQ * (n_win - 1 - t)
        _attn_tile(q_ref, k_refs, v_refs, o_ref, bias_s, first_valid, n_heads)

    @pl.when(t >= n_win - 1)
    def _():
        _attn_tile(q_ref, k_refs, v_refs, o_ref, bias_s, None, n_heads)


def _attention(qkv, rbp):
    bsz, seq, d3 = qkv.shape
    d_att = d3 // 3
    n_heads = d_att // HEAD_DIM
    n_win = NBUF // TQ
    assert seq % TQ == 0 and NBUF % TQ == 0 and n_heads % HEADS_PER_VREG == 0
    kern = functools.partial(_attn_kernel, n_heads=n_heads)

    def window(col, j):
        return pl.BlockSpec((None, TQ, d_att),
                            lambda b, t: (b, jnp.maximum(t - (n_win - 1 - j), 0), col))

    return pl.pallas_call(
        kern,
        grid=(bsz, seq // TQ),
        in_specs=([pl.BlockSpec((None, TQ, d_att), lambda b, t: (b, t, 0))]
                  + [window(1, j) for j in range(n_win)]
                  + [window(2, j) for j in range(n_win)]
                  + [_single(rbp.shape, lambda b, t: (0, 0))]),
        out_specs=pl.BlockSpec((None, TQ, d_att), lambda b, t: (b, t, 0)),
        out_shape=jax.ShapeDtypeStruct((bsz, seq, d_att), BF16),
        scratch_shapes=[pltpu.VMEM((n_heads, 2, CHUNK, WIN), F32)],
        compiler_params=pltpu.CompilerParams(dimension_semantics=("arbitrary", "arbitrary"),
                                             vmem_limit_bytes=VMEM_LIMIT_BYTES),
        name="chunk_attn",
    )(*([qkv] * (1 + 2 * n_win)), rbp)


def _outproj_ffn_kernel(x_ref, yr_ref, ya_ref, mod_ref, n2g_ref, fg_ref, wo_ref,
                        wu_ref, fcw_ref, fcb_ref, wd_ref, o_ref, halo_s,
                        *, d_model, d_ff, conv_w):
    i = pl.program_id(0)
    rows = TT * SUBLANES
    halo = (conv_w - 1) * SUBLANES
    n_hp = TT // HT

    @pl.when(i == 0)
    def _():
        halo_s[...] = jnp.zeros_like(halo_s)

    g1 = mod_ref[:, :, 2 * d_model:3 * d_model]
    sh2 = mod_ref[:, :, 3 * d_model:4 * d_model]
    gain2 = n2g_ref[...] * (1.0 + mod_ref[:, :, 4 * d_model:5 * d_model])
    g2 = mod_ref[:, :, 5 * d_model:6 * d_model]

    def out_proj(t):
        ts = slice(t * TT, (t + 1) * TT)
        yr = yr_ref[:, ts, :].reshape(rows, yr_ref.shape[2])
        ya = ya_ref[:, ts, :].reshape(rows, ya_ref.shape[2])
        d_rnn = yr_ref.shape[2]
        y = (jnp.dot(yr, wo_ref[0:d_rnn, :], preferred_element_type=F32)
             + jnp.dot(ya, wo_ref[d_rnn:, :], preferred_element_type=F32))
        return y.reshape(SUBLANES, TT, d_model)

    def head_piece(t, c, y3):
        cs = slice(c * HT, (c + 1) * HT)
        x1 = x_ref[:, t * TT + c * HT:t * TT + (c + 1) * HT, :] + g1 * y3[:, cs, :]
        ms = jnp.mean(x1 * x1, axis=-1, keepdims=True)
        hn = x1 * lax.rsqrt(ms + EPS) * gain2 + sh2
        return x1, jnp.swapaxes(hn, 0, 1).reshape(HT * SUBLANES, d_model).astype(BF16)

    def tail_piece(t, c, z, x1):
        zc = z[c * HT * SUBLANES:(c + 1) * HT * SUBLANES, :].reshape(HT, SUBLANES, d_model)
        x2 = x1 + g2 * jnp.swapaxes(zc, 0, 1)
        ms2 = jnp.mean(x2 * x2, axis=-1, keepdims=True)
        o_ref[:, t * TT + c * HT:t * TT + (c + 1) * HT, :] = x2 * lax.rsqrt(ms2 + EPS) * fg_ref[...]

    def up_piece(hb, cols):
        return jnp.dot(hb, wu_ref[:, cols], preferred_element_type=F32)

    def conv_piece(up, tail, cols):
        ext = jnp.concatenate([tail, up], axis=0)
        u = fcb_ref[:, cols]
        for k in range(conv_w):
            u = u + ext[k * SUBLANES:k * SUBLANES + rows, :] * fcw_ref[k:k + 1, cols]
        return u, ext[rows:rows + halo, :]

    def ffn(hb, tails, side):
        side = list(side)
        z = None
        new_tails = []
        for j, (c0, w) in enumerate(FF_SPLITS):
            cg = slice(c0, c0 + w)
            cv = slice(d_ff + c0, d_ff + c0 + w)
            up_g = up_piece(hb, cg)
            if side:
                side.pop(0)()
            up_v = up_piece(hb, cv)
            if side:
                side.pop(0)()
            ug, tg = conv_piece(up_g, tails[j][0], cg)
            uv, tv = conv_piece(up_v, tails[j][1], cv)
            new_tails.append((tg, tv))
            act = (ug * jax.nn.sigmoid(ug) * uv).astype(BF16)
            zc = jnp.dot(act, wd_ref[c0:c0 + w, :], preferred_element_type=F32)
            z = zc if z is None else z + zc
        assert not side
        return z, new_tails

    tails = [(halo_s[:, c0:c0 + w], halo_s[:, d_ff + c0:d_ff + c0 + w]) for c0, w in FF_SPLITS]
    y3 = out_proj(0)
    hp = [head_piece(0, c, y3) for c in range(n_hp)]
    x1_0 = [p[0] for p in hp]
    hb_0 = jnp.concatenate([p[1] for p in hp], axis=0)

    state = {}

    def side_out_proj():
        state["y3"] = out_proj(1)

    def side_head(c):
        def run():
            state.setdefault("hp", []).append(head_piece(1, c, state["y3"]))
        return run

    z_0, tails = ffn(hb_0, tails, [side_out_proj] + [side_head(c) for c in range(n_hp)])
    x1_1 = [p[0] for p in state["hp"]]
    hb_1 = jnp.concatenate([p[1] for p in state["hp"]], axis=0)

    def side_tail(c):
        return lambda: tail_piece(0, c, z_0, x1_0[c])

    z_1, tails = ffn(hb_1, tails, [side_tail(c) for c in range(n_hp)])
    for c in range(n_hp):
        tail_piece(1, c, z_1, x1_1[c])
    for (c0, w), (tg, tv) in zip(FF_SPLITS, tails):
        halo_s[:, c0:c0 + w] = tg
        halo_s[:, d_ff + c0:d_ff + c0 + w] = tv


def _outproj_ffn(x, yr, ya, mod, n2g, fg, w_out_b, w_up_b, fcw, fcb, w_down_b):
    bsz, seq, d = x.shape
    d_ff = w_down_b.shape[0]
    cw = fcw.shape[0]
    d_rnn = yr.shape[2]
    d_att = ya.shape[2]
    step_t = K3_TILES * TT
    assert sum(w for _, w in FF_SPLITS) == d_ff and seq % step_t == 0 and TT % HT == 0
    const = lambda i: (0, 0)
    kern = functools.partial(_outproj_ffn_kernel, d_model=d, d_ff=d_ff, conv_w=cw)
    return pl.pallas_call(
        kern,
        grid=(seq // step_t,),
        in_specs=[
            pl.BlockSpec((bsz, step_t, d), lambda i: (0, i, 0)),
            pl.BlockSpec((bsz, step_t, d_rnn), lambda i: (0, i, 0)),
            pl.BlockSpec((bsz, step_t, d_att), lambda i: (0, i, 0)),
            _single(mod.shape, lambda i: (0, 0, 0)),
            _single((1, d), const),
            _single((1, d), const),
            _single(w_out_b.shape, const),
            _single(w_up_b.shape, const),
            _single(fcw.shape, const),
            _single((1, 2 * d_ff), const),
            _single(w_down_b.shape, const),
        ],
        out_specs=pl.BlockSpec((bsz, step_t, d), lambda i: (0, i, 0)),
        out_shape=jax.ShapeDtypeStruct((bsz, seq, d), F32),
        scratch_shapes=[pltpu.VMEM(((cw - 1) * SUBLANES, 2 * d_ff), F32)],
        compiler_params=pltpu.CompilerParams(dimension_semantics=("arbitrary",),
                                             vmem_limit_bytes=VMEM_LIMIT_BYTES),
        name="outproj_ffn",
    )(x, yr, ya, mod, n2g.reshape(1, d), fg.reshape(1, d), w_out_b, w_up_b, fcw,
      fcb.reshape(1, 2 * d_ff), w_down_b)


def kernel(x, c, ada_w, ada_b, norm1_g, w_in, rnn_conv_w, rnn_conv_b, rg_wa, rg_ba, rg_wx, rg_bx, rg_lambda, rel_bias, w_out, norm2_g, w_up, ffn_conv_w, ffn_conv_b, w_down, final_g):
    bsz, seq, d = x.shape
    assert ada_w.shape[0] == 1
    l = 0
    d_rnn = rnn_conv_w.shape[2]
    d_att = w_out.shape[1] - d_rnn
    mod = _ada(c, ada_w[l], ada_b[l])
    qkv, yr, (w_out_b, w_up_b, w_down_b) = _inproj_rglru(
        x, mod, norm1_g[l], w_in[l], rnn_conv_w[l], rnn_conv_b[l], rg_wa[l], rg_wx[l],
        rg_ba[l], rg_bx[l], rg_lambda[l], (w_out[l], w_up[l], w_down[l]))
    rbp = jnp.pad(rel_bias[l], ((0, 0), (0, RBPAD - rel_bias.shape[2])))
    ya = _attention(qkv, rbp)
    return _outproj_ffn(x, yr, ya, mod, norm2_g[l], final_g, w_out_b, w_up_b,
                        ffn_conv_w[l], ffn_conv_b[l], w_down_b)
```

```python
import functools

import jax
import jax.numpy as jnp
from jax import lax
from jax.experimental import pallas as pl
from jax.experimental.pallas import tpu as pltpu

F32 = jnp.float32
BF16 = jnp.bfloat16

CHUNK = 64
LOOKBACK = 8
REL_CLIP = 128
HEAD_DIM = 64
RG_C = 8.0
EPS = 1e-6
NEG_INF = -1e30
LOG2E = 1.4426950408889634

SUBLANES = 8
LANES = 128
MXU_DIM = 256
VMEM_LIMIT_BYTES = 56 * 1024 * 1024

TT = 64
RT = 8
PW = 256
T1 = 128
CHUNKS_PER_TILE = 4
TQ = CHUNKS_PER_TILE * CHUNK
HEADS_PER_VREG = LANES // HEAD_DIM
HIST = LOOKBACK * CHUNK
NBUF = HIST + TQ
WIN = (LOOKBACK + 2) * CHUNK
GPAD = 768
RBPAD = 384
FF_SPLITS = ((0, 1024), (1024, 1024), (2048, 768))
K3_TILES = 2
HT = 16


def _single(block_shape, index_map):
    return pl.BlockSpec(block_shape, index_map, pipeline_mode=pl.Buffered(1))


def _ada_kernel(c_ref, w_ref, b_ref, o_ref):
    c = c_ref[...]
    sc = c * jax.nn.sigmoid(c)
    w = w_ref[...]
    w_hi = w.astype(BF16)
    w_lo = (w - w_hi.astype(F32)).astype(BF16)
    sc_hi = sc.astype(BF16)
    sc_lo = (sc - sc_hi.astype(F32)).astype(BF16)
    acc = jnp.dot(sc_hi, w_hi, preferred_element_type=F32)
    acc = acc + jnp.dot(sc_lo, w_hi, preferred_element_type=F32)
    acc = acc + jnp.dot(sc_hi, w_lo, preferred_element_type=F32)
    o_ref[...] = (acc + b_ref[...]).reshape(o_ref.shape)


def _ada(c, ada_w, ada_b):
    bsz, d = c.shape
    n = ada_w.shape[1]
    bn = 1024
    return pl.pallas_call(
        _ada_kernel,
        grid=(n // bn,),
        in_specs=[pl.BlockSpec((bsz, d), lambda j: (0, 0)),
                  pl.BlockSpec((d, bn), lambda j: (0, j)),
                  pl.BlockSpec((1, bn), lambda j: (0, j))],
        out_specs=pl.BlockSpec((bsz, 1, bn), lambda j: (0, 0, j)),
        out_shape=jax.ShapeDtypeStruct((bsz, 1, n), F32),
        name="ada_mod",
    )(c, ada_w, ada_b.reshape(1, n))


def _inproj_rglru_kernel(x_ref, mod_ref, n1g_ref, win_ref, cw_ref, cb_ref, wa_ref, wx_ref,
                         ba_ref, bx_ref, lam_ref, *rest, d_model, d_rnn, d_att, conv_w, n_cast):
    cast_in = rest[:n_cast]
    qkv_ref, yr_ref = rest[n_cast:n_cast + 2]
    cast_out = rest[n_cast + 2:2 * n_cast + 2]
    tail_s, hc, win_s, wg_s = rest[2 * n_cast + 2:]
    i = pl.program_id(0)

    for src, dst in zip(cast_in, cast_out):
        dst[...] = src[...].astype(BF16)
    halo = (conv_w - 1) * SUBLANES
    prow = RT * SUBLANES
    half = d_rnn // 2

    @pl.when(i == 0)
    def _():
        tail_s[...] = jnp.zeros_like(tail_s)
        hc[...] = jnp.zeros_like(hc)
        for c0 in range(0, win_ref.shape[1], PW):
            win_s[:, c0:c0 + PW] = win_ref[:, c0:c0 + PW].astype(BF16)
        wg_s[...] = jnp.zeros_like(wg_s)
        nb, bw = wa_ref.shape[0], wa_ref.shape[1]
        per_half = nb // 2
        for n in range(nb):
            hf, k = divmod(n, per_half)
            rs = slice(k * bw, (k + 1) * bw)
            wg_s[hf, rs, k * bw:(k + 1) * bw] = wa_ref[n].astype(BF16)
            wg_s[hf, rs, half + k * bw:half + (k + 1) * bw] = wx_ref[n].astype(BF16)

    gain1 = n1g_ref[...] * (1.0 + mod_ref[:, :, d_model:2 * d_model])
    sh1 = mod_ref[:, :, 0:d_model]
    trow = TT * SUBLANES
    n_rec = TT // RT

    def normed_lhs(t):
        x = x_ref[:, t * TT:(t + 1) * TT, :]
        ms = jnp.mean(x * x, axis=-1, keepdims=True)
        h = x * lax.rsqrt(ms + EPS) * gain1 + sh1
        return h.reshape(trow, d_model).astype(BF16)

    def project_cols(hb, lo, hi):
        part = jnp.dot(hb, win_s[:, lo:hi], preferred_element_type=F32)
        return part.reshape(SUBLANES, TT, hi - lo)

    def branch(hb, j):
        return [project_cols(hb, j * d_rnn + p, j * d_rnn + p + PW) for p in range(0, d_rnn, PW)]

    def emit_qkv(t, hb, p):
        o_q = 2 * d_rnn
        part = project_cols(hb, o_q + p * PW, o_q + (p + 1) * PW)
        if (p + 1) * PW <= d_att:
            part = part * (LOG2E * HEAD_DIM ** -0.5)
        qkv_ref[:, t * TT:(t + 1) * TT, p * PW:(p + 1) * PW] = part.astype(BF16)

    def recur_piece(t, xr_p, gr_p, c, tail, hcur):
        ts = slice(c * RT, (c + 1) * RT)
        xr = jnp.concatenate([jnp.swapaxes(p[:, ts, :], 0, 1).reshape(prow, PW) for p in xr_p], axis=1)
        gr = jnp.concatenate([jnp.swapaxes(p[:, ts, :], 0, 1).reshape(prow, PW) for p in gr_p], axis=1)
        ext = jnp.concatenate([tail, xr], axis=0)
        xc = cb_ref[...]
        for k in range(conv_w):
            xc = xc + ext[k * SUBLANES:k * SUBLANES + prow, :] * cw_ref[k:k + 1, :]
        xcb = xc.astype(BF16)
        g0 = jnp.dot(xcb[:, 0:half], wg_s[0], preferred_element_type=F32)
        g1 = jnp.dot(xcb[:, half:d_rnn], wg_s[1], preferred_element_type=F32)
        r = jax.nn.sigmoid(jnp.concatenate([g0[:, 0:half], g1[:, 0:half]], axis=1) + ba_ref[...])
        ig = jax.nn.sigmoid(jnp.concatenate([g0[:, half:], g1[:, half:]], axis=1) + bx_ref[...])
        lam = lam_ref[...]
        log_sig = jnp.minimum(lam, 0.0) - jnp.log1p(jnp.exp(-jnp.abs(lam)))
        log_a = RG_C * r * log_sig
        a = jnp.exp(log_a)
        w = jnp.tanh(-log_a) * (a * a + 1.0)
        mult = jnp.where(w > 0.0, w * lax.rsqrt(w), 0.0)
        a3 = a.reshape(RT, SUBLANES, d_rnn)
        b3 = (mult * (ig * xc)).reshape(RT, SUBLANES, d_rnn)
        hs = []
        for s in range(RT):
            hcur = a3[s] * hcur + b3[s]
            hs.append(hcur)
        yr = jnp.stack(hs, axis=0) * jax.nn.gelu(gr).reshape(RT, SUBLANES, d_rnn)
        yr_ref[:, t * TT + c * RT:t * TT + (c + 1) * RT, :] = jnp.swapaxes(yr, 0, 1).astype(BF16)
        return ext[prow:prow + halo, :], hcur

    n_tiles = T1 // TT
    n_qkv = (3 * d_att) // PW
    tail, hcur = tail_s[...], hc[...]
    hb = normed_lhs(0)
    xr_p, gr_p = branch(hb, 0), branch(hb, 1)
    for t in range(n_tiles):
        nxt = t + 1 < n_tiles
        ahead = {}
        for c in range(max(n_rec, n_qkv)):
            if c < n_qkv:
                emit_qkv(t, hb, c)
            if c < n_rec:
                tail, hcur = recur_piece(t, xr_p, gr_p, c, tail, hcur)
            if nxt and c == n_rec - 3:
                ahead["hb"] = normed_lhs(t + 1)
            if nxt and c == n_rec - 2:
                ahead["xr"] = branch(ahead["hb"], 0)
            if nxt and c == n_rec - 1:
                ahead["gr"] = branch(ahead["hb"], 1)
        if nxt:
            hb, xr_p, gr_p = ahead["hb"], ahead["xr"], ahead["gr"]
    tail_s[...] = tail
    hc[...] = hcur


def _inproj_rglru(x, mod, n1g, w_in, conv_w, conv_b, wa, wx, ba, bx, lam, later_weights):
    bsz, seq, d = x.shape
    assert bsz == SUBLANES and seq % T1 == 0
    d_in = w_in.shape[1]
    d_rnn = conv_w.shape[1]
    d_att = (d_in - 2 * d_rnn) // 3
    assert d_att == d_rnn and T1 % TT == 0 and TT % RT == 0
    cw = conv_w.shape[0]
    const = lambda i: (0, 0)
    n_steps = seq // T1
    assert all(w.shape[0] % (2 * SUBLANES * n_steps) == 0 for w in later_weights)
    cast_specs = [pl.BlockSpec((w.shape[0] // n_steps, w.shape[1]), lambda i: (i, 0))
                  for w in later_weights]
    kern = functools.partial(_inproj_rglru_kernel, d_model=d, d_rnn=d_rnn, d_att=d_att, conv_w=cw,
                             n_cast=len(later_weights))
    outs = pl.pallas_call(
        kern,
        grid=(n_steps,),
        in_specs=[
            pl.BlockSpec((bsz, T1, d), lambda i: (0, i, 0)),
            _single(mod.shape, lambda i: (0, 0, 0)),
            _single((1, d), const),
            _single(w_in.shape, const),
            _single(conv_w.shape, const),
            _single((1, d_rnn), const),
            _single(wa.shape, lambda i: (0, 0, 0)),
            _single(wx.shape, lambda i: (0, 0, 0)),
            _single((1, d_rnn), const),
            _single((1, d_rnn), const),
            _single((1, d_rnn), const),
        ] + cast_specs,
        out_specs=[pl.BlockSpec((bsz, T1, 3 * d_att), lambda i: (0, i, 0)),
                   pl.BlockSpec((bsz, T1, d_rnn), lambda i: (0, i, 0))] + cast_specs,
        out_shape=[jax.ShapeDtypeStruct((bsz, seq, 3 * d_att), BF16),
                   jax.ShapeDtypeStruct((bsz, seq, d_rnn), BF16)]
        + [jax.ShapeDtypeStruct(w.shape, BF16) for w in later_weights],
        scratch_shapes=[
            pltpu.VMEM(((cw - 1) * SUBLANES, d_rnn), F32),
            pltpu.VMEM((SUBLANES, d_rnn), F32),
            pltpu.VMEM(w_in.shape, BF16),
            pltpu.VMEM((2, d_rnn // 2, d_rnn), BF16),
        ],
        compiler_params=pltpu.CompilerParams(dimension_semantics=("arbitrary",),
                                             vmem_limit_bytes=VMEM_LIMIT_BYTES),
        name="inproj_rglru",
    )(x, mod, n1g.reshape(1, d), w_in, conv_w, conv_b.reshape(1, d_rnn), wa, wx,
      ba.reshape(1, d_rnn), bx.reshape(1, d_rnn), lam.reshape(1, d_rnn), *later_weights)
    return outs[0], outs[1], outs[2:]


def _build_bias_tables(rbp_ref, bias_s, n_heads):
    r_io = lax.broadcasted_iota(jnp.int32, (RBPAD, GPAD), 0)
    j_io = lax.broadcasted_iota(jnp.int32, (RBPAD, GPAD), 1)
    idx = jnp.clip(HIST + CHUNK - 1 - j_io, -REL_CLIP, REL_CLIP) + REL_CLIP
    sel = (r_io == idx).astype(F32)
    g = jnp.dot(rbp_ref[...], sel, precision=lax.Precision.HIGHEST, preferred_element_type=F32)
    g = (g - g[:, 0:1]) * LOG2E
    col = lax.broadcasted_iota(jnp.int32, (CHUNK, WIN), 1)
    band = (LOOKBACK + 1) * CHUNK
    for h in range(n_heads):
        gh = jnp.broadcast_to(g[h:h + 1, :], (CHUNK, GPAD))
        even = pltpu.roll(gh, GPAD - (CHUNK - 1), 1, stride=1, stride_axis=0)[:, 0:WIN]
        odd = pltpu.roll(gh, 1, 1, stride=1, stride_axis=0)[:, 0:WIN]
        bias_s[h, 0] = jnp.where(col < band, even, NEG_INF)
        bias_s[h, 1] = jnp.where(col >= CHUNK, odd, NEG_INF)


ZERO_BIAS_COLS = ((0, HIST - REL_CLIP), (LANES, HIST - REL_CLIP))


def _attn_tile(q_ref, k_refs, v_refs, o_ref, bias_s, first_valid, n_heads):
    lane = lax.broadcasted_iota(jnp.int32, (1, LANES), 1)
    pad = jnp.zeros((CHUNK, NBUF - WIN), BF16)
    n_kt = len(k_refs)

    def masked_q(h):
        pair, hh = divmod(h, HEADS_PER_VREG)
        qp = q_ref[:, pair * LANES:(pair + 1) * LANES]
        in_head = jnp.logical_and(lane >= hh * HEAD_DIM, lane < (hh + 1) * HEAD_DIM)
        return jnp.where(in_head, qp, jnp.zeros_like(qp))

    def score_piece(h, qm, j):
        pair = h // HEADS_PER_VREG
        return lax.dot_general(qm, k_refs[j][:, pair * LANES:(pair + 1) * LANES],
                               (((1,), (1,)), ((), ())), preferred_element_type=F32)

    def softmax_piece(h, s_all, c):
        w0 = (c // 2) * LANES
        raw = s_all[c * CHUNK:(c + 1) * CHUNK, w0:w0 + WIN]
        z0, z1 = ZERO_BIAS_COLS[c % 2]
        parts = [raw[:, z0:z1], raw[:, z1:WIN] + bias_s[h, c % 2, :, z1:WIN]]
        if z0:
            parts.insert(0, raw[:, 0:z0] + bias_s[h, c % 2, :, 0:z0])
        s = jnp.concatenate(parts, axis=1)
        if first_valid is not None:
            col = lax.broadcasted_iota(jnp.int32, (1, WIN), 1) + w0
            s = jnp.where(col >= first_valid, s, NEG_INF)
        m = jnp.max(s, axis=-1, keepdims=True)
        p = jnp.exp2(s - m)
        inv = 1.0 / jnp.sum(p, axis=-1, keepdims=True)
        pb = p.astype(BF16)
        return inv, jnp.concatenate([pb, pad] if w0 == 0 else [pad, pb], axis=1)

    def value_piece(h, pmat, j):
        pair = h // HEADS_PER_VREG
        return jnp.dot(pmat[:, j * TQ:(j + 1) * TQ], v_refs[j][:, pair * LANES:(pair + 1) * LANES],
                       preferred_element_type=F32)

    def finish(h, o, inv, o_prev):
        pair, hh = divmod(h, HEADS_PER_VREG)
        o = o * inv
        if hh == 0:
            return o
        in_head = jnp.logical_and(lane >= hh * HEAD_DIM, lane < (hh + 1) * HEAD_DIM)
        o_ref[:, pair * LANES:(pair + 1) * LANES] = jnp.where(in_head, o, o_prev).astype(BF16)
        return None

    assert n_kt <= CHUNKS_PER_TILE and HEADS_PER_VREG == 2
    qm = masked_q(0)
    s_next = jnp.concatenate([score_piece(0, qm, j) for j in range(n_kt)], axis=1)
    o_pair = None
    prev = None
    for h in range(n_heads + 1):
        cur = h < n_heads
        nxt = h + 1 < n_heads
        s_cur = s_next
        if nxt:
            qm = masked_q(h + 1)
        pieces, invs, prow, o_acc = [], [], [], None
        for c in range(CHUNKS_PER_TILE):
            if nxt and c < n_kt:
                pieces.append(score_piece(h + 1, qm, c))
            if prev is not None and c < n_kt:
                part = value_piece(prev[0], prev[1], c)
                o_acc = part if o_acc is None else o_acc + part
            if cur:
                inv, pr = softmax_piece(h, s_cur, c)
                invs.append(inv)
                prow.append(pr)
        if prev is not None:
            o_pair = finish(prev[0], o_acc, prev[2], o_pair)
        if nxt:
            s_next = jnp.concatenate(pieces, axis=1)
        prev = (h, jnp.concatenate(prow, axis=0), jnp.concatenate(invs, axis=0)) if cur else None


def _attn_kernel(*refs, n_heads):
    n_win = NBUF // TQ
    q_ref = refs[0]
    k_refs = refs[1:1 + n_win]
    v_refs = refs[1 + n_win:1 + 2 * n_win]
    rbp_ref, o_ref, bias_s = refs[1 + 2 * n_win:]
    b = pl.program_id(0)
    t = pl.program_id(1)

    @pl.when(jnp.logical_and(b == 0, t == 0))
    def _():
        _build_bias_tables(rbp_ref, bias_s, n_heads)

    @pl.when(t < n_win - 1)
    def _():
        first_valid = TQ * (n_win - 1 - t)
        _attn_tile(q_ref, k_refs, v_refs, o_ref, bias_s, first_valid, n_heads)

    @pl.when(t >= n_win - 1)
    def _():
        _attn_tile(q_ref, k_refs, v_refs, o_ref, bias_s, None, n_heads)


def _attention(qkv, rbp):
    bsz, seq, d3 = qkv.shape
    d_att = d3 // 3
    n_heads = d_att // HEAD_DIM
    n_win = NBUF // TQ
    assert seq % TQ == 0 and NBUF % TQ == 0 and n_heads % HEADS_PER_VREG == 0
    kern = functools.partial(_attn_kernel, n_heads=n_heads)

    def window(col, j):
        return pl.BlockSpec((None, TQ, d_att),
                            lambda b, t: (b, jnp.maximum(t - (n_win - 1 - j), 0), col))

    return pl.pallas_call(
        kern,
        grid=(bsz, seq // TQ),
        in_specs=([pl.BlockSpec((None, TQ, d_att), lambda b, t: (b, t, 0))]
                  + [window(1, j) for j in range(n_win)]
                  + [window(2, j) for j in range(n_win)]
                  + [_single(rbp.shape, lambda b, t: (0, 0))]),
        out_specs=pl.BlockSpec((None, TQ, d_att), lambda b, t: (b, t, 0)),
        out_shape=jax.ShapeDtypeStruct((bsz, seq, d_att), BF16),
        scratch_shapes=[pltpu.VMEM((n_heads, 2, CHUNK, WIN), F32)],
        compiler_params=pltpu.CompilerParams(dimension_semantics=("arbitrary", "arbitrary"),
                                             vmem_limit_bytes=VMEM_LIMIT_BYTES),
        name="chunk_attn",
    )(*([qkv] * (1 + 2 * n_win)), rbp)


def _outproj_ffn_kernel(x_ref, yr_ref, ya_ref, mod_ref, n2g_ref, fg_ref, wo_ref,
                        wu_ref, fcw_ref, fcb_ref, wd_ref, o_ref, halo_s,
                        *, d_model, d_ff, conv_w):
    i = pl.program_id(0)
    rows = TT * SUBLANES
    halo = (conv_w - 1) * SUBLANES
    n_hp = TT // HT

    @pl.when(i == 0)
    def _():
        halo_s[...] = jnp.zeros_like(halo_s)

    g1 = mod_ref[:, :, 2 * d_model:3 * d_model]
    sh2 = mod_ref[:, :, 3 * d_model:4 * d_model]
    gain2 = n2g_ref[...] * (1.0 + mod_ref[:, :, 4 * d_model:5 * d_model])
    g2 = mod_ref[:, :, 5 * d_model:6 * d_model]

    def out_proj(t):
        ts = slice(t * TT, (t + 1) * TT)
        yr = yr_ref[:, ts, :].reshape(rows, yr_ref.shape[2])
        ya = ya_ref[:, ts, :].reshape(rows, ya_ref.shape[2])
        d_rnn = yr_ref.shape[2]
        y = (jnp.dot(yr, wo_ref[0:d_rnn, :], preferred_element_type=F32)
             + jnp.dot(ya, wo_ref[d_rnn:, :], preferred_element_type=F32))
        return y.reshape(SUBLANES, TT, d_model)

    def head_piece(t, c, y3):
        cs = slice(c * HT, (c + 1) * HT)
        x1 = x_ref[:, t * TT + c * HT:t * TT + (c + 1) * HT, :] + g1 * y3[:, cs, :]
        ms = jnp.mean(x1 * x1, axis=-1, keepdims=True)
        hn = x1 * lax.rsqrt(ms + EPS) * gain2 + sh2
        return x1, jnp.swapaxes(hn, 0, 1).reshape(HT * SUBLANES, d_model).astype(BF16)

    def tail_piece(t, c, z, x1):
        zc = z[c * HT * SUBLANES:(c + 1) * HT * SUBLANES, :].reshape(HT, SUBLANES, d_model)
        x2 = x1 + g2 * jnp.swapaxes(zc, 0, 1)
        ms2 = jnp.mean(x2 * x2, axis=-1, keepdims=True)
        o_ref[:, t * TT + c * HT:t * TT + (c + 1) * HT, :] = x2 * lax.rsqrt(ms2 + EPS) * fg_ref[...]

    def up_piece(hb, cols):
        return jnp.dot(hb, wu_ref[:, cols], preferred_element_type=F32)

    def conv_piece(up, tail, cols):
        ext = jnp.concatenate([tail, up], axis=0)
        u = fcb_ref[:, cols]
        for k in range(conv_w):
            u = u + ext[k * SUBLANES:k * SUBLANES + rows, :] * fcw_ref[k:k + 1, cols]
        return u, ext[rows:rows + halo, :]

    def ffn(hb, tails, side):
        side = list(side)
        z = None
        new_tails = []
        for j, (c0, w) in enumerate(FF_SPLITS):
            cg = slice(c0, c0 + w)
            cv = slice(d_ff + c0, d_ff + c0 + w)
            up_g = up_piece(hb, cg)
            if side:
                side.pop(0)()
            up_v = up_piece(hb, cv)
            if side:
                side.pop(0)()
            ug, tg = conv_piece(up_g, tails[j][0], cg)
            uv, tv = conv_piece(up_v, tails[j][1], cv)
            new_tails.append((tg, tv))
            act = (ug * jax.nn.sigmoid(ug) * uv).astype(BF16)
            zc = jnp.dot(act, wd_ref[c0:c0 + w, :], preferred_element_type=F32)
            z = zc if z is None else z + zc
        assert not side
        return z, new_tails

    tails = [(halo_s[:, c0:c0 + w], halo_s[:, d_ff + c0:d_ff + c0 + w]) for c0, w in FF_SPLITS]
    y3 = out_proj(0)
    hp = [head_piece(0, c, y3) for c in range(n_hp)]
    x1_0 = [p[0] for p in hp]
    hb_0 = jnp.concatenate([p[1] for p in hp], axis=0)

    state = {}

    def side_out_proj():
        state["y3"] = out_proj(1)

    def side_head(c):
        def run():
            state.setdefault("hp", []).append(head_piece(1, c, state["y3"]))
        return run

    z_0, tails = ffn(hb_0, tails, [side_out_proj] + [side_head(c) for c in range(n_hp)])
    x1_1 = [p[0] for p in state["hp"]]
    hb_1 = jnp.concatenate([p[1] for p in state["hp"]], axis=0)

    def side_tail(c):
        return lambda: tail_piece(0, c, z_0, x1_0[c])

    z_1, tails = ffn(hb_1, tails, [side_tail(c) for c in range(n_hp)])
    for c in range(n_hp):
        tail_piece(1, c, z_1, x1_1[c])
    for (c0, w), (tg, tv) in zip(FF_SPLITS, tails):
        halo_s[:, c0:c0 + w] = tg
        halo_s[:, d_ff + c0:d_ff + c0 + w] = tv


def _outproj_ffn(x, yr, ya, mod, n2g, fg, w_out_b, w_up_b, fcw, fcb, w_down_b):
    bsz, seq, d = x.shape
    d_ff = w_down_b.shape[0]
    cw = fcw.shape[0]
    d_rnn = yr.shape[2]
    d_att = ya.shape[2]
    step_t = K3_TILES * TT
    assert sum(w for _, w in FF_SPLITS) == d_ff and seq % step_t == 0 and TT % HT == 0
    const = lambda i: (0, 0)
    kern = functools.partial(_outproj_ffn_kernel, d_model=d, d_ff=d_ff, conv_w=cw)
    return pl.pallas_call(
        kern,
        grid=(seq // step_t,),
        in_specs=[
            pl.BlockSpec((bsz, step_t, d), lambda i: (0, i, 0)),
            pl.BlockSpec((bsz, step_t, d_rnn), lambda i: (0, i, 0)),
            pl.BlockSpec((bsz, step_t, d_att), lambda i: (0, i, 0)),
            _single(mod.shape, lambda i: (0, 0, 0)),
            _single((1, d), const),
            _single((1, d), const),
            _single(w_out_b.shape, const),
            _single(w_up_b.shape, const),
            _single(fcw.shape, const),
            _single((1, 2 * d_ff), const),
            _single(w_down_b.shape, const),
        ],
        out_specs=pl.BlockSpec((bsz, step_t, d), lambda i: (0, i, 0)),
        out_shape=jax.ShapeDtypeStruct((bsz, seq, d), F32),
        scratch_shapes=[pltpu.VMEM(((cw - 1) * SUBLANES, 2 * d_ff), F32)],
        compiler_params=pltpu.CompilerParams(dimension_semantics=("arbitrary",),
                                             vmem_limit_bytes=VMEM_LIMIT_BYTES),
        name="outproj_ffn",
    )(x, yr, ya, mod, n2g.reshape(1, d), fg.reshape(1, d), w_out_b, w_up_b, fcw,
      fcb.reshape(1, 2 * d_ff), w_down_b)


def kernel(x, c, ada_w, ada_b, norm1_g, w_in, rnn_conv_w, rnn_conv_b, rg_wa, rg_ba, rg_wx, rg_bx, rg_lambda, rel_bias, w_out, norm2_g, w_up, ffn_conv_w, ffn_conv_b, w_down, final_g):
    bsz, seq, d = x.shape
    assert ada_w.shape[0] == 1
    l = 0
    d_rnn = rnn_conv_w.shape[2]
    d_att = w_out.shape[1] - d_rnn
    mod = _ada(c, ada_w[l], ada_b[l])
    qkv, yr, (w_out_b, w_up_b, w_down_b) = _inproj_rglru(
        x, mod, norm1_g[l], w_in[l], rnn_conv_w[l], rnn_conv_b[l], rg_wa[l], rg_wx[l],
        rg_ba[l], rg_bx[l], rg_lambda[l], (w_out[l], w_up[l], w_down[l]))
    rbp = jnp.pad(rel_bias[l], ((0, 0), (0, RBPAD - rel_bias.shape[2])))
    ya = _attention(qkv, rbp)
    return _outproj_ffn(x, yr, ya, mod, norm2_g[l], final_g, w_out_b, w_up_b,
                        ffn_conv_w[l], ffn_conv_b[l], w_down_b)
```

```python
import functools

import jax
import jax.numpy as jnp
from jax import lax
from jax.experimental import pallas as pl
from jax.experimental.pallas import tpu as pltpu

F32 = jnp.float32
BF16 = jnp.bfloat16

CHUNK = 64
LOOKBACK = 8
REL_CLIP = 128
HEAD_DIM = 64
RG_C = 8.0
EPS = 1e-6
NEG_INF = -1e30
LOG2E = 1.4426950408889634

SUBLANES = 8
LANES = 128
MXU_DIM = 256
VMEM_LIMIT_BYTES = 56 * 1024 * 1024

TT = 64
RT = 8
PW = 256
T1 = 128
CHUNKS_PER_TILE = 4
TQ = CHUNKS_PER_TILE * CHUNK
HEADS_PER_VREG = LANES // HEAD_DIM
HIST = LOOKBACK * CHUNK
NBUF = HIST + TQ
WIN = (LOOKBACK + 2) * CHUNK
GPAD = 768
RBPAD = 384
FF_SPLITS = ((0, 1024), (1024, 1024), (2048, 768))
K3_TILES = 2
HT = 16


def _single(block_shape, index_map):
    return pl.BlockSpec(block_shape, index_map, pipeline_mode=pl.Buffered(1))


def _ada_kernel(c_ref, w_ref, b_ref, o_ref):
    c = c_ref[...]
    sc = c * jax.nn.sigmoid(c)
    w = w_ref[...]
    w_hi = w.astype(BF16)
    w_lo = (w - w_hi.astype(F32)).astype(BF16)
    sc_hi = sc.astype(BF16)
    sc_lo = (sc - sc_hi.astype(F32)).astype(BF16)
    acc = jnp.dot(sc_hi, w_hi, preferred_element_type=F32)
    acc = acc + jnp.dot(sc_lo, w_hi, preferred_element_type=F32)
    acc = acc + jnp.dot(sc_hi, w_lo, preferred_element_type=F32)
    o_ref[...] = (acc + b_ref[...]).reshape(o_ref.shape)


def _ada(c, ada_w, ada_b):
    bsz, d = c.shape
    n = ada_w.shape[1]
    bn = 1024
    return pl.pallas_call(
        _ada_kernel,
        grid=(n // bn,),
        in_specs=[pl.BlockSpec((bsz, d), lambda j: (0, 0)),
                  pl.BlockSpec((d, bn), lambda j: (0, j)),
                  pl.BlockSpec((1, bn), lambda j: (0, j))],
        out_specs=pl.BlockSpec((bsz, 1, bn), lambda j: (0, 0, j)),
        out_shape=jax.ShapeDtypeStruct((bsz, 1, n), F32),
        name="ada_mod",
    )(c, ada_w, ada_b.reshape(1, n))


def _inproj_rglru_kernel(x_ref, mod_ref, n1g_ref, win_ref, cw_ref, cb_ref, wa_ref, wx_ref,
                         ba_ref, bx_ref, lam_ref, *rest, d_model, d_rnn, d_att, conv_w, n_cast):
    cast_in = rest[:n_cast]
    qkv_ref, yr_ref = rest[n_cast:n_cast + 2]
    cast_out = rest[n_cast + 2:2 * n_cast + 2]
    tail_s, hc, win_s, wg_s = rest[2 * n_cast + 2:]
    i = pl.program_id(0)

    for src, dst in zip(cast_in, cast_out):
        dst[...] = src[...].astype(BF16)
    halo = (conv_w - 1) * SUBLANES
    prow = RT * SUBLANES
    half = d_rnn // 2

    @pl.when(i == 0)
    def _():
        tail_s[...] = jnp.zeros_like(tail_s)
        hc[...] = jnp.zeros_like(hc)
        for c0 in range(0, win_ref.shape[1], PW):
            win_s[:, c0:c0 + PW] = win_ref[:, c0:c0 + PW].astype(BF16)
        wg_s[...] = jnp.zeros_like(wg_s)
        nb, bw = wa_ref.shape[0], wa_ref.shape[1]
        per_half = nb // 2
        for n in range(nb):
            hf, k = divmod(n, per_half)
            rs = slice(k * bw, (k + 1) * bw)
            wg_s[hf, rs, k * bw:(k + 1) * bw] = wa_ref[n].astype(BF16)
            wg_s[hf, rs, half + k * bw:half + (k + 1) * bw] = wx_ref[n].astype(BF16)

    gain1 = n1g_ref[...] * (1.0 + mod_ref[:, :, d_model:2 * d_model])
    sh1 = mod_ref[:, :, 0:d_model]
    trow = TT * SUBLANES
    n_rec = TT // RT

    def normed_lhs(t):
        x = x_ref[:, t * TT:(t + 1) * TT, :]
        ms = jnp.mean(x * x, axis=-1, keepdims=True)
        h = x * lax.rsqrt(ms + EPS) * gain1 + sh1
        return h.reshape(trow, d_model).astype(BF16)

    def project_cols(hb, lo, hi):
        part = jnp.dot(hb, win_s[:, lo:hi], preferred_element_type=F32)
        return part.reshape(SUBLANES, TT, hi - lo)

    def branch(hb, j):
        return [project_cols(hb, j * d_rnn + p, j * d_rnn + p + PW) for p in range(0, d_rnn, PW)]

    def emit_qkv(t, hb, p):
        o_q = 2 * d_rnn
        part = project_cols(hb, o_q + p * PW, o_q + (p + 1) * PW)
        if (p + 1) * PW <= d_att:
            part = part * (LOG2E * HEAD_DIM ** -0.5)
        qkv_ref[:, t * TT:(t + 1) * TT, p * PW:(p + 1) * PW] = part.astype(BF16)

    def recur_piece(t, xr_p, gr_p, c, tail, hcur):
        ts = slice(c * RT, (c + 1) * RT)
        xr = jnp.concatenate([jnp.swapaxes(p[:, ts, :], 0, 1).reshape(prow, PW) for p in xr_p], axis=1)
        gr = jnp.concatenate([jnp.swapaxes(p[:, ts, :], 0, 1).reshape(prow, PW) for p in gr_p], axis=1)
        ext = jnp.concatenate([tail, xr], axis=0)
        xc = cb_ref[...]
        for k in range(conv_w):
            xc = xc + ext[k * SUBLANES:k * SUBLANES + prow, :] * cw_ref[k:k + 1, :]
        xcb = xc.astype(BF16)
        g0 = jnp.dot(xcb[:, 0:half], wg_s[0], preferred_element_type=F32)
        g1 = jnp.dot(xcb[:, half:d_rnn], wg_s[1], preferred_element_type=F32)
        r = jax.nn.sigmoid(jnp.concatenate([g0[:, 0:half], g1[:, 0:half]], axis=1) + ba_ref[...])
        ig = jax.nn.sigmoid(jnp.concatenate([g0[:, half:], g1[:, half:]], axis=1) + bx_ref[...])
        lam = lam_ref[...]
        log_sig = jnp.minimum(lam, 0.0) - jnp.log1p(jnp.exp(-jnp.abs(lam)))
        log_a = RG_C * r * log_sig
        a = jnp.exp(log_a)
        w = jnp.tanh(-log_a) * (a * a + 1.0)
        mult = jnp.where(w > 0.0, w * lax.rsqrt(w), 0.0)
        a3 = a.reshape(RT, SUBLANES, d_rnn)
        b3 = (mult * (ig * xc)).reshape(RT, SUBLANES, d_rnn)
        hs = []
        for s in range(RT):
            hcur = a3[s] * hcur + b3[s]
            hs.append(hcur)
        yr = jnp.stack(hs, axis=0) * jax.nn.gelu(gr).reshape(RT, SUBLANES, d_rnn)
        yr_ref[:, t * TT + c * RT:t * TT + (c + 1) * RT, :] = jnp.swapaxes(yr, 0, 1).astype(BF16)
        return ext[prow:prow + halo, :], hcur

    n_tiles = T1 // TT
    n_qkv = (3 * d_att) // PW
    tail, hcur = tail_s[...], hc[...]
    hb = normed_lhs(0)
    xr_p, gr_p = branch(hb, 0), branch(hb, 1)
    for t in range(n_tiles):
        nxt = t + 1 < n_tiles
        ahead = {}
        for c in range(max(n_rec, n_qkv)):
            if c < n_qkv:
                emit_qkv(t, hb, c)
            if c < n_rec:
                tail, hcur = recur_piece(t, xr_p, gr_p, c, tail, hcur)
            if nxt and c == n_rec - 3:
                ahead["hb"] = normed_lhs(t + 1)
            if nxt and c == n_rec - 2:
                ahead["xr"] = branch(ahead["hb"], 0)
            if nxt and c == n_rec - 1:
                ahead["gr"] = branch(ahead["hb"], 1)
        if nxt:
            hb, xr_p, gr_p = ahead["hb"], ahead["xr"], ahead["gr"]
    tail_s[...] = tail
    hc[...] = hcur


def _inproj_rglru(x, mod, n1g, w_in, conv_w, conv_b, wa, wx, ba, bx, lam, later_weights):
    bsz, seq, d = x.shape
    assert bsz == SUBLANES and seq % T1 == 0
    d_in = w_in.shape[1]
    d_rnn = conv_w.shape[1]
    d_att = (d_in - 2 * d_rnn) // 3
    assert d_att == d_rnn and T1 % TT == 0 and TT % RT == 0
    cw = conv_w.shape[0]
    const = lambda i: (0, 0)
    n_steps = seq // T1
    assert all(w.shape[0] % (2 * SUBLANES * n_steps) == 0 for w in later_weights)
    cast_specs = [pl.BlockSpec((w.shape[0] // n_steps, w.shape[1]), lambda i: (i, 0))
                  for w in later_weights]
    kern = functools.partial(_inproj_rglru_kernel, d_model=d, d_rnn=d_rnn, d_att=d_att, conv_w=cw,
                             n_cast=len(later_weights))
    outs = pl.pallas_call(
        kern,
        grid=(n_steps,),
        in_specs=[
            pl.BlockSpec((bsz, T1, d), lambda i: (0, i, 0)),
            _single(mod.shape, lambda i: (0, 0, 0)),
            _single((1, d), const),
            _single(w_in.shape, const),
            _single(conv_w.shape, const),
            _single((1, d_rnn), const),
            _single(wa.shape, lambda i: (0, 0, 0)),
            _single(wx.shape, lambda i: (0, 0, 0)),
            _single((1, d_rnn), const),
            _single((1, d_rnn), const),
            _single((1, d_rnn), const),
        ] + cast_specs,
        out_specs=[pl.BlockSpec((bsz, T1, 3 * d_att), lambda i: (0, i, 0)),
                   pl.BlockSpec((bsz, T1, d_rnn), lambda i: (0, i, 0))] + cast_specs,
        out_shape=[jax.ShapeDtypeStruct((bsz, seq, 3 * d_att), BF16),
                   jax.ShapeDtypeStruct((bsz, seq, d_rnn), BF16)]
        + [jax.ShapeDtypeStruct(w.shape, BF16) for w in later_weights],
        scratch_shapes=[
            pltpu.VMEM(((cw - 1) * SUBLANES, d_rnn), F32),
            pltpu.VMEM((SUBLANES, d_rnn), F32),
            pltpu.VMEM(w_in.shape, BF16),
            pltpu.VMEM((2, d_rnn // 2, d_rnn), BF16),
        ],
        compiler_params=pltpu.CompilerParams(dimension_semantics=("arbitrary",),
                                             vmem_limit_bytes=VMEM_LIMIT_BYTES),
        name="inproj_rglru",
    )(x, mod, n1g.reshape(1, d), w_in, conv_w, conv_b.reshape(1, d_rnn), wa, wx,
      ba.reshape(1, d_rnn), bx.reshape(1, d_rnn), lam.reshape(1, d_rnn), *later_weights)
    return outs[0], outs[1], outs[2:]


def _build_bias_tables(rbp_ref, bias_s, n_heads):
    r_io = lax.broadcasted_iota(jnp.int32, (RBPAD, GPAD), 0)
    j_io = lax.broadcasted_iota(jnp.int32, (RBPAD, GPAD), 1)
    idx = jnp.clip(HIST + CHUNK - 1 - j_io, -REL_CLIP, REL_CLIP) + REL_CLIP
    sel = (r_io == idx).astype(F32)
    g = jnp.dot(rbp_ref[...], sel, precision=lax.Precision.HIGHEST, preferred_element_type=F32)
    g = (g - g[:, 0:1]) * LOG2E
    col = lax.broadcasted_iota(jnp.int32, (CHUNK, WIN), 1)
    band = (LOOKBACK + 1) * CHUNK
    for h in range(n_heads):
        gh = jnp.broadcast_to(g[h:h + 1, :], (CHUNK, GPAD))
        even = pltpu.roll(gh, GPAD - (CHUNK - 1), 1, stride=1, stride_axis=0)[:, 0:WIN]
        odd = pltpu.roll(gh, 1, 1, stride=1, stride_axis=0)[:, 0:WIN]
        bias_s[h, 0] = jnp.where(col < band, even, NEG_INF)
        bias_s[h, 1] = jnp.where(col >= CHUNK, odd, NEG_INF)


ZERO_BIAS_COLS = ((0, HIST - REL_CLIP), (LANES, HIST - REL_CLIP))


def _attn_tile(q_ref, k_refs, v_refs, o_ref, bias_s, first_valid, n_heads):
    lane = lax.broadcasted_iota(jnp.int32, (1, LANES), 1)
    pad = jnp.zeros((CHUNK, NBUF - WIN), BF16)
    n_kt = len(k_refs)

    def masked_q(h):
        pair, hh = divmod(h, HEADS_PER_VREG)
        qp = q_ref[:, pair * LANES:(pair + 1) * LANES]
        in_head = jnp.logical_and(lane >= hh * HEAD_DIM, lane < (hh + 1) * HEAD_DIM)
        return jnp.where(in_head, qp, jnp.zeros_like(qp))

    def score_piece(h, qm, j):
        pair = h // HEADS_PER_VREG
        return lax.dot_general(qm, k_refs[j][:, pair * LANES:(pair + 1) * LANES],
                               (((1,), (1,)), ((), ())), preferred_element_type=F32)

    def softmax_piece(h, s_all, c):
        w0 = (c // 2) * LANES
        raw = s_all[c * CHUNK:(c + 1) * CHUNK, w0:w0 + WIN]
        z0, z1 = ZERO_BIAS_COLS[c % 2]
        parts = [raw[:, z0:z1], raw[:, z1:WIN] + bias_s[h, c % 2, :, z1:WIN]]
        if z0:
            parts.insert(0, raw[:, 0:z0] + bias_s[h, c % 2, :, 0:z0])
        s = jnp.concatenate(parts, axis=1)
        if first_valid is not None:
            col = lax.broadcasted_iota(jnp.int32, (1, WIN), 1) + w0
            s = jnp.where(col >= first_valid, s, NEG_INF)
        m = jnp.max(s, axis=-1, keepdims=True)
        p = jnp.exp2(s - m)
        inv = 1.0 / jnp.sum(p, axis=-1, keepdims=True)
        pb = p.astype(BF16)
        return inv, jnp.concatenate([pb, pad] if w0 == 0 else [pad, pb], axis=1)

    def value_piece(h, pmat, j):
        pair = h // HEADS_PER_VREG
        return jnp.dot(pmat[:, j * TQ:(j + 1) * TQ], v_refs[j][:, pair * LANES:(pair + 1) * LANES],
                       preferred_element_type=F32)

    def finish(h, o, inv, o_prev):
        pair, hh = divmod(h, HEADS_PER_VREG)
        o = o * inv
        if hh == 0:
            return o
        in_head = jnp.logical_and(lane >= hh * HEAD_DIM, lane < (hh + 1) * HEAD_DIM)
        o_ref[:, pair * LANES:(pair + 1) * LANES] = jnp.where(in_head, o, o_prev).astype(BF16)
        return None

    assert n_kt <= CHUNKS_PER_TILE and HEADS_PER_VREG == 2
    qm = masked_q(0)
    s_next = jnp.concatenate([score_piece(0, qm, j) for j in range(n_kt)], axis=1)
    o_pair = None
    prev = None
    for h in range(n_heads + 1):
        cur = h < n_heads
        nxt = h + 1 < n_heads
        s_cur = s_next
        if nxt:
            qm = masked_q(h + 1)
        pieces, invs, prow, o_acc = [], [], [], None
        for c in range(CHUNKS_PER_TILE):
            if nxt and c < n_kt:
                pieces.append(score_piece(h + 1, qm, c))
            if prev is not None and c < n_kt:
                part = value_piece(prev[0], prev[1], c)
                o_acc = part if o_acc is None else o_acc + part
            if cur:
                inv, pr = softmax_piece(h, s_cur, c)
                invs.append(inv)
                prow.append(pr)
        if prev is not None:
            o_pair = finish(prev[0], o_acc, prev[2], o_pair)
        if nxt:
            s_next = jnp.concatenate(pieces, axis=1)
        prev = (h, jnp.concatenate(prow, axis=0), jnp.concatenate(invs, axis=0)) if cur else None


def _attn_kernel(*refs, n_heads):
    n_win = NBUF // TQ
    q_ref = refs[0]
    k_refs = refs[1:1 + n_win]
    v_refs = refs[1 + n_win:1 + 2 * n_win]
    rbp_ref, o_ref, bias_s = refs[1 + 2 * n_win:]
    b = pl.program_id(0)
    t = pl.program_id(1)

    @pl.when(jnp.logical_and(b == 0, t == 0))
    def _():
        _build_bias_tables(rbp_ref, bias_s, n_heads)

    @pl.when(t < n_win - 1)
    def _():
        first_valid = TQ * (n_win - 1 - t)
        _attn_tile(q_ref, k_refs, v_refs, o_ref, bias_s, first_valid, n_heads)

    @pl.when(t >= n_win - 1)
    def _():
        _attn_tile(q_ref, k_refs, v_refs, o_ref, bias_s, None, n_heads)


def _attention(qkv, rbp):
    bsz, seq, d3 = qkv.shape
    d_att = d3 // 3
    n_heads = d_att // HEAD_DIM
    n_win = NBUF // TQ
    assert TQ == MXU_DIM and seq % TQ == 0 and NBUF % TQ == 0 and n_heads % HEADS_PER_VREG == 0
    kern = functools.partial(_attn_kernel, n_heads=n_heads)

    def window(col, j):
        return pl.BlockSpec((None, TQ, d_att),
                            lambda b, t: (b, jnp.maximum(t - (n_win - 1 - j), 0), col))

    return pl.pallas_call(
        kern,
        grid=(bsz, seq // TQ),
        in_specs=([pl.BlockSpec((None, TQ, d_att), lambda b, t: (b, t, 0))]
                  + [window(1, j) for j in range(n_win)]
                  + [window(2, j) for j in range(n_win)]
                  + [_single(rbp.shape, lambda b, t: (0, 0))]),
        out_specs=pl.BlockSpec((None, TQ, d_att), lambda b, t: (b, t, 0)),
        out_shape=jax.ShapeDtypeStruct((bsz, seq, d_att), BF16),
        scratch_shapes=[pltpu.VMEM((n_heads, 2, CHUNK, WIN), F32)],
        compiler_params=pltpu.CompilerParams(dimension_semantics=("arbitrary", "arbitrary"),
                                             vmem_limit_bytes=VMEM_LIMIT_BYTES),
        name="chunk_attn",
    )(*([qkv] * (1 + 2 * n_win)), rbp)


def _outproj_ffn_kernel(x_ref, yr_ref, ya_ref, mod_ref, n2g_ref, fg_ref, wo_ref,
                        wu_ref, fcw_ref, fcb_ref, wd_ref, o_ref, halo_s,
                        *, d_model, d_ff, conv_w):
    i = pl.program_id(0)
    rows = TT * SUBLANES
    halo = (conv_w - 1) * SUBLANES
    n_hp = TT // HT

    @pl.when(i == 0)
    def _():
        halo_s[...] = jnp.zeros_like(halo_s)

    g1 = mod_ref[:, :, 2 * d_model:3 * d_model]
    sh2 = mod_ref[:, :, 3 * d_model:4 * d_model]
    gain2 = n2g_ref[...] * (1.0 + mod_ref[:, :, 4 * d_model:5 * d_model])
    g2 = mod_ref[:, :, 5 * d_model:6 * d_model]

    def out_proj(t):
        ts = slice(t * TT, (t + 1) * TT)
        yr = yr_ref[:, ts, :].reshape(rows, yr_ref.shape[2])
        ya = ya_ref[:, ts, :].reshape(rows, ya_ref.shape[2])
        d_rnn = yr_ref.shape[2]
        y = (jnp.dot(yr, wo_ref[0:d_rnn, :], preferred_element_type=F32)
             + jnp.dot(ya, wo_ref[d_rnn:, :], preferred_element_type=F32))
        return y.reshape(SUBLANES, TT, d_model)

    def head_piece(t, c, y3):
        cs = slice(c * HT, (c + 1) * HT)
        x1 = x_ref[:, t * TT + c * HT:t * TT + (c + 1) * HT, :] + g1 * y3[:, cs, :]
        ms = jnp.mean(x1 * x1, axis=-1, keepdims=True)
        hn = x1 * lax.rsqrt(ms + EPS) * gain2 + sh2
        return x1, jnp.swapaxes(hn, 0, 1).reshape(HT * SUBLANES, d_model).astype(BF16)

    def tail_piece(t, c, z, x1):
        zc = z[c * HT * SUBLANES:(c + 1) * HT * SUBLANES, :].reshape(HT, SUBLANES, d_model)
        x2 = x1 + g2 * jnp.swapaxes(zc, 0, 1)
        ms2 = jnp.mean(x2 * x2, axis=-1, keepdims=True)
        o_ref[:, t * TT + c * HT:t * TT + (c + 1) * HT, :] = x2 * lax.rsqrt(ms2 + EPS) * fg_ref[...]

    def up_piece(hb, cols):
        return jnp.dot(hb, wu_ref[:, cols], preferred_element_type=F32)

    def conv_piece(up, tail, cols):
        ext = jnp.concatenate([tail, up], axis=0)
        u = fcb_ref[:, cols]
        for k in range(conv_w):
            u = u + ext[k * SUBLANES:k * SUBLANES + rows, :] * fcw_ref[k:k + 1, cols]
        return u, ext[rows:rows + halo, :]

    def ffn(hb, tails, side):
        side = list(side)
        z = None
        new_tails = []
        for j, (c0, w) in enumerate(FF_SPLITS):
            cg = slice(c0, c0 + w)
            cv = slice(d_ff + c0, d_ff + c0 + w)
            up_g = up_piece(hb, cg)
            if side:
                side.pop(0)()
            up_v = up_piece(hb, cv)
            if side:
                side.pop(0)()
            ug, tg = conv_piece(up_g, tails[j][0], cg)
            uv, tv = conv_piece(up_v, tails[j][1], cv)
            new_tails.append((tg, tv))
            act = (ug * jax.nn.sigmoid(ug) * uv).astype(BF16)
            zc = jnp.dot(act, wd_ref[c0:c0 + w, :], preferred_element_type=F32)
            z = zc if z is None else z + zc
        assert not side
        return z, new_tails

    tails = [(halo_s[:, c0:c0 + w], halo_s[:, d_ff + c0:d_ff + c0 + w]) for c0, w in FF_SPLITS]
    y3 = out_proj(0)
    hp = [head_piece(0, c, y3) for c in range(n_hp)]
    x1_0 = [p[0] for p in hp]
    hb_0 = jnp.concatenate([p[1] for p in hp], axis=0)

    state = {}

    def side_out_proj():
        state["y3"] = out_proj(1)

    def side_head(c):
        def run():
            state.setdefault("hp", []).append(head_piece(1, c, state["y3"]))
        return run

    z_0, tails = ffn(hb_0, tails, [side_out_proj] + [side_head(c) for c in range(n_hp)])
    x1_1 = [p[0] for p in state["hp"]]
    hb_1 = jnp.concatenate([p[1] for p in state["hp"]], axis=0)

    def side_tail(c):
        return lambda: tail_piece(0, c, z_0, x1_0[c])

    z_1, tails = ffn(hb_1, tails, [side_tail(c) for c in range(n_hp)])
    for c in range(n_hp):
        tail_piece(1, c, z_1, x1_1[c])
    for (c0, w), (tg, tv) in zip(FF_SPLITS, tails):
        halo_s[:, c0:c0 + w] = tg
        halo_s[:, d_ff + c0:d_ff + c0 + w] = tv


def _outproj_ffn(x, yr, ya, mod, n2g, fg, w_out_b, w_up_b, fcw, fcb, w_down_b):
    bsz, seq, d = x.shape
    d_ff = w_down_b.shape[0]
    cw = fcw.shape[0]
    d_rnn = yr.shape[2]
    d_att = ya.shape[2]
    step_t = K3_TILES * TT
    assert K3_TILES == 2 and sum(w for _, w in FF_SPLITS) == d_ff
    assert seq % step_t == 0 and TT % HT == 0
    const = lambda i: (0, 0)
    kern = functools.partial(_outproj_ffn_kernel, d_model=d, d_ff=d_ff, conv_w=cw)
    return pl.pallas_call(
        kern,
        grid=(seq // step_t,),
        in_specs=[
            pl.BlockSpec((bsz, step_t, d), lambda i: (0, i, 0)),
            pl.BlockSpec((bsz, step_t, d_rnn), lambda i: (0, i, 0)),
            pl.BlockSpec((bsz, step_t, d_att), lambda i: (0, i, 0)),
            _single(mod.shape, lambda i: (0, 0, 0)),
            _single((1, d), const),
            _single((1, d), const),
            _single(w_out_b.shape, const),
            _single(w_up_b.shape, const),
            _single(fcw.shape, const),
            _single((1, 2 * d_ff), const),
            _single(w_down_b.shape, const),
        ],
        out_specs=pl.BlockSpec((bsz, step_t, d), lambda i: (0, i, 0)),
        out_shape=jax.ShapeDtypeStruct((bsz, seq, d), F32),
        scratch_shapes=[pltpu.VMEM(((cw - 1) * SUBLANES, 2 * d_ff), F32)],
        compiler_params=pltpu.CompilerParams(dimension_semantics=("arbitrary",),
                                             vmem_limit_bytes=VMEM_LIMIT_BYTES),
        name="outproj_ffn",
    )(x, yr, ya, mod, n2g.reshape(1, d), fg.reshape(1, d), w_out_b, w_up_b, fcw,
      fcb.reshape(1, 2 * d_ff), w_down_b)


def kernel(x, c, ada_w, ada_b, norm1_g, w_in, rnn_conv_w, rnn_conv_b, rg_wa, rg_ba, rg_wx, rg_bx, rg_lambda, rel_bias, w_out, norm2_g, w_up, ffn_conv_w, ffn_conv_b, w_down, final_g):
    assert ada_w.shape[0] == 1
    l = 0
    mod = _ada(c, ada_w[l], ada_b[l])
    qkv, yr, (w_out_b, w_up_b, w_down_b) = _inproj_rglru(
        x, mod, norm1_g[l], w_in[l], rnn_conv_w[l], rnn_conv_b[l], rg_wa[l], rg_wx[l],
        rg_ba[l], rg_bx[l], rg_lambda[l], (w_out[l], w_up[l], w_down[l]))
    rbp = jnp.pad(rel_bias[l], ((0, 0), (0, RBPAD - rel_bias.shape[2])))
    ya = _attention(qkv, rbp)
    return _outproj_ffn(x, yr, ya, mod, norm2_g[l], final_g, w_out_b, w_up_b,
                        ffn_conv_w[l], ffn_conv_b[l], w_down_b)
```

```python
import functools

import jax
import jax.numpy as jnp
from jax import lax
from jax.experimental import pallas as pl
from jax.experimental.pallas import tpu as pltpu

F32 = jnp.float32
BF16 = jnp.bfloat16

CHUNK = 64
LOOKBACK = 8
REL_CLIP = 128
HEAD_DIM = 64
RG_C = 8.0
EPS = 1e-6
NEG_INF = -1e30
LOG2E = 1.4426950408889634

SUBLANES = 8
LANES = 128
MXU_DIM = 256
VMEM_LIMIT_BYTES = 56 * 1024 * 1024

TT = 64
RT = 8
PW = 256
T1 = 128
CHUNKS_PER_TILE = 4
TQ = CHUNKS_PER_TILE * CHUNK
HEADS_PER_VREG = LANES // HEAD_DIM
HIST = LOOKBACK * CHUNK
NBUF = HIST + TQ
WIN = (LOOKBACK + 2) * CHUNK
GPAD = 768
RBPAD = 384
FF_SPLITS = ((0, 1024), (1024, 1024), (2048, 768))
K3_TILES = 2
HT = 16


def _single(block_shape, index_map):
    return pl.BlockSpec(block_shape, index_map, pipeline_mode=pl.Buffered(1))


def _ada_kernel(c_ref, w_ref, b_ref, o_ref):
    c = c_ref[...]
    sc = c * jax.nn.sigmoid(c)
    w = w_ref[...]
    w_hi = w.astype(BF16)
    w_lo = (w - w_hi.astype(F32)).astype(BF16)
    sc_hi = sc.astype(BF16)
    sc_lo = (sc - sc_hi.astype(F32)).astype(BF16)
    acc = jnp.dot(sc_hi, w_hi, preferred_element_type=F32)
    acc = acc + jnp.dot(sc_lo, w_hi, preferred_element_type=F32)
    acc = acc + jnp.dot(sc_hi, w_lo, preferred_element_type=F32)
    o_ref[...] = (acc + b_ref[...]).reshape(o_ref.shape)


def _ada(c, ada_w, ada_b):
    bsz, d = c.shape
    n = ada_w.shape[1]
    bn = 1024
    return pl.pallas_call(
        _ada_kernel,
        grid=(n // bn,),
        in_specs=[pl.BlockSpec((bsz, d), lambda j: (0, 0)),
                  pl.BlockSpec((d, bn), lambda j: (0, j)),
                  pl.BlockSpec((1, bn), lambda j: (0, j))],
        out_specs=pl.BlockSpec((bsz, 1, bn), lambda j: (0, 0, j)),
        out_shape=jax.ShapeDtypeStruct((bsz, 1, n), F32),
        name="ada_mod",
    )(c, ada_w, ada_b.reshape(1, n))


def _inproj_rglru_kernel(x_ref, mod_ref, n1g_ref, win_ref, cw_ref, cb_ref, wa_ref, wx_ref,
                         ba_ref, bx_ref, lam_ref, *rest, d_model, d_rnn, d_att, conv_w, n_cast):
    cast_in = rest[:n_cast]
    qkv_ref, yr_ref = rest[n_cast:n_cast + 2]
    cast_out = rest[n_cast + 2:2 * n_cast + 2]
    tail_s, hc, win_s, wg_s = rest[2 * n_cast + 2:]
    i = pl.program_id(0)

    for src, dst in zip(cast_in, cast_out):
        dst[...] = src[...].astype(BF16)
    halo = (conv_w - 1) * SUBLANES
    prow = RT * SUBLANES
    half = d_rnn // 2

    @pl.when(i == 0)
    def _():
        tail_s[...] = jnp.zeros_like(tail_s)
        hc[...] = jnp.zeros_like(hc)
        for c0 in range(0, win_ref.shape[1], PW):
            win_s[:, c0:c0 + PW] = win_ref[:, c0:c0 + PW].astype(BF16)
        wg_s[...] = jnp.zeros_like(wg_s)
        nb, bw = wa_ref.shape[0], wa_ref.shape[1]
        per_half = nb // 2
        for n in range(nb):
            hf, k = divmod(n, per_half)
            rs = slice(k * bw, (k + 1) * bw)
            wg_s[hf, rs, k * bw:(k + 1) * bw] = wa_ref[n].astype(BF16)
            wg_s[hf, rs, half + k * bw:half + (k + 1) * bw] = wx_ref[n].astype(BF16)

    gain1 = n1g_ref[...] * (1.0 + mod_ref[:, :, d_model:2 * d_model])
    sh1 = mod_ref[:, :, 0:d_model]
    trow = TT * SUBLANES
    n_rec = TT // RT

    def normed_lhs(t):
        x = x_ref[:, t * TT:(t + 1) * TT, :]
        ms = jnp.mean(x * x, axis=-1, keepdims=True)
        h = x * lax.rsqrt(ms + EPS) * gain1 + sh1
        return h.reshape(trow, d_model).astype(BF16)

    def project_cols(hb, lo, hi):
        part = jnp.dot(hb, win_s[:, lo:hi], preferred_element_type=F32)
        return part.reshape(SUBLANES, TT, hi - lo)

    def branch(hb, j):
        return [project_cols(hb, j * d_rnn + p, j * d_rnn + p + PW) for p in range(0, d_rnn, PW)]

    def emit_qkv(t, hb, p):
        o_q = 2 * d_rnn
        part = project_cols(hb, o_q + p * PW, o_q + (p + 1) * PW)
        if (p + 1) * PW <= d_att:
            part = part * (LOG2E * HEAD_DIM ** -0.5)
        qkv_ref[:, t * TT:(t + 1) * TT, p * PW:(p + 1) * PW] = part.astype(BF16)

    def recur_piece(t, xr_p, gr_p, c, tail, hcur):
        ts = slice(c * RT, (c + 1) * RT)
        xr = jnp.concatenate([jnp.swapaxes(p[:, ts, :], 0, 1).reshape(prow, PW) for p in xr_p], axis=1)
        gr = jnp.concatenate([jnp.swapaxes(p[:, ts, :], 0, 1).reshape(prow, PW) for p in gr_p], axis=1)
        ext = jnp.concatenate([tail, xr], axis=0)
        xc = cb_ref[...]
        for k in range(conv_w):
            xc = xc + ext[k * SUBLANES:k * SUBLANES + prow, :] * cw_ref[k:k + 1, :]
        xcb = xc.astype(BF16)
        g0 = jnp.dot(xcb[:, 0:half], wg_s[0], preferred_element_type=F32)
        g1 = jnp.dot(xcb[:, half:d_rnn], wg_s[1], preferred_element_type=F32)
        r = jax.nn.sigmoid(jnp.concatenate([g0[:, 0:half], g1[:, 0:half]], axis=1) + ba_ref[...])
        ig = jax.nn.sigmoid(jnp.concatenate([g0[:, half:], g1[:, half:]], axis=1) + bx_ref[...])
        lam = lam_ref[...]
        log_sig = jnp.minimum(lam, 0.0) - jnp.log1p(jnp.exp(-jnp.abs(lam)))
        log_a = RG_C * r * log_sig
        a = jnp.exp(log_a)
        w = jnp.tanh(-log_a) * (a * a + 1.0)
        mult = jnp.where(w > 0.0, w * lax.rsqrt(w), 0.0)
        a3 = a.reshape(RT, SUBLANES, d_rnn)
        b3 = (mult * (ig * xc)).reshape(RT, SUBLANES, d_rnn)
        hs = []
        for s in range(RT):
            hcur = a3[s] * hcur + b3[s]
            hs.append(hcur)
        yr = jnp.stack(hs, axis=0) * jax.nn.gelu(gr).reshape(RT, SUBLANES, d_rnn)
        yr_ref[:, t * TT + c * RT:t * TT + (c + 1) * RT, :] = jnp.swapaxes(yr, 0, 1).astype(BF16)
        return ext[prow:prow + halo, :], hcur

    n_tiles = T1 // TT
    n_qkv = (3 * d_att) // PW
    tail, hcur = tail_s[...], hc[...]
    hb = normed_lhs(0)
    xr_p, gr_p = branch(hb, 0), branch(hb, 1)
    for t in range(n_tiles):
        nxt = t + 1 < n_tiles
        ahead = {}
        for c in range(max(n_rec, n_qkv)):
            if c < n_qkv:
                emit_qkv(t, hb, c)
            if c < n_rec:
                tail, hcur = recur_piece(t, xr_p, gr_p, c, tail, hcur)
            if nxt and c == n_rec - 3:
                ahead["hb"] = normed_lhs(t + 1)
            if nxt and c == n_rec - 2:
                ahead["xr"] = branch(ahead["hb"], 0)
            if nxt and c == n_rec - 1:
                ahead["gr"] = branch(ahead["hb"], 1)
        if nxt:
            hb, xr_p, gr_p = ahead["hb"], ahead["xr"], ahead["gr"]
    tail_s[...] = tail
    hc[...] = hcur


def _inproj_rglru(x, mod, n1g, w_in, conv_w, conv_b, wa, wx, ba, bx, lam, later_weights):
    bsz, seq, d = x.shape
    assert bsz == SUBLANES and seq % T1 == 0
    d_in = w_in.shape[1]
    d_rnn = conv_w.shape[1]
    d_att = (d_in - 2 * d_rnn) // 3
    assert d_att == d_rnn and T1 % TT == 0 and TT % RT == 0
    cw = conv_w.shape[0]
    const = lambda i: (0, 0)
    n_steps = seq // T1
    assert all(w.shape[0] % (2 * SUBLANES * n_steps) == 0 for w in later_weights)
    cast_specs = [pl.BlockSpec((w.shape[0] // n_steps, w.shape[1]), lambda i: (i, 0))
                  for w in later_weights]
    kern = functools.partial(_inproj_rglru_kernel, d_model=d, d_rnn=d_rnn, d_att=d_att, conv_w=cw,
                             n_cast=len(later_weights))
    outs = pl.pallas_call(
        kern,
        grid=(n_steps,),
        in_specs=[
            pl.BlockSpec((bsz, T1, d), lambda i: (0, i, 0)),
            _single(mod.shape, lambda i: (0, 0, 0)),
            _single((1, d), const),
            _single(w_in.shape, const),
            _single(conv_w.shape, const),
            _single((1, d_rnn), const),
            _single(wa.shape, lambda i: (0, 0, 0)),
            _single(wx.shape, lambda i: (0, 0, 0)),
            _single((1, d_rnn), const),
            _single((1, d_rnn), const),
            _single((1, d_rnn), const),
        ] + cast_specs,
        out_specs=[pl.BlockSpec((bsz, T1, 3 * d_att), lambda i: (0, i, 0)),
                   pl.BlockSpec((bsz, T1, d_rnn), lambda i: (0, i, 0))] + cast_specs,
        out_shape=[jax.ShapeDtypeStruct((bsz, seq, 3 * d_att), BF16),
                   jax.ShapeDtypeStruct((bsz, seq, d_rnn), BF16)]
        + [jax.ShapeDtypeStruct(w.shape, BF16) for w in later_weights],
        scratch_shapes=[
            pltpu.VMEM(((cw - 1) * SUBLANES, d_rnn), F32),
            pltpu.VMEM((SUBLANES, d_rnn), F32),
            pltpu.VMEM(w_in.shape, BF16),
            pltpu.VMEM((2, d_rnn // 2, d_rnn), BF16),
        ],
        compiler_params=pltpu.CompilerParams(dimension_semantics=("arbitrary",),
                                             vmem_limit_bytes=VMEM_LIMIT_BYTES),
        name="inproj_rglru",
    )(x, mod, n1g.reshape(1, d), w_in, conv_w, conv_b.reshape(1, d_rnn), wa, wx,
      ba.reshape(1, d_rnn), bx.reshape(1, d_rnn), lam.reshape(1, d_rnn), *later_weights)
    return outs[0], outs[1], outs[2:]


def _build_bias_tables(rbp_ref, bias_s, n_heads):
    r_io = lax.broadcasted_iota(jnp.int32, (RBPAD, GPAD), 0)
    j_io = lax.broadcasted_iota(jnp.int32, (RBPAD, GPAD), 1)
    idx = jnp.clip(HIST + CHUNK - 1 - j_io, -REL_CLIP, REL_CLIP) + REL_CLIP
    sel = (r_io == idx).astype(F32)
    g = jnp.dot(rbp_ref[...], sel, precision=lax.Precision.HIGHEST, preferred_element_type=F32)
    g = (g - g[:, 0:1]) * LOG2E
    col = lax.broadcasted_iota(jnp.int32, (CHUNK, WIN), 1)
    band = (LOOKBACK + 1) * CHUNK
    for h in range(n_heads):
        gh = jnp.broadcast_to(g[h:h + 1, :], (CHUNK, GPAD))
        even = pltpu.roll(gh, GPAD - (CHUNK - 1), 1, stride=1, stride_axis=0)[:, 0:WIN]
        odd = pltpu.roll(gh, 1, 1, stride=1, stride_axis=0)[:, 0:WIN]
        bias_s[h, 0] = jnp.where(col < band, even, NEG_INF)
        bias_s[h, 1] = jnp.where(col >= CHUNK, odd, NEG_INF)


ZERO_BIAS_COLS = ((0, HIST - REL_CLIP), (LANES, HIST - REL_CLIP))


def _attn_tile(q_ref, k_refs, v_refs, o_ref, bias_s, skip, first_valid, n_heads):
    lane = lax.broadcasted_iota(jnp.int32, (1, LANES), 1)
    n_kt = len(k_refs)
    base = skip * TQ
    n_p = n_kt - skip

    def masked_q(h):
        pair, hh = divmod(h, HEADS_PER_VREG)
        qp = q_ref[:, pair * LANES:(pair + 1) * LANES]
        in_head = jnp.logical_and(lane >= hh * HEAD_DIM, lane < (hh + 1) * HEAD_DIM)
        return jnp.where(in_head, qp, jnp.zeros_like(qp))

    def score_piece(h, qm, j):
        pair = h // HEADS_PER_VREG
        return lax.dot_general(qm, k_refs[skip + j][:, pair * LANES:(pair + 1) * LANES],
                               (((1,), (1,)), ((), ())), preferred_element_type=F32)

    def softmax_piece(h, s_all, c):
        w0 = (c // 2) * LANES
        lo = max(base - w0, 0)
        raw = s_all[c * CHUNK:(c + 1) * CHUNK, w0 + lo - base:w0 + WIN - base]
        z0, z1 = ZERO_BIAS_COLS[c % 2]
        z0 = max(z0, lo)
        parts = []
        if z0 > lo:
            parts.append(raw[:, 0:z0 - lo] + bias_s[h, c % 2, :, lo:z0])
        if z1 > z0:
            parts.append(raw[:, z0 - lo:z1 - lo])
        hi0 = max(z1, lo)
        parts.append(raw[:, hi0 - lo:WIN - lo] + bias_s[h, c % 2, :, hi0:WIN])
        s = jnp.concatenate(parts, axis=1)
        if first_valid is not None:
            col = lax.broadcasted_iota(jnp.int32, (1, WIN - lo), 1) + (w0 + lo)
            s = jnp.where(col >= first_valid, s, NEG_INF)
        m = jnp.max(s, axis=-1, keepdims=True)
        p = jnp.exp2(s - m)
        inv = 1.0 / jnp.sum(p, axis=-1, keepdims=True)
        row = [p.astype(BF16)]
        if w0 + lo > base:
            row.insert(0, jnp.zeros((CHUNK, w0 + lo - base), BF16))
        if w0 + WIN < NBUF:
            row.append(jnp.zeros((CHUNK, NBUF - w0 - WIN), BF16))
        return inv, jnp.concatenate(row, axis=1)

    def value_piece(h, pmat, j):
        pair = h // HEADS_PER_VREG
        return jnp.dot(pmat[:, j * TQ:(j + 1) * TQ],
                       v_refs[skip + j][:, pair * LANES:(pair + 1) * LANES],
                       preferred_element_type=F32)

    def finish(h, o, inv, o_prev):
        pair, hh = divmod(h, HEADS_PER_VREG)
        o = o * inv
        if hh == 0:
            return o
        in_head = jnp.logical_and(lane >= hh * HEAD_DIM, lane < (hh + 1) * HEAD_DIM)
        o_ref[:, pair * LANES:(pair + 1) * LANES] = jnp.where(in_head, o, o_prev).astype(BF16)
        return None

    assert 0 < n_p <= CHUNKS_PER_TILE and HEADS_PER_VREG == 2
    qm = masked_q(0)
    s_next = jnp.concatenate([score_piece(0, qm, j) for j in range(n_p)], axis=1)
    o_pair = None
    prev = None
    for h in range(n_heads + 1):
        cur = h < n_heads
        nxt = h + 1 < n_heads
        s_cur = s_next
        if nxt:
            qm = masked_q(h + 1)
        pieces, invs, prow, o_acc = [], [], [], None
        for c in range(CHUNKS_PER_TILE):
            if nxt and c < n_p:
                pieces.append(score_piece(h + 1, qm, c))
            if prev is not None and c < n_p:
                part = value_piece(prev[0], prev[1], c)
                o_acc = part if o_acc is None else o_acc + part
            if cur:
                inv, pr = softmax_piece(h, s_cur, c)
                invs.append(inv)
                prow.append(pr)
        if prev is not None:
            o_pair = finish(prev[0], o_acc, prev[2], o_pair)
        if nxt:
            s_next = jnp.concatenate(pieces, axis=1)
        prev = (h, jnp.concatenate(prow, axis=0), jnp.concatenate(invs, axis=0)) if cur else None


def _attn_kernel(*refs, n_heads):
    n_win = NBUF // TQ
    q_ref = refs[0]
    k_refs = refs[1:1 + n_win]
    v_refs = refs[1 + n_win:1 + 2 * n_win]
    rbp_ref, o_ref, bias_s = refs[1 + 2 * n_win:]
    b = pl.program_id(0)
    t = pl.program_id(1)

    @pl.when(jnp.logical_and(b == 0, t == 0))
    def _():
        _build_bias_tables(rbp_ref, bias_s, n_heads)

    @pl.when(t == 0)
    def _():
        _attn_tile(q_ref, k_refs, v_refs, o_ref, bias_s, n_win - 1, None, n_heads)

    @pl.when(jnp.logical_and(t > 0, t < n_win - 1))
    def _():
        _attn_tile(q_ref, k_refs, v_refs, o_ref, bias_s, 0, TQ * (n_win - 1 - t), n_heads)

    @pl.when(t >= n_win - 1)
    def _():
        _attn_tile(q_ref, k_refs, v_refs, o_ref, bias_s, 0, None, n_heads)


def _attention(qkv, rbp):
    bsz, seq, d3 = qkv.shape
    d_att = d3 // 3
    n_heads = d_att // HEAD_DIM
    n_win = NBUF // TQ
    assert TQ == MXU_DIM and seq % TQ == 0 and NBUF % TQ == 0 and n_heads % HEADS_PER_VREG == 0
    kern = functools.partial(_attn_kernel, n_heads=n_heads)

    def window(col, j):
        return pl.BlockSpec((None, TQ, d_att),
                            lambda b, t: (b, jnp.maximum(t - (n_win - 1 - j), 0), col))

    return pl.pallas_call(
        kern,
        grid=(bsz, seq // TQ),
        in_specs=([pl.BlockSpec((None, TQ, d_att), lambda b, t: (b, t, 0))]
                  + [window(1, j) for j in range(n_win)]
                  + [window(2, j) for j in range(n_win)]
                  + [_single(rbp.shape, lambda b, t: (0, 0))]),
        out_specs=pl.BlockSpec((None, TQ, d_att), lambda b, t: (b, t, 0)),
        out_shape=jax.ShapeDtypeStruct((bsz, seq, d_att), BF16),
        scratch_shapes=[pltpu.VMEM((n_heads, 2, CHUNK, WIN), F32)],
        compiler_params=pltpu.CompilerParams(dimension_semantics=("arbitrary", "arbitrary"),
                                             vmem_limit_bytes=VMEM_LIMIT_BYTES),
        name="chunk_attn",
    )(*([qkv] * (1 + 2 * n_win)), rbp)


def _outproj_ffn_kernel(x_ref, yr_ref, ya_ref, mod_ref, n2g_ref, fg_ref, wo_ref,
                        wu_ref, fcw_ref, fcb_ref, wd_ref, o_ref, halo_s,
                        *, d_model, d_ff, conv_w):
    i = pl.program_id(0)
    rows = TT * SUBLANES
    halo = (conv_w - 1) * SUBLANES
    n_hp = TT // HT

    @pl.when(i == 0)
    def _():
        halo_s[...] = jnp.zeros_like(halo_s)

    g1 = mod_ref[:, :, 2 * d_model:3 * d_model]
    sh2 = mod_ref[:, :, 3 * d_model:4 * d_model]
    gain2 = n2g_ref[...] * (1.0 + mod_ref[:, :, 4 * d_model:5 * d_model])
    g2 = mod_ref[:, :, 5 * d_model:6 * d_model]

    def out_proj(t):
        ts = slice(t * TT, (t + 1) * TT)
        yr = yr_ref[:, ts, :].reshape(rows, yr_ref.shape[2])
        ya = ya_ref[:, ts, :].reshape(rows, ya_ref.shape[2])
        d_rnn = yr_ref.shape[2]
        y = (jnp.dot(yr, wo_ref[0:d_rnn, :], preferred_element_type=F32)
             + jnp.dot(ya, wo_ref[d_rnn:, :], preferred_element_type=F32))
        return y.reshape(SUBLANES, TT, d_model)

    def head_piece(t, c, y3):
        cs = slice(c * HT, (c + 1) * HT)
        x1 = x_ref[:, t * TT + c * HT:t * TT + (c + 1) * HT, :] + g1 * y3[:, cs, :]
        ms = jnp.mean(x1 * x1, axis=-1, keepdims=True)
        hn = x1 * lax.rsqrt(ms + EPS) * gain2 + sh2
        return x1, jnp.swapaxes(hn, 0, 1).reshape(HT * SUBLANES, d_model).astype(BF16)

    def tail_piece(t, c, z, x1):
        zc = z[c * HT * SUBLANES:(c + 1) * HT * SUBLANES, :].reshape(HT, SUBLANES, d_model)
        x2 = x1 + g2 * jnp.swapaxes(zc, 0, 1)
        ms2 = jnp.mean(x2 * x2, axis=-1, keepdims=True)
        o_ref[:, t * TT + c * HT:t * TT + (c + 1) * HT, :] = x2 * lax.rsqrt(ms2 + EPS) * fg_ref[...]

    def up_piece(hb, cols):
        return jnp.dot(hb, wu_ref[:, cols], preferred_element_type=F32)

    def conv_piece(up, tail, cols):
        ext = jnp.concatenate([tail, up], axis=0)
        u = fcb_ref[:, cols]
        for k in range(conv_w):
            u = u + ext[k * SUBLANES:k * SUBLANES + rows, :] * fcw_ref[k:k + 1, cols]
        return u, ext[rows:rows + halo, :]

    def ffn(hb, tails, side):
        side = list(side)
        z = None
        new_tails = []
        for j, (c0, w) in enumerate(FF_SPLITS):
            cg = slice(c0, c0 + w)
            cv = slice(d_ff + c0, d_ff + c0 + w)
            up_g = up_piece(hb, cg)
            if side:
                side.pop(0)()
            up_v = up_piece(hb, cv)
            if side:
                side.pop(0)()
            ug, tg = conv_piece(up_g, tails[j][0], cg)
            uv, tv = conv_piece(up_v, tails[j][1], cv)
            new_tails.append((tg, tv))
            act = (ug * jax.nn.sigmoid(ug) * uv).astype(BF16)
            zc = jnp.dot(act, wd_ref[c0:c0 + w, :], preferred_element_type=F32)
            z = zc if z is None else z + zc
        assert not side
        return z, new_tails

    tails = [(halo_s[:, c0:c0 + w], halo_s[:, d_ff + c0:d_ff + c0 + w]) for c0, w in FF_SPLITS]
    y3 = out_proj(0)
    hp = [head_piece(0, c, y3) for c in range(n_hp)]
    x1_0 = [p[0] for p in hp]
    hb_0 = jnp.concatenate([p[1] for p in hp], axis=0)

    state = {}

    def side_out_proj():
        state["y3"] = out_proj(1)

    def side_head(c):
        def run():
            state.setdefault("hp", []).append(head_piece(1, c, state["y3"]))
        return run

    z_0, tails = ffn(hb_0, tails, [side_out_proj] + [side_head(c) for c in range(n_hp)])
    x1_1 = [p[0] for p in state["hp"]]
    hb_1 = jnp.concatenate([p[1] for p in state["hp"]], axis=0)

    def side_tail(c):
        return lambda: tail_piece(0, c, z_0, x1_0[c])

    z_1, tails = ffn(hb_1, tails, [side_tail(c) for c in range(n_hp)])
    for c in range(n_hp):
        tail_piece(1, c, z_1, x1_1[c])
    for (c0, w), (tg, tv) in zip(FF_SPLITS, tails):
        halo_s[:, c0:c0 + w] = tg
        halo_s[:, d_ff + c0:d_ff + c0 + w] = tv


def _outproj_ffn(x, yr, ya, mod, n2g, fg, w_out_b, w_up_b, fcw, fcb, w_down_b):
    bsz, seq, d = x.shape
    d_ff = w_down_b.shape[0]
    cw = fcw.shape[0]
    d_rnn = yr.shape[2]
    d_att = ya.shape[2]
    step_t = K3_TILES * TT
    assert K3_TILES == 2 and sum(w for _, w in FF_SPLITS) == d_ff
    assert seq % step_t == 0 and TT % HT == 0
    const = lambda i: (0, 0)
    kern = functools.partial(_outproj_ffn_kernel, d_model=d, d_ff=d_ff, conv_w=cw)
    return pl.pallas_call(
        kern,
        grid=(seq // step_t,),
        in_specs=[
            pl.BlockSpec((bsz, step_t, d), lambda i: (0, i, 0)),
            pl.BlockSpec((bsz, step_t, d_rnn), lambda i: (0, i, 0)),
            pl.BlockSpec((bsz, step_t, d_att), lambda i: (0, i, 0)),
            _single(mod.shape, lambda i: (0, 0, 0)),
            _single((1, d), const),
            _single((1, d), const),
            _single(w_out_b.shape, const),
            _single(w_up_b.shape, const),
            _single(fcw.shape, const),
            _single((1, 2 * d_ff), const),
            _single(w_down_b.shape, const),
        ],
        out_specs=pl.BlockSpec((bsz, step_t, d), lambda i: (0, i, 0)),
        out_shape=jax.ShapeDtypeStruct((bsz, seq, d), F32),
        scratch_shapes=[pltpu.VMEM(((cw - 1) * SUBLANES, 2 * d_ff), F32)],
        compiler_params=pltpu.CompilerParams(dimension_semantics=("arbitrary",),
                                             vmem_limit_bytes=VMEM_LIMIT_BYTES),
        name="outproj_ffn",
    )(x, yr, ya, mod, n2g.reshape(1, d), fg.reshape(1, d), w_out_b, w_up_b, fcw,
      fcb.reshape(1, 2 * d_ff), w_down_b)


def kernel(x, c, ada_w, ada_b, norm1_g, w_in, rnn_conv_w, rnn_conv_b, rg_wa, rg_ba, rg_wx, rg_bx, rg_lambda, rel_bias, w_out, norm2_g, w_up, ffn_conv_w, ffn_conv_b, w_down, final_g):
    assert ada_w.shape[0] == 1
    l = 0
    mod = _ada(c, ada_w[l], ada_b[l])
    qkv, yr, (w_out_b, w_up_b, w_down_b) = _inproj_rglru(
        x, mod, norm1_g[l], w_in[l], rnn_conv_w[l], rnn_conv_b[l], rg_wa[l], rg_wx[l],
        rg_ba[l], rg_bx[l], rg_lambda[l], (w_out[l], w_up[l], w_down[l]))
    rbp = jnp.pad(rel_bias[l], ((0, 0), (0, RBPAD - rel_bias.shape[2])))
    ya = _attention(qkv, rbp)
    return _outproj_ffn(x, yr, ya, mod, norm2_g[l], final_g, w_out_b, w_up_b,
                        ffn_conv_w[l], ffn_conv_b[l], w_down_b)
```

```python
import functools

import jax
import jax.numpy as jnp
from jax import lax
from jax.experimental import pallas as pl
from jax.experimental.pallas import tpu as pltpu

F32 = jnp.float32
BF16 = jnp.bfloat16

CHUNK = 64
LOOKBACK = 8
REL_CLIP = 128
HEAD_DIM = 64
RG_C = 8.0
EPS = 1e-6
NEG_INF = -1e30
LOG2E = 1.4426950408889634

SUBLANES = 8
LANES = 128
MXU_DIM = 256
VMEM_LIMIT_BYTES = 56 * 1024 * 1024

TT = 64
RT = 8
PW = 256
T1 = 128
CHUNKS_PER_TILE = 4
TQ = CHUNKS_PER_TILE * CHUNK
HEADS_PER_VREG = LANES // HEAD_DIM
HIST = LOOKBACK * CHUNK
NBUF = HIST + TQ
WIN = (LOOKBACK + 2) * CHUNK
GPAD = 768
RBPAD = 384
FF_SPLITS = ((0, 1024), (1024, 1024), (2048, 768))
K3_TILES = 2
HT = 16


def _single(block_shape, index_map):
    return pl.BlockSpec(block_shape, index_map, pipeline_mode=pl.Buffered(1))


def _ada_kernel(c_ref, w_ref, b_ref, o_ref):
    c = c_ref[...]
    sc = c * jax.nn.sigmoid(c)
    w = w_ref[...]
    w_hi = w.astype(BF16)
    w_lo = (w - w_hi.astype(F32)).astype(BF16)
    sc_hi = sc.astype(BF16)
    sc_lo = (sc - sc_hi.astype(F32)).astype(BF16)
    acc = jnp.dot(sc_hi, w_hi, preferred_element_type=F32)
    acc = acc + jnp.dot(sc_lo, w_hi, preferred_element_type=F32)
    acc = acc + jnp.dot(sc_hi, w_lo, preferred_element_type=F32)
    o_ref[...] = (acc + b_ref[...]).reshape(o_ref.shape)


def _ada(c, ada_w, ada_b):
    bsz, d = c.shape
    n = ada_w.shape[1]
    bn = 1024
    return pl.pallas_call(
        _ada_kernel,
        grid=(n // bn,),
        in_specs=[pl.BlockSpec((bsz, d), lambda j: (0, 0)),
                  pl.BlockSpec((d, bn), lambda j: (0, j)),
                  pl.BlockSpec((1, bn), lambda j: (0, j))],
        out_specs=pl.BlockSpec((bsz, 1, bn), lambda j: (0, 0, j)),
        out_shape=jax.ShapeDtypeStruct((bsz, 1, n), F32),
        name="ada_mod",
    )(c, ada_w, ada_b.reshape(1, n))


def _inproj_rglru_kernel(x_ref, mod_ref, n1g_ref, win_ref, cw_ref, cb_ref, wa_ref, wx_ref,
                         ba_ref, bx_ref, lam_ref, *rest, d_model, d_rnn, d_att, conv_w, n_cast):
    cast_in = rest[:n_cast]
    qkv_ref, yr_ref = rest[n_cast:n_cast + 2]
    cast_out = rest[n_cast + 2:2 * n_cast + 2]
    tail_s, hc, win_s, wg_s = rest[2 * n_cast + 2:]
    i = pl.program_id(0)

    for src, dst in zip(cast_in, cast_out):
        dst[...] = src[...].astype(BF16)
    halo = (conv_w - 1) * SUBLANES
    prow = RT * SUBLANES
    half = d_rnn // 2

    @pl.when(i == 0)
    def _():
        tail_s[...] = jnp.zeros_like(tail_s)
        hc[...] = jnp.zeros_like(hc)
        for c0 in range(0, win_ref.shape[1], PW):
            wcol = win_ref[:, c0:c0 + PW]
            if 2 * d_rnn <= c0 < 2 * d_rnn + d_att:
                wcol = wcol * (LOG2E * HEAD_DIM ** -0.5)
            win_s[:, c0:c0 + PW] = wcol.astype(BF16)
        wg_s[...] = jnp.zeros_like(wg_s)
        nb, bw = wa_ref.shape[0], wa_ref.shape[1]
        per_half = nb // 2
        for n in range(nb):
            hf, k = divmod(n, per_half)
            rs = slice(k * bw, (k + 1) * bw)
            wg_s[hf, rs, k * bw:(k + 1) * bw] = wa_ref[n].astype(BF16)
            wg_s[hf, rs, half + k * bw:half + (k + 1) * bw] = wx_ref[n].astype(BF16)

    gain1 = n1g_ref[...] * (1.0 + mod_ref[:, :, d_model:2 * d_model])
    lam = lam_ref[...]
    log_sig = jnp.minimum(lam, 0.0) - jnp.log1p(jnp.exp(-jnp.abs(lam)))
    neg_rate = -RG_C * log_sig
    rate2 = -LOG2E * neg_rate
    sh1 = mod_ref[:, :, 0:d_model]
    trow = TT * SUBLANES
    n_rec = TT // RT

    def normed_lhs(t):
        x = x_ref[:, t * TT:(t + 1) * TT, :]
        ms = jnp.mean(x * x, axis=-1, keepdims=True)
        h = x * lax.rsqrt(ms + EPS) * gain1 + sh1
        return h.reshape(trow, d_model).astype(BF16)

    def project_cols(hb, lo, hi):
        part = jnp.dot(hb, win_s[:, lo:hi], preferred_element_type=F32)
        return part.reshape(SUBLANES, TT, hi - lo)

    def branch(hb, j):
        return [project_cols(hb, j * d_rnn + p, j * d_rnn + p + PW) for p in range(0, d_rnn, PW)]

    def emit_qkv(t, hb, p):
        o_q = 2 * d_rnn
        part = project_cols(hb, o_q + p * PW, o_q + (p + 1) * PW)
        qkv_ref[:, t * TT:(t + 1) * TT, p * PW:(p + 1) * PW] = part.astype(BF16)

    def recur_piece(t, xr_p, gr_p, c, tail, hcur):
        ts = slice(c * RT, (c + 1) * RT)
        xr = jnp.concatenate([jnp.swapaxes(p[:, ts, :], 0, 1).reshape(prow, PW) for p in xr_p], axis=1)
        ext = jnp.concatenate([tail, xr], axis=0)
        xc = cb_ref[...]
        for k in range(conv_w):
            xc = xc + ext[k * SUBLANES:k * SUBLANES + prow, :] * cw_ref[k:k + 1, :]
        xcb = xc.astype(BF16)
        g0 = jnp.dot(xcb[:, 0:half], wg_s[0], preferred_element_type=F32)
        g1 = jnp.dot(xcb[:, half:d_rnn], wg_s[1], preferred_element_type=F32)
        r = jax.nn.sigmoid(jnp.concatenate([g0[:, 0:half], g1[:, 0:half]], axis=1) + ba_ref[...])
        ig = jax.nn.sigmoid(jnp.concatenate([g0[:, half:], g1[:, half:]], axis=1) + bx_ref[...])
        a = jnp.exp2(r * rate2)
        w = jnp.tanh(r * neg_rate) * (a * a + 1.0)
        mult = jnp.where(w > 0.0, w * lax.rsqrt(w), 0.0)
        a3 = a.reshape(RT, SUBLANES, d_rnn)
        b3 = (mult * (ig * xc)).reshape(RT, SUBLANES, d_rnn)
        hs = []
        for s in range(RT):
            hcur = a3[s] * hcur + b3[s]
            hs.append(hcur)
        gate = jnp.concatenate([jax.nn.gelu(p[:, ts, :]) for p in gr_p], axis=2)
        yr = jnp.swapaxes(jnp.stack(hs, axis=0), 0, 1) * gate
        yr_ref[:, t * TT + c * RT:t * TT + (c + 1) * RT, :] = yr.astype(BF16)
        return ext[prow:prow + halo, :], hcur

    n_tiles = T1 // TT
    n_qkv = (3 * d_att) // PW
    tail, hcur = tail_s[...], hc[...]
    hb = normed_lhs(0)
    xr_p, gr_p = branch(hb, 0), branch(hb, 1)
    for t in range(n_tiles):
        nxt = t + 1 < n_tiles
        ahead = {}
        for c in range(max(n_rec, n_qkv)):
            if c < n_qkv:
                emit_qkv(t, hb, c)
            if c < n_rec:
                tail, hcur = recur_piece(t, xr_p, gr_p, c, tail, hcur)
            if nxt and c == n_rec - 3:
                ahead["hb"] = normed_lhs(t + 1)
            if nxt and c == n_rec - 2:
                ahead["xr"] = branch(ahead["hb"], 0)
            if nxt and c == n_rec - 1:
                ahead["gr"] = branch(ahead["hb"], 1)
        if nxt:
            hb, xr_p, gr_p = ahead["hb"], ahead["xr"], ahead["gr"]
    tail_s[...] = tail
    hc[...] = hcur


def _inproj_rglru(x, mod, n1g, w_in, conv_w, conv_b, wa, wx, ba, bx, lam, later_weights):
    bsz, seq, d = x.shape
    assert bsz == SUBLANES and seq % T1 == 0
    d_in = w_in.shape[1]
    d_rnn = conv_w.shape[1]
    d_att = (d_in - 2 * d_rnn) // 3
    assert d_att == d_rnn and T1 % TT == 0 and TT % RT == 0
    cw = conv_w.shape[0]
    const = lambda i: (0, 0)
    n_steps = seq // T1
    assert all(w.shape[0] % (2 * SUBLANES * n_steps) == 0 for w in later_weights)
    cast_specs = [pl.BlockSpec((w.shape[0] // n_steps, w.shape[1]), lambda i: (i, 0))
                  for w in later_weights]
    kern = functools.partial(_inproj_rglru_kernel, d_model=d, d_rnn=d_rnn, d_att=d_att, conv_w=cw,
                             n_cast=len(later_weights))
    outs = pl.pallas_call(
        kern,
        grid=(n_steps,),
        in_specs=[
            pl.BlockSpec((bsz, T1, d), lambda i: (0, i, 0)),
            _single(mod.shape, lambda i: (0, 0, 0)),
            _single((1, d), const),
            _single(w_in.shape, const),
            _single(conv_w.shape, const),
            _single((1, d_rnn), const),
            _single(wa.shape, lambda i: (0, 0, 0)),
            _single(wx.shape, lambda i: (0, 0, 0)),
            _single((1, d_rnn), const),
            _single((1, d_rnn), const),
            _single((1, d_rnn), const),
        ] + cast_specs,
        out_specs=[pl.BlockSpec((bsz, T1, 3 * d_att), lambda i: (0, i, 0)),
                   pl.BlockSpec((bsz, T1, d_rnn), lambda i: (0, i, 0))] + cast_specs,
        out_shape=[jax.ShapeDtypeStruct((bsz, seq, 3 * d_att), BF16),
                   jax.ShapeDtypeStruct((bsz, seq, d_rnn), BF16)]
        + [jax.ShapeDtypeStruct(w.shape, BF16) for w in later_weights],
        scratch_shapes=[
            pltpu.VMEM(((cw - 1) * SUBLANES, d_rnn), F32),
            pltpu.VMEM((SUBLANES, d_rnn), F32),
            pltpu.VMEM(w_in.shape, BF16),
            pltpu.VMEM((2, d_rnn // 2, d_rnn), BF16),
        ],
        compiler_params=pltpu.CompilerParams(dimension_semantics=("arbitrary",),
                                             vmem_limit_bytes=VMEM_LIMIT_BYTES),
        name="inproj_rglru",
    )(x, mod, n1g.reshape(1, d), w_in, conv_w, conv_b.reshape(1, d_rnn), wa, wx,
      ba.reshape(1, d_rnn), bx.reshape(1, d_rnn), lam.reshape(1, d_rnn), *later_weights)
    return outs[0], outs[1], outs[2:]


def _build_bias_tables(rbp_ref, bias_s, n_heads):
    r_io = lax.broadcasted_iota(jnp.int32, (RBPAD, GPAD), 0)
    j_io = lax.broadcasted_iota(jnp.int32, (RBPAD, GPAD), 1)
    idx = jnp.clip(HIST + CHUNK - 1 - j_io, -REL_CLIP, REL_CLIP) + REL_CLIP
    sel = (r_io == idx).astype(F32)
    g = jnp.dot(rbp_ref[...], sel, precision=lax.Precision.HIGHEST, preferred_element_type=F32)
    g = (g - g[:, 0:1]) * LOG2E
    col = lax.broadcasted_iota(jnp.int32, (CHUNK, WIN), 1)
    band = (LOOKBACK + 1) * CHUNK
    for h in range(n_heads):
        gh = jnp.broadcast_to(g[h:h + 1, :], (CHUNK, GPAD))
        even = pltpu.roll(gh, GPAD - (CHUNK - 1), 1, stride=1, stride_axis=0)[:, 0:WIN]
        odd = pltpu.roll(gh, 1, 1, stride=1, stride_axis=0)[:, 0:WIN]
        bias_s[h, 0] = jnp.where(col < band, even, NEG_INF)
        bias_s[h, 1] = jnp.where(col >= CHUNK, odd, NEG_INF)


ZERO_BIAS_COLS = ((0, HIST - REL_CLIP), (LANES, HIST - REL_CLIP))


def _attn_tile(q_ref, k_refs, v_refs, o_ref, bias_s, skip, first_valid, n_heads):
    lane = lax.broadcasted_iota(jnp.int32, (1, LANES), 1)
    n_kt = len(k_refs)
    base = skip * TQ
    n_p = n_kt - skip

    def masked_q(h):
        pair, hh = divmod(h, HEADS_PER_VREG)
        qp = q_ref[:, pair * LANES:(pair + 1) * LANES]
        in_head = jnp.logical_and(lane >= hh * HEAD_DIM, lane < (hh + 1) * HEAD_DIM)
        return jnp.where(in_head, qp, jnp.zeros_like(qp))

    def score_piece(h, qm, j):
        pair = h // HEADS_PER_VREG
        return lax.dot_general(qm, k_refs[skip + j][:, pair * LANES:(pair + 1) * LANES],
                               (((1,), (1,)), ((), ())), preferred_element_type=F32)

    def softmax_piece(h, s_all, c):
        w0 = (c // 2) * LANES
        lo = max(base - w0, 0)
        raw = s_all[c * CHUNK:(c + 1) * CHUNK, w0 + lo - base:w0 + WIN - base]
        z0, z1 = ZERO_BIAS_COLS[c % 2]
        z0 = max(z0, lo)
        parts = []
        if z0 > lo:
            parts.append(raw[:, 0:z0 - lo] + bias_s[h, c % 2, :, lo:z0])
        if z1 > z0:
            parts.append(raw[:, z0 - lo:z1 - lo])
        hi0 = max(z1, lo)
        parts.append(raw[:, hi0 - lo:WIN - lo] + bias_s[h, c % 2, :, hi0:WIN])
        s = jnp.concatenate(parts, axis=1)
        if first_valid is not None:
            col = lax.broadcasted_iota(jnp.int32, (1, WIN - lo), 1) + (w0 + lo)
            s = jnp.where(col >= first_valid, s, NEG_INF)
        m = jnp.max(s, axis=-1, keepdims=True)
        p = jnp.exp2(s - m)
        inv = 1.0 / jnp.sum(p, axis=-1, keepdims=True)
        row = [p.astype(BF16)]
        if w0 + lo > base:
            row.insert(0, jnp.zeros((CHUNK, w0 + lo - base), BF16))
        if w0 + WIN < NBUF:
            row.append(jnp.zeros((CHUNK, NBUF - w0 - WIN), BF16))
        return inv, jnp.concatenate(row, axis=1)

    def value_piece(h, pmat, j):
        pair = h // HEADS_PER_VREG
        return jnp.dot(pmat[:, j * TQ:(j + 1) * TQ],
                       v_refs[skip + j][:, pair * LANES:(pair + 1) * LANES],
                       preferred_element_type=F32)

    def finish(h, o, inv, o_prev):
        pair, hh = divmod(h, HEADS_PER_VREG)
        o = o * inv
        if hh == 0:
            return o
        in_head = jnp.logical_and(lane >= hh * HEAD_DIM, lane < (hh + 1) * HEAD_DIM)
        o_ref[:, pair * LANES:(pair + 1) * LANES] = jnp.where(in_head, o, o_prev).astype(BF16)
        return None

    assert 0 < n_p <= CHUNKS_PER_TILE and HEADS_PER_VREG == 2
    qm = masked_q(0)
    s_next = jnp.concatenate([score_piece(0, qm, j) for j in range(n_p)], axis=1)
    o_pair = None
    prev = None
    for h in range(n_heads + 1):
        cur = h < n_heads
        nxt = h + 1 < n_heads
        s_cur = s_next
        if nxt:
            qm = masked_q(h + 1)
        pieces, invs, prow, o_acc = [], [], [], None
        for c in range(CHUNKS_PER_TILE):
            if nxt and c < n_p:
                pieces.append(score_piece(h + 1, qm, c))
            if prev is not None and c < n_p:
                part = value_piece(prev[0], prev[1], c)
                o_acc = part if o_acc is None else o_acc + part
            if cur:
                inv, pr = softmax_piece(h, s_cur, c)
                invs.append(inv)
                prow.append(pr)
        if prev is not None:
            o_pair = finish(prev[0], o_acc, prev[2], o_pair)
        if nxt:
            s_next = jnp.concatenate(pieces, axis=1)
        prev = (h, jnp.concatenate(prow, axis=0), jnp.concatenate(invs, axis=0)) if cur else None


def _attn_kernel(*refs, n_heads):
    n_win = NBUF // TQ
    q_ref = refs[0]
    k_refs = refs[1:1 + n_win]
    v_refs = refs[1 + n_win:1 + 2 * n_win]
    rbp_ref, o_ref, bias_s = refs[1 + 2 * n_win:]
    b = pl.program_id(0)
    t = pl.program_id(1)

    @pl.when(jnp.logical_and(b == 0, t == 0))
    def _():
        _build_bias_tables(rbp_ref, bias_s, n_heads)

    @pl.when(t == 0)
    def _():
        _attn_tile(q_ref, k_refs, v_refs, o_ref, bias_s, n_win - 1, None, n_heads)

    @pl.when(jnp.logical_and(t > 0, t < n_win - 1))
    def _():
        _attn_tile(q_ref, k_refs, v_refs, o_ref, bias_s, 0, TQ * (n_win - 1 - t), n_heads)

    @pl.when(t >= n_win - 1)
    def _():
        _attn_tile(q_ref, k_refs, v_refs, o_ref, bias_s, 0, None, n_heads)


def _attention(qkv, rbp):
    bsz, seq, d3 = qkv.shape
    d_att = d3 // 3
    n_heads = d_att // HEAD_DIM
    n_win = NBUF // TQ
    assert TQ == MXU_DIM and seq % TQ == 0 and NBUF % TQ == 0 and n_heads % HEADS_PER_VREG == 0
    kern = functools.partial(_attn_kernel, n_heads=n_heads)

    def window(col, j):
        return pl.BlockSpec((None, TQ, d_att),
                            lambda b, t: (b, jnp.maximum(t - (n_win - 1 - j), 0), col))

    return pl.pallas_call(
        kern,
        grid=(bsz, seq // TQ),
        in_specs=([pl.BlockSpec((None, TQ, d_att), lambda b, t: (b, t, 0))]
                  + [window(1, j) for j in range(n_win)]
                  + [window(2, j) for j in range(n_win)]
                  + [_single(rbp.shape, lambda b, t: (0, 0))]),
        out_specs=pl.BlockSpec((None, TQ, d_att), lambda b, t: (b, t, 0)),
        out_shape=jax.ShapeDtypeStruct((bsz, seq, d_att), BF16),
        scratch_shapes=[pltpu.VMEM((n_heads, 2, CHUNK, WIN), F32)],
        compiler_params=pltpu.CompilerParams(dimension_semantics=("arbitrary", "arbitrary"),
                                             vmem_limit_bytes=VMEM_LIMIT_BYTES),
        name="chunk_attn",
    )(*([qkv] * (1 + 2 * n_win)), rbp)


def _outproj_ffn_kernel(x_ref, yr_ref, ya_ref, mod_ref, n2g_ref, fg_ref, wo_ref,
                        wu_ref, fcw_ref, fcb_ref, wd_ref, o_ref, halo_s,
                        *, d_model, d_ff, conv_w):
    i = pl.program_id(0)
    rows = TT * SUBLANES
    halo = (conv_w - 1) * SUBLANES
    n_hp = TT // HT

    @pl.when(i == 0)
    def _():
        halo_s[...] = jnp.zeros_like(halo_s)

    g1 = mod_ref[:, :, 2 * d_model:3 * d_model]
    sh2 = mod_ref[:, :, 3 * d_model:4 * d_model]
    gain2 = n2g_ref[...] * (1.0 + mod_ref[:, :, 4 * d_model:5 * d_model])
    g2 = mod_ref[:, :, 5 * d_model:6 * d_model]

    def out_proj(t):
        ts = slice(t * TT, (t + 1) * TT)
        yr = yr_ref[:, ts, :].reshape(rows, yr_ref.shape[2])
        ya = ya_ref[:, ts, :].reshape(rows, ya_ref.shape[2])
        d_rnn = yr_ref.shape[2]
        y = (jnp.dot(yr, wo_ref[0:d_rnn, :], preferred_element_type=F32)
             + jnp.dot(ya, wo_ref[d_rnn:, :], preferred_element_type=F32))
        return y.reshape(SUBLANES, TT, d_model)

    def head_piece(t, c, y3):
        cs = slice(c * HT, (c + 1) * HT)
        x1 = x_ref[:, t * TT + c * HT:t * TT + (c + 1) * HT, :] + g1 * y3[:, cs, :]
        ms = jnp.mean(x1 * x1, axis=-1, keepdims=True)
        hn = x1 * lax.rsqrt(ms + EPS) * gain2 + sh2
        return x1, jnp.swapaxes(hn, 0, 1).reshape(HT * SUBLANES, d_model).astype(BF16)

    def tail_piece(t, c, z, x1):
        zc = z[c * HT * SUBLANES:(c + 1) * HT * SUBLANES, :].reshape(HT, SUBLANES, d_model)
        x2 = x1 + g2 * jnp.swapaxes(zc, 0, 1)
        ms2 = jnp.mean(x2 * x2, axis=-1, keepdims=True)
        o_ref[:, t * TT + c * HT:t * TT + (c + 1) * HT, :] = x2 * lax.rsqrt(ms2 + EPS) * fg_ref[...]

    def up_piece(hb, cols):
        return jnp.dot(hb, wu_ref[:, cols], preferred_element_type=F32)

    def conv_piece(up, tail, cols):
        ext = jnp.concatenate([tail, up], axis=0)
        u = fcb_ref[:, cols]
        for k in range(conv_w):
            u = u + ext[k * SUBLANES:k * SUBLANES + rows, :] * fcw_ref[k:k + 1, cols]
        return u, ext[rows:rows + halo, :]

    def ffn(hb, tails, side):
        side = list(side)
        z = None
        new_tails = []
        for j, (c0, w) in enumerate(FF_SPLITS):
            cg = slice(c0, c0 + w)
            cv = slice(d_ff + c0, d_ff + c0 + w)
            up_g = up_piece(hb, cg)
            if side:
                side.pop(0)()
            up_v = up_piece(hb, cv)
            if side:
                side.pop(0)()
            ug, tg = conv_piece(up_g, tails[j][0], cg)
            uv, tv = conv_piece(up_v, tails[j][1], cv)
            new_tails.append((tg, tv))
            act = (ug * jax.nn.sigmoid(ug) * uv).astype(BF16)
            zc = jnp.dot(act, wd_ref[c0:c0 + w, :], preferred_element_type=F32)
            z = zc if z is None else z + zc
        assert not side
        return z, new_tails

    tails = [(halo_s[:, c0:c0 + w], halo_s[:, d_ff + c0:d_ff + c0 + w]) for c0, w in FF_SPLITS]
    y3 = out_proj(0)
    hp = [head_piece(0, c, y3) for c in range(n_hp)]
    x1_0 = [p[0] for p in hp]
    hb_0 = jnp.concatenate([p[1] for p in hp], axis=0)

    state = {}

    def side_out_proj():
        state["y3"] = out_proj(1)

    def side_head(c):
        def run():
            state.setdefault("hp", []).append(head_piece(1, c, state["y3"]))
        return run

    z_0, tails = ffn(hb_0, tails, [side_out_proj] + [side_head(c) for c in range(n_hp)])
    x1_1 = [p[0] for p in state["hp"]]
    hb_1 = jnp.concatenate([p[1] for p in state["hp"]], axis=0)

    def side_tail(c):
        return lambda: tail_piece(0, c, z_0, x1_0[c])

    z_1, tails = ffn(hb_1, tails, [side_tail(c) for c in range(n_hp)])
    for c in range(n_hp):
        tail_piece(1, c, z_1, x1_1[c])
    for (c0, w), (tg, tv) in zip(FF_SPLITS, tails):
        halo_s[:, c0:c0 + w] = tg
        halo_s[:, d_ff + c0:d_ff + c0 + w] = tv


def _outproj_ffn(x, yr, ya, mod, n2g, fg, w_out_b, w_up_b, fcw, fcb, w_down_b):
    bsz, seq, d = x.shape
    d_ff = w_down_b.shape[0]
    cw = fcw.shape[0]
    d_rnn = yr.shape[2]
    d_att = ya.shape[2]
    step_t = K3_TILES * TT
    assert K3_TILES == 2 and sum(w for _, w in FF_SPLITS) == d_ff
    assert seq % step_t == 0 and TT % HT == 0
    const = lambda i: (0, 0)
    kern = functools.partial(_outproj_ffn_kernel, d_model=d, d_ff=d_ff, conv_w=cw)
    return pl.pallas_call(
        kern,
        grid=(seq // step_t,),
        in_specs=[
            pl.BlockSpec((bsz, step_t, d), lambda i: (0, i, 0)),
            pl.BlockSpec((bsz, step_t, d_rnn), lambda i: (0, i, 0)),
            pl.BlockSpec((bsz, step_t, d_att), lambda i: (0, i, 0)),
            _single(mod.shape, lambda i: (0, 0, 0)),
            _single((1, d), const),
            _single((1, d), const),
            _single(w_out_b.shape, const),
            _single(w_up_b.shape, const),
            _single(fcw.shape, const),
            _single((1, 2 * d_ff), const),
            _single(w_down_b.shape, const),
        ],
        out_specs=pl.BlockSpec((bsz, step_t, d), lambda i: (0, i, 0)),
        out_shape=jax.ShapeDtypeStruct((bsz, seq, d), F32),
        scratch_shapes=[pltpu.VMEM(((cw - 1) * SUBLANES, 2 * d_ff), F32)],
        compiler_params=pltpu.CompilerParams(dimension_semantics=("arbitrary",),
                                             vmem_limit_bytes=VMEM_LIMIT_BYTES),
        name="outproj_ffn",
    )(x, yr, ya, mod, n2g.reshape(1, d), fg.reshape(1, d), w_out_b, w_up_b, fcw,
      fcb.reshape(1, 2 * d_ff), w_down_b)


def kernel(x, c, ada_w, ada_b, norm1_g, w_in, rnn_conv_w, rnn_conv_b, rg_wa, rg_ba, rg_wx, rg_bx, rg_lambda, rel_bias, w_out, norm2_g, w_up, ffn_conv_w, ffn_conv_b, w_down, final_g):
    assert ada_w.shape[0] == 1
    l = 0
    mod = _ada(c, ada_w[l], ada_b[l])
    qkv, yr, (w_out_b, w_up_b, w_down_b) = _inproj_rglru(
        x, mod, norm1_g[l], w_in[l], rnn_conv_w[l], rnn_conv_b[l], rg_wa[l], rg_wx[l],
        rg_ba[l], rg_bx[l], rg_lambda[l], (w_out[l], w_up[l], w_down[l]))
    rbp = jnp.pad(rel_bias[l], ((0, 0), (0, RBPAD - rel_bias.shape[2])))
    ya = _attention(qkv, rbp)
    return _outproj_ffn(x, yr, ya, mod, norm2_g[l], final_g, w_out_b, w_up_b,
                        ffn_conv_w[l], ffn_conv_b[l], w_down_b)
```

```python
import functools

import jax
import jax.numpy as jnp
from jax import lax
from jax.experimental import pallas as pl
from jax.experimental.pallas import tpu as pltpu

F32 = jnp.float32
BF16 = jnp.bfloat16

CHUNK = 64
LOOKBACK = 8
REL_CLIP = 128
HEAD_DIM = 64
RG_C = 8.0
EPS = 1e-6
NEG_INF = -1e30
LOG2E = 1.4426950408889634

SUBLANES = 8
LANES = 128
MXU_DIM = 256
VMEM_LIMIT_BYTES = 56 * 1024 * 1024

TT = 64
RT = 8
PW = 256
T1 = 128
CHUNKS_PER_TILE = 4
TQ = CHUNKS_PER_TILE * CHUNK
HEADS_PER_VREG = LANES // HEAD_DIM
HIST = LOOKBACK * CHUNK
NBUF = HIST + TQ
WIN = (LOOKBACK + 2) * CHUNK
GPAD = 768
RBPAD = 384
FF_SPLITS = ((0, 1024), (1024, 1024), (2048, 768))
K3_TILES = 2
HT = 16


def _single(block_shape, index_map):
    return pl.BlockSpec(block_shape, index_map, pipeline_mode=pl.Buffered(1))


def _ada_kernel(c_ref, w_ref, b_ref, o_ref):
    c = c_ref[...]
    sc = c * jax.nn.sigmoid(c)
    w = w_ref[...]
    w_hi = w.astype(BF16)
    w_lo = (w - w_hi.astype(F32)).astype(BF16)
    sc_hi = sc.astype(BF16)
    sc_lo = (sc - sc_hi.astype(F32)).astype(BF16)
    acc = jnp.dot(sc_hi, w_hi, preferred_element_type=F32)
    acc = acc + jnp.dot(sc_lo, w_hi, preferred_element_type=F32)
    acc = acc + jnp.dot(sc_hi, w_lo, preferred_element_type=F32)
    o_ref[...] = (acc + b_ref[...]).reshape(o_ref.shape)


def _ada(c, ada_w, ada_b):
    bsz, d = c.shape
    n = ada_w.shape[1]
    bn = 1024
    return pl.pallas_call(
        _ada_kernel,
        grid=(n // bn,),
        in_specs=[pl.BlockSpec((bsz, d), lambda j: (0, 0)),
                  pl.BlockSpec((d, bn), lambda j: (0, j)),
                  pl.BlockSpec((1, bn), lambda j: (0, j))],
        out_specs=pl.BlockSpec((bsz, 1, bn), lambda j: (0, 0, j)),
        out_shape=jax.ShapeDtypeStruct((bsz, 1, n), F32),
        name="ada_mod",
    )(c, ada_w, ada_b.reshape(1, n))


def _inproj_rglru_kernel(x_ref, mod_ref, n1g_ref, win_ref, cw_ref, cb_ref, wa_ref, wx_ref,
                         ba_ref, bx_ref, lam_ref, *rest, d_model, d_rnn, d_att, conv_w, n_cast):
    cast_in = rest[:n_cast]
    qkv_ref, yr_ref = rest[n_cast:n_cast + 2]
    cast_out = rest[n_cast + 2:2 * n_cast + 2]
    tail_s, hc, win_s, wg_s = rest[2 * n_cast + 2:]
    i = pl.program_id(0)

    for src, dst in zip(cast_in, cast_out):
        dst[...] = src[...].astype(BF16)
    halo = (conv_w - 1) * SUBLANES
    prow = RT * SUBLANES
    half = d_rnn // 2

    @pl.when(i == 0)
    def _():
        tail_s[...] = jnp.zeros_like(tail_s)
        hc[...] = jnp.zeros_like(hc)
        for c0 in range(0, win_ref.shape[1], PW):
            wcol = win_ref[:, c0:c0 + PW]
            if 2 * d_rnn <= c0 < 2 * d_rnn + d_att:
                wcol = wcol * (LOG2E * HEAD_DIM ** -0.5)
            win_s[:, c0:c0 + PW] = wcol.astype(BF16)
        wg_s[...] = jnp.zeros_like(wg_s)
        nb, bw = wa_ref.shape[0], wa_ref.shape[1]
        per_half = nb // 2
        for n in range(nb):
            hf, k = divmod(n, per_half)
            rs = slice(k * bw, (k + 1) * bw)
            wg_s[hf, rs, k * bw:(k + 1) * bw] = wa_ref[n].astype(BF16)
            wg_s[hf, rs, half + k * bw:half + (k + 1) * bw] = wx_ref[n].astype(BF16)

    gain1 = n1g_ref[...] * (1.0 + mod_ref[:, :, d_model:2 * d_model])
    lam = lam_ref[...]
    log_sig = jnp.minimum(lam, 0.0) - jnp.log1p(jnp.exp(-jnp.abs(lam)))
    neg_rate_half = (-0.5 * RG_C) * log_sig
    rate2_half = -LOG2E * neg_rate_half
    cw_half = 0.5 * cw_ref[...]
    cb_half = 0.5 * cb_ref[...]
    ba_half = 0.5 * ba_ref[...]
    bx_half = 0.5 * bx_ref[...]
    sh1 = mod_ref[:, :, 0:d_model]
    trow = TT * SUBLANES
    n_rec = TT // RT

    def normed_lhs(t):
        x = x_ref[:, t * TT:(t + 1) * TT, :]
        ms = jnp.mean(x * x, axis=-1, keepdims=True)
        h = x * lax.rsqrt(ms + EPS) * gain1 + sh1
        return h.reshape(trow, d_model).astype(BF16)

    def project_cols(hb, lo, hi):
        part = jnp.dot(hb, win_s[:, lo:hi], preferred_element_type=F32)
        return part.reshape(SUBLANES, TT, hi - lo)

    def branch(hb, j):
        return [project_cols(hb, j * d_rnn + p, j * d_rnn + p + PW) for p in range(0, d_rnn, PW)]

    def emit_qkv(t, hb, p):
        o_q = 2 * d_rnn
        part = project_cols(hb, o_q + p * PW, o_q + (p + 1) * PW)
        qkv_ref[:, t * TT:(t + 1) * TT, p * PW:(p + 1) * PW] = part.astype(BF16)

    def recur_piece(t, xr_p, gr_p, c, tail, hcur):
        ts = slice(c * RT, (c + 1) * RT)
        xr = jnp.concatenate([jnp.swapaxes(p[:, ts, :], 0, 1).reshape(prow, PW) for p in xr_p], axis=1)
        ext = jnp.concatenate([tail, xr], axis=0)
        xh = cb_half
        for k in range(conv_w):
            xh = xh + ext[k * SUBLANES:k * SUBLANES + prow, :] * cw_half[k:k + 1, :]
        xhb = xh.astype(BF16)
        g0 = jnp.dot(xhb[:, 0:half], wg_s[0], preferred_element_type=F32)
        g1 = jnp.dot(xhb[:, half:d_rnn], wg_s[1], preferred_element_type=F32)
        th_r = jnp.tanh(jnp.concatenate([g0[:, 0:half], g1[:, 0:half]], axis=1) + ba_half)
        th_i = jnp.tanh(jnp.concatenate([g0[:, half:], g1[:, half:]], axis=1) + bx_half)
        a = jnp.exp2(th_r * rate2_half + rate2_half)
        w = jnp.tanh(th_r * neg_rate_half + neg_rate_half) * (a * a + 1.0)
        mult = jnp.where(w > 0.0, w * lax.rsqrt(w), 0.0)
        a3 = a.reshape(RT, SUBLANES, d_rnn)
        b3 = (mult * (th_i * xh + xh)).reshape(RT, SUBLANES, d_rnn)
        hs = []
        for s in range(RT):
            hcur = a3[s] * hcur + b3[s]
            hs.append(hcur)
        gate = jnp.concatenate([jax.nn.gelu(p[:, ts, :]) for p in gr_p], axis=2)
        yr = jnp.swapaxes(jnp.stack(hs, axis=0), 0, 1) * gate
        yr_ref[:, t * TT + c * RT:t * TT + (c + 1) * RT, :] = yr.astype(BF16)
        return ext[prow:prow + halo, :], hcur

    n_tiles = T1 // TT
    n_qkv = (3 * d_att) // PW
    tail, hcur = tail_s[...], hc[...]
    hb = normed_lhs(0)
    xr_p, gr_p = branch(hb, 0), branch(hb, 1)
    for t in range(n_tiles):
        nxt = t + 1 < n_tiles
        ahead = {}
        for c in range(max(n_rec, n_qkv)):
            if c < n_qkv:
                emit_qkv(t, hb, c)
            if c < n_rec:
                tail, hcur = recur_piece(t, xr_p, gr_p, c, tail, hcur)
            if nxt and c == n_rec - 3:
                ahead["hb"] = normed_lhs(t + 1)
            if nxt and c == n_rec - 2:
                ahead["xr"] = branch(ahead["hb"], 0)
            if nxt and c == n_rec - 1:
                ahead["gr"] = branch(ahead["hb"], 1)
        if nxt:
            hb, xr_p, gr_p = ahead["hb"], ahead["xr"], ahead["gr"]
    tail_s[...] = tail
    hc[...] = hcur


def _inproj_rglru(x, mod, n1g, w_in, conv_w, conv_b, wa, wx, ba, bx, lam, later_weights):
    bsz, seq, d = x.shape
    assert bsz == SUBLANES and seq % T1 == 0
    d_in = w_in.shape[1]
    d_rnn = conv_w.shape[1]
    d_att = (d_in - 2 * d_rnn) // 3
    assert d_att == d_rnn and T1 % TT == 0 and TT % RT == 0
    cw = conv_w.shape[0]
    const = lambda i: (0, 0)
    n_steps = seq // T1
    assert all(w.shape[0] % (2 * SUBLANES * n_steps) == 0 for w in later_weights)
    cast_specs = [pl.BlockSpec((w.shape[0] // n_steps, w.shape[1]), lambda i: (i, 0))
                  for w in later_weights]
    kern = functools.partial(_inproj_rglru_kernel, d_model=d, d_rnn=d_rnn, d_att=d_att, conv_w=cw,
                             n_cast=len(later_weights))
    outs = pl.pallas_call(
        kern,
        grid=(n_steps,),
        in_specs=[
            pl.BlockSpec((bsz, T1, d), lambda i: (0, i, 0)),
            _single(mod.shape, lambda i: (0, 0, 0)),
            _single((1, d), const),
            _single(w_in.shape, const),
            _single(conv_w.shape, const),
            _single((1, d_rnn), const),
            _single(wa.shape, lambda i: (0, 0, 0)),
            _single(wx.shape, lambda i: (0, 0, 0)),
            _single((1, d_rnn), const),
            _single((1, d_rnn), const),
            _single((1, d_rnn), const),
        ] + cast_specs,
        out_specs=[pl.BlockSpec((bsz, T1, 3 * d_att), lambda i: (0, i, 0)),
                   pl.BlockSpec((bsz, T1, d_rnn), lambda i: (0, i, 0))] + cast_specs,
        out_shape=[jax.ShapeDtypeStruct((bsz, seq, 3 * d_att), BF16),
                   jax.ShapeDtypeStruct((bsz, seq, d_rnn), BF16)]
        + [jax.ShapeDtypeStruct(w.shape, BF16) for w in later_weights],
        scratch_shapes=[
            pltpu.VMEM(((cw - 1) * SUBLANES, d_rnn), F32),
            pltpu.VMEM((SUBLANES, d_rnn), F32),
            pltpu.VMEM(w_in.shape, BF16),
            pltpu.VMEM((2, d_rnn // 2, d_rnn), BF16),
        ],
        compiler_params=pltpu.CompilerParams(dimension_semantics=("arbitrary",),
                                             vmem_limit_bytes=VMEM_LIMIT_BYTES),
        name="inproj_rglru",
    )(x, mod, n1g.reshape(1, d), w_in, conv_w, conv_b.reshape(1, d_rnn), wa, wx,
      ba.reshape(1, d_rnn), bx.reshape(1, d_rnn), lam.reshape(1, d_rnn), *later_weights)
    return outs[0], outs[1], outs[2:]


def _build_bias_tables(rbp_ref, bias_s, n_heads):
    r_io = lax.broadcasted_iota(jnp.int32, (RBPAD, GPAD), 0)
    j_io = lax.broadcasted_iota(jnp.int32, (RBPAD, GPAD), 1)
    idx = jnp.clip(HIST + CHUNK - 1 - j_io, -REL_CLIP, REL_CLIP) + REL_CLIP
    sel = (r_io == idx).astype(F32)
    g = jnp.dot(rbp_ref[...], sel, precision=lax.Precision.HIGHEST, preferred_element_type=F32)
    g = (g - g[:, 0:1]) * LOG2E
    col = lax.broadcasted_iota(jnp.int32, (CHUNK, WIN), 1)
    band = (LOOKBACK + 1) * CHUNK
    for h in range(n_heads):
        gh = jnp.broadcast_to(g[h:h + 1, :], (CHUNK, GPAD))
        even = pltpu.roll(gh, GPAD - (CHUNK - 1), 1, stride=1, stride_axis=0)[:, 0:WIN]
        odd = pltpu.roll(gh, 1, 1, stride=1, stride_axis=0)[:, 0:WIN]
        bias_s[h, 0] = jnp.where(col < band, even, NEG_INF)
        bias_s[h, 1] = jnp.where(col >= CHUNK, odd, NEG_INF)


ZERO_BIAS_COLS = ((0, HIST - REL_CLIP), (LANES, HIST - REL_CLIP))


def _attn_tile(q_ref, k_refs, v_refs, o_ref, bias_s, skip, first_valid, n_heads):
    lane = lax.broadcasted_iota(jnp.int32, (1, LANES), 1)
    n_kt = len(k_refs)
    base = skip * TQ
    n_p = n_kt - skip

    def masked_q(h):
        pair, hh = divmod(h, HEADS_PER_VREG)
        qp = q_ref[:, pair * LANES:(pair + 1) * LANES]
        in_head = jnp.logical_and(lane >= hh * HEAD_DIM, lane < (hh + 1) * HEAD_DIM)
        return jnp.where(in_head, qp, jnp.zeros_like(qp))

    def score_piece(h, qm, j):
        pair = h // HEADS_PER_VREG
        return lax.dot_general(qm, k_refs[skip + j][:, pair * LANES:(pair + 1) * LANES],
                               (((1,), (1,)), ((), ())), preferred_element_type=F32)

    def softmax_piece(h, s_all, c):
        w0 = (c // 2) * LANES
        lo = max(base - w0, 0)
        raw = s_all[c * CHUNK:(c + 1) * CHUNK, w0 + lo - base:w0 + WIN - base]
        z0, z1 = ZERO_BIAS_COLS[c % 2]
        z0 = max(z0, lo)
        parts = []
        if z0 > lo:
            parts.append(raw[:, 0:z0 - lo] + bias_s[h, c % 2, :, lo:z0])
        if z1 > z0:
            parts.append(raw[:, z0 - lo:z1 - lo])
        hi0 = max(z1, lo)
        parts.append(raw[:, hi0 - lo:WIN - lo] + bias_s[h, c % 2, :, hi0:WIN])
        s = jnp.concatenate(parts, axis=1)
        if first_valid is not None:
            col = lax.broadcasted_iota(jnp.int32, (1, WIN - lo), 1) + (w0 + lo)
            s = jnp.where(col >= first_valid, s, NEG_INF)
        m = jnp.max(s, axis=-1, keepdims=True)
        p = jnp.exp2(s - m)
        inv = 1.0 / jnp.sum(p, axis=-1, keepdims=True)
        row = [p.astype(BF16)]
        if w0 + lo > base:
            row.insert(0, jnp.zeros((CHUNK, w0 + lo - base), BF16))
        if w0 + WIN < NBUF:
            row.append(jnp.zeros((CHUNK, NBUF - w0 - WIN), BF16))
        return inv, jnp.concatenate(row, axis=1)

    def value_piece(h, pmat, j):
        pair = h // HEADS_PER_VREG
        return jnp.dot(pmat[:, j * TQ:(j + 1) * TQ],
                       v_refs[skip + j][:, pair * LANES:(pair + 1) * LANES],
                       preferred_element_type=F32)

    def finish(h, o, inv, o_prev):
        pair, hh = divmod(h, HEADS_PER_VREG)
        o = o * inv
        if hh == 0:
            return o
        in_head = jnp.logical_and(lane >= hh * HEAD_DIM, lane < (hh + 1) * HEAD_DIM)
        o_ref[:, pair * LANES:(pair + 1) * LANES] = jnp.where(in_head, o, o_prev).astype(BF16)
        return None

    assert 0 < n_p <= CHUNKS_PER_TILE and HEADS_PER_VREG == 2
    qm = masked_q(0)
    s_next = jnp.concatenate([score_piece(0, qm, j) for j in range(n_p)], axis=1)
    o_pair = None
    prev = None
    for h in range(n_heads + 1):
        cur = h < n_heads
        nxt = h + 1 < n_heads
        s_cur = s_next
        if nxt:
            qm = masked_q(h + 1)
        pieces, invs, prow, o_acc = [], [], [], None
        for c in range(CHUNKS_PER_TILE):
            if nxt and c < n_p:
                pieces.append(score_piece(h + 1, qm, c))
            if prev is not None and c < n_p:
                part = value_piece(prev[0], prev[1], c)
                o_acc = part if o_acc is None else o_acc + part
            if cur:
                inv, pr = softmax_piece(h, s_cur, c)
                invs.append(inv)
                prow.append(pr)
        if prev is not None:
            o_pair = finish(prev[0], o_acc, prev[2], o_pair)
        if nxt:
            s_next = jnp.concatenate(pieces, axis=1)
        prev = (h, jnp.concatenate(prow, axis=0), jnp.concatenate(invs, axis=0)) if cur else None


def _attn_kernel(*refs, n_heads):
    n_win = NBUF // TQ
    q_ref = refs[0]
    k_refs = refs[1:1 + n_win]
    v_refs = refs[1 + n_win:1 + 2 * n_win]
    rbp_ref, o_ref, bias_s = refs[1 + 2 * n_win:]
    b = pl.program_id(0)
    t = pl.program_id(1)

    @pl.when(jnp.logical_and(b == 0, t == 0))
    def _():
        _build_bias_tables(rbp_ref, bias_s, n_heads)

    @pl.when(t == 0)
    def _():
        _attn_tile(q_ref, k_refs, v_refs, o_ref, bias_s, n_win - 1, None, n_heads)

    @pl.when(jnp.logical_and(t > 0, t < n_win - 1))
    def _():
        _attn_tile(q_ref, k_refs, v_refs, o_ref, bias_s, 0, TQ * (n_win - 1 - t), n_heads)

    @pl.when(t >= n_win - 1)
    def _():
        _attn_tile(q_ref, k_refs, v_refs, o_ref, bias_s, 0, None, n_heads)


def _attention(qkv, rbp):
    bsz, seq, d3 = qkv.shape
    d_att = d3 // 3
    n_heads = d_att // HEAD_DIM
    n_win = NBUF // TQ
    assert TQ == MXU_DIM and seq % TQ == 0 and NBUF % TQ == 0 and n_heads % HEADS_PER_VREG == 0
    kern = functools.partial(_attn_kernel, n_heads=n_heads)

    def window(col, j):
        return pl.BlockSpec((None, TQ, d_att),
                            lambda b, t: (b, jnp.maximum(t - (n_win - 1 - j), 0), col))

    return pl.pallas_call(
        kern,
        grid=(bsz, seq // TQ),
        in_specs=([pl.BlockSpec((None, TQ, d_att), lambda b, t: (b, t, 0))]
                  + [window(1, j) for j in range(n_win)]
                  + [window(2, j) for j in range(n_win)]
                  + [_single(rbp.shape, lambda b, t: (0, 0))]),
        out_specs=pl.BlockSpec((None, TQ, d_att), lambda b, t: (b, t, 0)),
        out_shape=jax.ShapeDtypeStruct((bsz, seq, d_att), BF16),
        scratch_shapes=[pltpu.VMEM((n_heads, 2, CHUNK, WIN), F32)],
        compiler_params=pltpu.CompilerParams(dimension_semantics=("arbitrary", "arbitrary"),
                                             vmem_limit_bytes=VMEM_LIMIT_BYTES),
        name="chunk_attn",
    )(*([qkv] * (1 + 2 * n_win)), rbp)


def _outproj_ffn_kernel(x_ref, yr_ref, ya_ref, mod_ref, n2g_ref, fg_ref, wo_ref,
                        wu_ref, fcw_ref, fcb_ref, wd_ref, o_ref, halo_s,
                        *, d_model, d_ff, conv_w):
    i = pl.program_id(0)
    rows = TT * SUBLANES
    halo = (conv_w - 1) * SUBLANES
    n_hp = TT // HT

    @pl.when(i == 0)
    def _():
        halo_s[...] = jnp.zeros_like(halo_s)

    g1 = mod_ref[:, :, 2 * d_model:3 * d_model]
    sh2 = mod_ref[:, :, 3 * d_model:4 * d_model]
    gain2 = n2g_ref[...] * (1.0 + mod_ref[:, :, 4 * d_model:5 * d_model])
    g2 = mod_ref[:, :, 5 * d_model:6 * d_model]

    def out_proj(t):
        ts = slice(t * TT, (t + 1) * TT)
        yr = yr_ref[:, ts, :].reshape(rows, yr_ref.shape[2])
        ya = ya_ref[:, ts, :].reshape(rows, ya_ref.shape[2])
        d_rnn = yr_ref.shape[2]
        y = (jnp.dot(yr, wo_ref[0:d_rnn, :], preferred_element_type=F32)
             + jnp.dot(ya, wo_ref[d_rnn:, :], preferred_element_type=F32))
        return y.reshape(SUBLANES, TT, d_model)

    def head_piece(t, c, y3):
        cs = slice(c * HT, (c + 1) * HT)
        x1 = x_ref[:, t * TT + c * HT:t * TT + (c + 1) * HT, :] + g1 * y3[:, cs, :]
        ms = jnp.mean(x1 * x1, axis=-1, keepdims=True)
        hn = x1 * lax.rsqrt(ms + EPS) * gain2 + sh2
        return x1, jnp.swapaxes(hn, 0, 1).reshape(HT * SUBLANES, d_model).astype(BF16)

    def tail_piece(t, c, z, x1):
        zc = z[c * HT * SUBLANES:(c + 1) * HT * SUBLANES, :].reshape(HT, SUBLANES, d_model)
        x2 = x1 + g2 * jnp.swapaxes(zc, 0, 1)
        ms2 = jnp.mean(x2 * x2, axis=-1, keepdims=True)
        o_ref[:, t * TT + c * HT:t * TT + (c + 1) * HT, :] = x2 * lax.rsqrt(ms2 + EPS) * fg_ref[...]

    def up_piece(hb, cols):
        return jnp.dot(hb, wu_ref[:, cols], preferred_element_type=F32)

    def conv_piece(up, tail, cols, scale):
        ext = jnp.concatenate([tail, up], axis=0)
        u = scale * fcb_ref[:, cols]
        for k in range(conv_w):
            u = u + ext[k * SUBLANES:k * SUBLANES + rows, :] * (scale * fcw_ref[k:k + 1, cols])
        return u, ext[rows:rows + halo, :]

    def ffn(hb, tails, side):
        side = list(side)
        z = None
        new_tails = []
        for j, (c0, w) in enumerate(FF_SPLITS):
            cg = slice(c0, c0 + w)
            cv = slice(d_ff + c0, d_ff + c0 + w)
            up_g = up_piece(hb, cg)
            if side:
                side.pop(0)()
            up_v = up_piece(hb, cv)
            if side:
                side.pop(0)()
            hg, tg = conv_piece(up_g, tails[j][0], cg, 0.5)
            uv, tv = conv_piece(up_v, tails[j][1], cv, 1.0)
            new_tails.append((tg, tv))
            act = ((hg + hg * jnp.tanh(hg)) * uv).astype(BF16)
            zc = jnp.dot(act, wd_ref[c0:c0 + w, :], preferred_element_type=F32)
            z = zc if z is None else z + zc
        assert not side
        return z, new_tails

    tails = [(halo_s[:, c0:c0 + w], halo_s[:, d_ff + c0:d_ff + c0 + w]) for c0, w in FF_SPLITS]
    y3 = out_proj(0)
    hp = [head_piece(0, c, y3) for c in range(n_hp)]
    x1_0 = [p[0] for p in hp]
    hb_0 = jnp.concatenate([p[1] for p in hp], axis=0)

    state = {}

    def side_out_proj():
        state["y3"] = out_proj(1)

    def side_head(c):
        def run():
            state.setdefault("hp", []).append(head_piece(1, c, state["y3"]))
        return run

    z_0, tails = ffn(hb_0, tails, [side_out_proj] + [side_head(c) for c in range(n_hp)])
    x1_1 = [p[0] for p in state["hp"]]
    hb_1 = jnp.concatenate([p[1] for p in state["hp"]], axis=0)

    def side_tail(c):
        return lambda: tail_piece(0, c, z_0, x1_0[c])

    z_1, tails = ffn(hb_1, tails, [side_tail(c) for c in range(n_hp)])
    for c in range(n_hp):
        tail_piece(1, c, z_1, x1_1[c])
    for (c0, w), (tg, tv) in zip(FF_SPLITS, tails):
        halo_s[:, c0:c0 + w] = tg
        halo_s[:, d_ff + c0:d_ff + c0 + w] = tv


def _outproj_ffn(x, yr, ya, mod, n2g, fg, w_out_b, w_up_b, fcw, fcb, w_down_b):
    bsz, seq, d = x.shape
    d_ff = w_down_b.shape[0]
    cw = fcw.shape[0]
    d_rnn = yr.shape[2]
    d_att = ya.shape[2]
    step_t = K3_TILES * TT
    assert K3_TILES == 2 and sum(w for _, w in FF_SPLITS) == d_ff
    assert seq % step_t == 0 and TT % HT == 0
    const = lambda i: (0, 0)
    kern = functools.partial(_outproj_ffn_kernel, d_model=d, d_ff=d_ff, conv_w=cw)
    return pl.pallas_call(
        kern,
        grid=(seq // step_t,),
        in_specs=[
            pl.BlockSpec((bsz, step_t, d), lambda i: (0, i, 0)),
            pl.BlockSpec((bsz, step_t, d_rnn), lambda i: (0, i, 0)),
            pl.BlockSpec((bsz, step_t, d_att), lambda i: (0, i, 0)),
            _single(mod.shape, lambda i: (0, 0, 0)),
            _single((1, d), const),
            _single((1, d), const),
            _single(w_out_b.shape, const),
            _single(w_up_b.shape, const),
            _single(fcw.shape, const),
            _single((1, 2 * d_ff), const),
            _single(w_down_b.shape, const),
        ],
        out_specs=pl.BlockSpec((bsz, step_t, d), lambda i: (0, i, 0)),
        out_shape=jax.ShapeDtypeStruct((bsz, seq, d), F32),
        scratch_shapes=[pltpu.VMEM(((cw - 1) * SUBLANES, 2 * d_ff), F32)],
        compiler_params=pltpu.CompilerParams(dimension_semantics=("arbitrary",),
                                             vmem_limit_bytes=VMEM_LIMIT_BYTES),
        name="outproj_ffn",
    )(x, yr, ya, mod, n2g.reshape(1, d), fg.reshape(1, d), w_out_b, w_up_b, fcw,
      fcb.reshape(1, 2 * d_ff), w_down_b)


def kernel(x, c, ada_w, ada_b, norm1_g, w_in, rnn_conv_w, rnn_conv_b, rg_wa, rg_ba, rg_wx, rg_bx, rg_lambda, rel_bias, w_out, norm2_g, w_up, ffn_conv_w, ffn_conv_b, w_down, final_g):
    assert ada_w.shape[0] == 1
    l = 0
    mod = _ada(c, ada_w[l], ada_b[l])
    qkv, yr, (w_out_b, w_up_b, w_down_b) = _inproj_rglru(
        x, mod, norm1_g[l], w_in[l], rnn_conv_w[l], rnn_conv_b[l], rg_wa[l], rg_wx[l],
        rg_ba[l], rg_bx[l], rg_lambda[l], (w_out[l], w_up[l], w_down[l]))
    rbp = jnp.pad(rel_bias[l], ((0, 0), (0, RBPAD - rel_bias.shape[2])))
    ya = _attention(qkv, rbp)
    return _outproj_ffn(x, yr, ya, mod, norm2_g[l], final_g, w_out_b, w_up_b,
                        ffn_conv_w[l], ffn_conv_b[l], w_down_b)
```

```python
import functools

import jax
import jax.numpy as jnp
from jax import lax
from jax.experimental import pallas as pl
from jax.experimental.pallas import tpu as pltpu

F32 = jnp.float32
BF16 = jnp.bfloat16

CHUNK = 64
LOOKBACK = 8
REL_CLIP = 128
HEAD_DIM = 64
RG_C = 8.0
EPS = 1e-6
NEG_INF = -1e30
LOG2E = 1.4426950408889634

SUBLANES = 8
LANES = 128
MXU_DIM = 256
VMEM_LIMIT_BYTES = 56 * 1024 * 1024

TT = 64
RT = 8
PW = 256
T1 = 128
CHUNKS_PER_TILE = 4
TQ = CHUNKS_PER_TILE * CHUNK
HEADS_PER_VREG = LANES // HEAD_DIM
HIST = LOOKBACK * CHUNK
NBUF = HIST + TQ
WIN = (LOOKBACK + 2) * CHUNK
GPAD = 768
RBPAD = 384
FF_SPLITS = ((0, 1024), (1024, 1024), (2048, 768))
K3_TILES = 2
HT = 16


def _single(block_shape, index_map):
    return pl.BlockSpec(block_shape, index_map, pipeline_mode=pl.Buffered(1))


def _ada_kernel(c_ref, w_ref, b_ref, o_ref):
    c = c_ref[...]
    sc = c * jax.nn.sigmoid(c)
    w = w_ref[...]
    w_hi = w.astype(BF16)
    w_lo = (w - w_hi.astype(F32)).astype(BF16)
    sc_hi = sc.astype(BF16)
    sc_lo = (sc - sc_hi.astype(F32)).astype(BF16)
    acc = jnp.dot(sc_hi, w_hi, preferred_element_type=F32)
    acc = acc + jnp.dot(sc_lo, w_hi, preferred_element_type=F32)
    acc = acc + jnp.dot(sc_hi, w_lo, preferred_element_type=F32)
    o_ref[...] = (acc + b_ref[...]).reshape(o_ref.shape)


def _ada(c, ada_w, ada_b):
    bsz, d = c.shape
    n = ada_w.shape[1]
    bn = 1024
    return pl.pallas_call(
        _ada_kernel,
        grid=(n // bn,),
        in_specs=[pl.BlockSpec((bsz, d), lambda j: (0, 0)),
                  pl.BlockSpec((d, bn), lambda j: (0, j)),
                  pl.BlockSpec((1, bn), lambda j: (0, j))],
        out_specs=pl.BlockSpec((bsz, 1, bn), lambda j: (0, 0, j)),
        out_shape=jax.ShapeDtypeStruct((bsz, 1, n), F32),
        name="ada_mod",
    )(c, ada_w, ada_b.reshape(1, n))


def _gelu_from_half(hx):
    c = (2.0 / jnp.pi) ** 0.5
    u = hx * (2.0 * c + (8.0 * c * 0.044715) * (hx * hx))
    return hx + hx * jnp.tanh(u)


def _inproj_rglru_kernel(x_ref, mod_ref, n1g_ref, win_ref, cw_ref, cb_ref, wa_ref, wx_ref,
                         ba_ref, bx_ref, lam_ref, *rest, d_model, d_rnn, d_att, conv_w, n_cast):
    cast_in = rest[:n_cast]
    qkv_ref, yr_ref = rest[n_cast:n_cast + 2]
    cast_out = rest[n_cast + 2:2 * n_cast + 2]
    tail_s, hc, win_s, wg_s = rest[2 * n_cast + 2:]
    i = pl.program_id(0)

    for src, dst in zip(cast_in, cast_out):
        dst[...] = src[...].astype(BF16)
    halo = (conv_w - 1) * SUBLANES
    prow = RT * SUBLANES
    half = d_rnn // 2

    @pl.when(i == 0)
    def _():
        tail_s[...] = jnp.zeros_like(tail_s)
        hc[...] = jnp.zeros_like(hc)
        for c0 in range(0, win_ref.shape[1], PW):
            wcol = win_ref[:, c0:c0 + PW]
            if 2 * d_rnn <= c0 < 2 * d_rnn + d_att:
                wcol = wcol * (LOG2E * HEAD_DIM ** -0.5)
            elif d_rnn <= c0 < 2 * d_rnn:
                wcol = wcol * 0.5
            win_s[:, c0:c0 + PW] = wcol.astype(BF16)
        wg_s[...] = jnp.zeros_like(wg_s)
        nb, bw = wa_ref.shape[0], wa_ref.shape[1]
        per_half = nb // 2
        for n in range(nb):
            hf, k = divmod(n, per_half)
            rs = slice(k * bw, (k + 1) * bw)
            wg_s[hf, rs, k * bw:(k + 1) * bw] = wa_ref[n].astype(BF16)
            wg_s[hf, rs, half + k * bw:half + (k + 1) * bw] = wx_ref[n].astype(BF16)

    gain1 = n1g_ref[...] * (1.0 + mod_ref[:, :, d_model:2 * d_model])
    lam = lam_ref[...]
    log_sig = jnp.minimum(lam, 0.0) - jnp.log1p(jnp.exp(-jnp.abs(lam)))
    neg_rate_half = (-0.5 * RG_C) * log_sig
    rate2_half = -LOG2E * neg_rate_half
    cw_half = 0.5 * cw_ref[...]
    cb_half = 0.5 * cb_ref[...]
    ba_half = 0.5 * ba_ref[...]
    bx_half = 0.5 * bx_ref[...]
    sh1 = mod_ref[:, :, 0:d_model]
    trow = TT * SUBLANES
    n_rec = TT // RT

    def normed_lhs(t):
        x = x_ref[:, t * TT:(t + 1) * TT, :]
        ms = jnp.mean(x * x, axis=-1, keepdims=True)
        h = x * lax.rsqrt(ms + EPS) * gain1 + sh1
        return h.reshape(trow, d_model).astype(BF16)

    def project_cols(hb, lo, hi):
        part = jnp.dot(hb, win_s[:, lo:hi], preferred_element_type=F32)
        return part.reshape(SUBLANES, TT, hi - lo)

    def branch(hb, j):
        return [project_cols(hb, j * d_rnn + p, j * d_rnn + p + PW) for p in range(0, d_rnn, PW)]

    def emit_qkv(t, hb, p):
        o_q = 2 * d_rnn
        part = project_cols(hb, o_q + p * PW, o_q + (p + 1) * PW)
        qkv_ref[:, t * TT:(t + 1) * TT, p * PW:(p + 1) * PW] = part.astype(BF16)

    def recur_piece(t, xr_p, gr_p, c, tail, hcur):
        ts = slice(c * RT, (c + 1) * RT)
        xr = jnp.concatenate([jnp.swapaxes(p[:, ts, :], 0, 1).reshape(prow, PW) for p in xr_p], axis=1)
        ext = jnp.concatenate([tail, xr], axis=0)
        xh = cb_half
        for k in range(conv_w):
            xh = xh + ext[k * SUBLANES:k * SUBLANES + prow, :] * cw_half[k:k + 1, :]
        xhb = xh.astype(BF16)
        g0 = jnp.dot(xhb[:, 0:half], wg_s[0], preferred_element_type=F32)
        g1 = jnp.dot(xhb[:, half:d_rnn], wg_s[1], preferred_element_type=F32)
        th_r = jnp.tanh(jnp.concatenate([g0[:, 0:half], g1[:, 0:half]], axis=1) + ba_half)
        th_i = jnp.tanh(jnp.concatenate([g0[:, half:], g1[:, half:]], axis=1) + bx_half)
        a = jnp.exp2(th_r * rate2_half + rate2_half)
        w = jnp.tanh(th_r * neg_rate_half + neg_rate_half) * (a * a + 1.0)
        mult = jnp.where(w > 0.0, w * lax.rsqrt(w), 0.0)
        a3 = a.reshape(RT, SUBLANES, d_rnn)
        b3 = (mult * (th_i * xh + xh)).reshape(RT, SUBLANES, d_rnn)
        hs = []
        for s in range(RT):
            hcur = a3[s] * hcur + b3[s]
            hs.append(hcur)
        gate = jnp.concatenate([_gelu_from_half(p[:, ts, :]) for p in gr_p], axis=2)
        yr = jnp.swapaxes(jnp.stack(hs, axis=0), 0, 1) * gate
        yr_ref[:, t * TT + c * RT:t * TT + (c + 1) * RT, :] = yr.astype(BF16)
        return ext[prow:prow + halo, :], hcur

    n_tiles = T1 // TT
    n_qkv = (3 * d_att) // PW
    tail, hcur = tail_s[...], hc[...]
    hb = normed_lhs(0)
    xr_p, gr_p = branch(hb, 0), branch(hb, 1)
    for t in range(n_tiles):
        nxt = t + 1 < n_tiles
        ahead = {}
        for c in range(max(n_rec, n_qkv)):
            if c < n_qkv:
                emit_qkv(t, hb, c)
            if c < n_rec:
                tail, hcur = recur_piece(t, xr_p, gr_p, c, tail, hcur)
            if nxt and c == n_rec - 3:
                ahead["hb"] = normed_lhs(t + 1)
            if nxt and c == n_rec - 2:
                ahead["xr"] = branch(ahead["hb"], 0)
            if nxt and c == n_rec - 1:
                ahead["gr"] = branch(ahead["hb"], 1)
        if nxt:
            hb, xr_p, gr_p = ahead["hb"], ahead["xr"], ahead["gr"]
    tail_s[...] = tail
    hc[...] = hcur


def _inproj_rglru(x, mod, n1g, w_in, conv_w, conv_b, wa, wx, ba, bx, lam, later_weights):
    bsz, seq, d = x.shape
    assert bsz == SUBLANES and seq % T1 == 0
    d_in = w_in.shape[1]
    d_rnn = conv_w.shape[1]
    d_att = (d_in - 2 * d_rnn) // 3
    assert d_att == d_rnn and T1 % TT == 0 and TT % RT == 0
    cw = conv_w.shape[0]
    const = lambda i: (0, 0)
    n_steps = seq // T1
    assert all(w.shape[0] % (2 * SUBLANES * n_steps) == 0 for w in later_weights)
    cast_specs = [pl.BlockSpec((w.shape[0] // n_steps, w.shape[1]), lambda i: (i, 0))
                  for w in later_weights]
    kern = functools.partial(_inproj_rglru_kernel, d_model=d, d_rnn=d_rnn, d_att=d_att, conv_w=cw,
                             n_cast=len(later_weights))
    outs = pl.pallas_call(
        kern,
        grid=(n_steps,),
        in_specs=[
            pl.BlockSpec((bsz, T1, d), lambda i: (0, i, 0)),
            _single(mod.shape, lambda i: (0, 0, 0)),
            _single((1, d), const),
            _single(w_in.shape, const),
            _single(conv_w.shape, const),
            _single((1, d_rnn), const),
            _single(wa.shape, lambda i: (0, 0, 0)),
            _single(wx.shape, lambda i: (0, 0, 0)),
            _single((1, d_rnn), const),
            _single((1, d_rnn), const),
            _single((1, d_rnn), const),
        ] + cast_specs,
        out_specs=[pl.BlockSpec((bsz, T1, 3 * d_att), lambda i: (0, i, 0)),
                   pl.BlockSpec((bsz, T1, d_rnn), lambda i: (0, i, 0))] + cast_specs,
        out_shape=[jax.ShapeDtypeStruct((bsz, seq, 3 * d_att), BF16),
                   jax.ShapeDtypeStruct((bsz, seq, d_rnn), BF16)]
        + [jax.ShapeDtypeStruct(w.shape, BF16) for w in later_weights],
        scratch_shapes=[
            pltpu.VMEM(((cw - 1) * SUBLANES, d_rnn), F32),
            pltpu.VMEM((SUBLANES, d_rnn), F32),
            pltpu.VMEM(w_in.shape, BF16),
            pltpu.VMEM((2, d_rnn // 2, d_rnn), BF16),
        ],
        compiler_params=pltpu.CompilerParams(dimension_semantics=("arbitrary",),
                                             vmem_limit_bytes=VMEM_LIMIT_BYTES),
        name="inproj_rglru",
    )(x, mod, n1g.reshape(1, d), w_in, conv_w, conv_b.reshape(1, d_rnn), wa, wx,
      ba.reshape(1, d_rnn), bx.reshape(1, d_rnn), lam.reshape(1, d_rnn), *later_weights)
    return outs[0], outs[1], outs[2:]


def _build_bias_tables(rbp_ref, bias_s, n_heads):
    r_io = lax.broadcasted_iota(jnp.int32, (RBPAD, GPAD), 0)
    j_io = lax.broadcasted_iota(jnp.int32, (RBPAD, GPAD), 1)
    idx = jnp.clip(HIST + CHUNK - 1 - j_io, -REL_CLIP, REL_CLIP) + REL_CLIP
    sel = (r_io == idx).astype(F32)
    g = jnp.dot(rbp_ref[...], sel, precision=lax.Precision.HIGHEST, preferred_element_type=F32)
    g = (g - g[:, 0:1]) * LOG2E
    col = lax.broadcasted_iota(jnp.int32, (CHUNK, WIN), 1)
    band = (LOOKBACK + 1) * CHUNK
    for h in range(n_heads):
        gh = jnp.broadcast_to(g[h:h + 1, :], (CHUNK, GPAD))
        even = pltpu.roll(gh, GPAD - (CHUNK - 1), 1, stride=1, stride_axis=0)[:, 0:WIN]
        odd = pltpu.roll(gh, 1, 1, stride=1, stride_axis=0)[:, 0:WIN]
        bias_s[h, 0] = jnp.where(col < band, even, NEG_INF)
        bias_s[h, 1] = jnp.where(col >= CHUNK, odd, NEG_INF)


ZERO_BIAS_COLS = ((0, HIST - REL_CLIP), (LANES, HIST - REL_CLIP))


def _attn_tile(q_ref, k_refs, v_refs, o_ref, bias_s, skip, first_valid, n_heads):
    lane = lax.broadcasted_iota(jnp.int32, (1, LANES), 1)
    n_kt = len(k_refs)
    base = skip * TQ
    n_p = n_kt - skip

    def masked_q(h):
        pair, hh = divmod(h, HEADS_PER_VREG)
        qp = q_ref[:, pair * LANES:(pair + 1) * LANES]
        in_head = jnp.logical_and(lane >= hh * HEAD_DIM, lane < (hh + 1) * HEAD_DIM)
        return jnp.where(in_head, qp, jnp.zeros_like(qp))

    def score_piece(h, qm, j):
        pair = h // HEADS_PER_VREG
        return lax.dot_general(qm, k_refs[skip + j][:, pair * LANES:(pair + 1) * LANES],
                               (((1,), (1,)), ((), ())), preferred_element_type=F32)

    def softmax_piece(h, s_all, c):
        w0 = (c // 2) * LANES
        lo = max(base - w0, 0)
        raw = s_all[c * CHUNK:(c + 1) * CHUNK, w0 + lo - base:w0 + WIN - base]
        z0, z1 = ZERO_BIAS_COLS[c % 2]
        z0 = max(z0, lo)
        parts = []
        if z0 > lo:
            parts.append(raw[:, 0:z0 - lo] + bias_s[h, c % 2, :, lo:z0])
        if z1 > z0:
            parts.append(raw[:, z0 - lo:z1 - lo])
        hi0 = max(z1, lo)
        parts.append(raw[:, hi0 - lo:WIN - lo] + bias_s[h, c % 2, :, hi0:WIN])
        s = jnp.concatenate(parts, axis=1)
        if first_valid is not None:
            col = lax.broadcasted_iota(jnp.int32, (1, WIN - lo), 1) + (w0 + lo)
            s = jnp.where(col >= first_valid, s, NEG_INF)
        m = jnp.max(s, axis=-1, keepdims=True)
        p = jnp.exp2(s - m)
        inv = 1.0 / jnp.sum(p, axis=-1, keepdims=True)
        row = [p.astype(BF16)]
        if w0 + lo > base:
            row.insert(0, jnp.zeros((CHUNK, w0 + lo - base), BF16))
        if w0 + WIN < NBUF:
            row.append(jnp.zeros((CHUNK, NBUF - w0 - WIN), BF16))
        return inv, jnp.concatenate(row, axis=1)

    def value_piece(h, pmat, j):
        pair = h // HEADS_PER_VREG
        return jnp.dot(pmat[:, j * TQ:(j + 1) * TQ],
                       v_refs[skip + j][:, pair * LANES:(pair + 1) * LANES],
                       preferred_element_type=F32)

    def finish(h, o, inv, o_prev):
        pair, hh = divmod(h, HEADS_PER_VREG)
        o = o * inv
        if hh == 0:
            return o
        in_head = jnp.logical_and(lane >= hh * HEAD_DIM, lane < (hh + 1) * HEAD_DIM)
        o_ref[:, pair * LANES:(pair + 1) * LANES] = jnp.where(in_head, o, o_prev).astype(BF16)
        return None

    assert 0 < n_p <= CHUNKS_PER_TILE and HEADS_PER_VREG == 2
    qm = masked_q(0)
    s_next = jnp.concatenate([score_piece(0, qm, j) for j in range(n_p)], axis=1)
    o_pair = None
    prev = None
    for h in range(n_heads + 1):
        cur = h < n_heads
        nxt = h + 1 < n_heads
        s_cur = s_next
        if nxt:
            qm = masked_q(h + 1)
        pieces, invs, prow, o_acc = [], [], [], None
        for c in range(CHUNKS_PER_TILE):
            if nxt and c < n_p:
                pieces.append(score_piece(h + 1, qm, c))
            if prev is not None and c < n_p:
                part = value_piece(prev[0], prev[1], c)
                o_acc = part if o_acc is None else o_acc + part
            if cur:
                inv, pr = softmax_piece(h, s_cur, c)
                invs.append(inv)
                prow.append(pr)
        if prev is not None:
            o_pair = finish(prev[0], o_acc, prev[2], o_pair)
        if nxt:
            s_next = jnp.concatenate(pieces, axis=1)
        prev = (h, jnp.concatenate(prow, axis=0), jnp.concatenate(invs, axis=0)) if cur else None


def _attn_kernel(*refs, n_heads):
    n_win = NBUF // TQ
    q_ref = refs[0]
    k_refs = refs[1:1 + n_win]
    v_refs = refs[1 + n_win:1 + 2 * n_win]
    rbp_ref, o_ref, bias_s = refs[1 + 2 * n_win:]
    b = pl.program_id(0)
    t = pl.program_id(1)

    @pl.when(jnp.logical_and(b == 0, t == 0))
    def _():
        _build_bias_tables(rbp_ref, bias_s, n_heads)

    @pl.when(t == 0)
    def _():
        _attn_tile(q_ref, k_refs, v_refs, o_ref, bias_s, n_win - 1, None, n_heads)

    @pl.when(jnp.logical_and(t > 0, t < n_win - 1))
    def _():
        _attn_tile(q_ref, k_refs, v_refs, o_ref, bias_s, 0, TQ * (n_win - 1 - t), n_heads)

    @pl.when(t >= n_win - 1)
    def _():
        _attn_tile(q_ref, k_refs, v_refs, o_ref, bias_s, 0, None, n_heads)


def _attention(qkv, rbp):
    bsz, seq, d3 = qkv.shape
    d_att = d3 // 3
    n_heads = d_att // HEAD_DIM
    n_win = NBUF // TQ
    assert TQ == MXU_DIM and seq % TQ == 0 and NBUF % TQ == 0 and n_heads % HEADS_PER_VREG == 0
    kern = functools.partial(_attn_kernel, n_heads=n_heads)

    def window(col, j):
        return pl.BlockSpec((None, TQ, d_att),
                            lambda b, t: (b, jnp.maximum(t - (n_win - 1 - j), 0), col))

    return pl.pallas_call(
        kern,
        grid=(bsz, seq // TQ),
        in_specs=([pl.BlockSpec((None, TQ, d_att), lambda b, t: (b, t, 0))]
                  + [window(1, j) for j in range(n_win)]
                  + [window(2, j) for j in range(n_win)]
                  + [_single(rbp.shape, lambda b, t: (0, 0))]),
        out_specs=pl.BlockSpec((None, TQ, d_att), lambda b, t: (b, t, 0)),
        out_shape=jax.ShapeDtypeStruct((bsz, seq, d_att), BF16),
        scratch_shapes=[pltpu.VMEM((n_heads, 2, CHUNK, WIN), F32)],
        compiler_params=pltpu.CompilerParams(dimension_semantics=("arbitrary", "arbitrary"),
                                             vmem_limit_bytes=VMEM_LIMIT_BYTES),
        name="chunk_attn",
    )(*([qkv] * (1 + 2 * n_win)), rbp)


def _outproj_ffn_kernel(x_ref, yr_ref, ya_ref, mod_ref, n2g_ref, fg_ref, wo_ref,
                        wu_ref, fcw_ref, fcb_ref, wd_ref, o_ref, halo_s,
                        *, d_model, d_ff, conv_w):
    i = pl.program_id(0)
    rows = TT * SUBLANES
    halo = (conv_w - 1) * SUBLANES
    n_hp = TT // HT

    @pl.when(i == 0)
    def _():
        halo_s[...] = jnp.zeros_like(halo_s)

    g1 = mod_ref[:, :, 2 * d_model:3 * d_model]
    sh2 = mod_ref[:, :, 3 * d_model:4 * d_model]
    gain2 = n2g_ref[...] * (1.0 + mod_ref[:, :, 4 * d_model:5 * d_model])
    g2 = mod_ref[:, :, 5 * d_model:6 * d_model]

    def out_proj(t):
        ts = slice(t * TT, (t + 1) * TT)
        yr = yr_ref[:, ts, :].reshape(rows, yr_ref.shape[2])
        ya = ya_ref[:, ts, :].reshape(rows, ya_ref.shape[2])
        d_rnn = yr_ref.shape[2]
        y = (jnp.dot(yr, wo_ref[0:d_rnn, :], preferred_element_type=F32)
             + jnp.dot(ya, wo_ref[d_rnn:, :], preferred_element_type=F32))
        return y.reshape(SUBLANES, TT, d_model)

    def head_piece(t, c, y3):
        cs = slice(c * HT, (c + 1) * HT)
        x1 = x_ref[:, t * TT + c * HT:t * TT + (c + 1) * HT, :] + g1 * y3[:, cs, :]
        ms = jnp.mean(x1 * x1, axis=-1, keepdims=True)
        hn = x1 * lax.rsqrt(ms + EPS) * gain2 + sh2
        return x1, jnp.swapaxes(hn, 0, 1).reshape(HT * SUBLANES, d_model).astype(BF16)

    def tail_piece(t, c, z, x1):
        zc = z[c * HT * SUBLANES:(c + 1) * HT * SUBLANES, :].reshape(HT, SUBLANES, d_model)
        x2 = x1 + g2 * jnp.swapaxes(zc, 0, 1)
        ms2 = jnp.mean(x2 * x2, axis=-1, keepdims=True)
        o_ref[:, t * TT + c * HT:t * TT + (c + 1) * HT, :] = x2 * lax.rsqrt(ms2 + EPS) * fg_ref[...]

    def up_piece(hb, cols):
        return jnp.dot(hb, wu_ref[:, cols], preferred_element_type=F32)

    def conv_piece(up, tail, cols, scale):
        ext = jnp.concatenate([tail, up], axis=0)
        u = scale * fcb_ref[:, cols]
        for k in range(conv_w):
            u = u + ext[k * SUBLANES:k * SUBLANES + rows, :] * (scale * fcw_ref[k:k + 1, cols])
        return u, ext[rows:rows + halo, :]

    def ffn(hb, tails, side):
        side = list(side)
        z = None
        new_tails = []
        for j, (c0, w) in enumerate(FF_SPLITS):
            cg = slice(c0, c0 + w)
            cv = slice(d_ff + c0, d_ff + c0 + w)
            up_g = up_piece(hb, cg)
            if side:
                side.pop(0)()
            up_v = up_piece(hb, cv)
            if side:
                side.pop(0)()
            hg, tg = conv_piece(up_g, tails[j][0], cg, 0.5)
            uv, tv = conv_piece(up_v, tails[j][1], cv, 1.0)
            new_tails.append((tg, tv))
            act = ((hg + hg * jnp.tanh(hg)) * uv).astype(BF16)
            zc = jnp.dot(act, wd_ref[c0:c0 + w, :], preferred_element_type=F32)
            z = zc if z is None else z + zc
        assert not side
        return z, new_tails

    tails = [(halo_s[:, c0:c0 + w], halo_s[:, d_ff + c0:d_ff + c0 + w]) for c0, w in FF_SPLITS]
    y3 = out_proj(0)
    hp = [head_piece(0, c, y3) for c in range(n_hp)]
    x1_0 = [p[0] for p in hp]
    hb_0 = jnp.concatenate([p[1] for p in hp], axis=0)

    state = {}

    def side_out_proj():
        state["y3"] = out_proj(1)

    def side_head(c):
        def run():
            state.setdefault("hp", []).append(head_piece(1, c, state["y3"]))
        return run

    z_0, tails = ffn(hb_0, tails, [side_out_proj] + [side_head(c) for c in range(n_hp)])
    x1_1 = [p[0] for p in state["hp"]]
    hb_1 = jnp.concatenate([p[1] for p in state["hp"]], axis=0)

    def side_tail(c):
        return lambda: tail_piece(0, c, z_0, x1_0[c])

    z_1, tails = ffn(hb_1, tails, [side_tail(c) for c in range(n_hp)])
    for c in range(n_hp):
        tail_piece(1, c, z_1, x1_1[c])
    for (c0, w), (tg, tv) in zip(FF_SPLITS, tails):
        halo_s[:, c0:c0 + w] = tg
        halo_s[:, d_ff + c0:d_ff + c0 + w] = tv


def _outproj_ffn(x, yr, ya, mod, n2g, fg, w_out_b, w_up_b, fcw, fcb, w_down_b):
    bsz, seq, d = x.shape
    d_ff = w_down_b.shape[0]
    cw = fcw.shape[0]
    d_rnn = yr.shape[2]
    d_att = ya.shape[2]
    step_t = K3_TILES * TT
    assert K3_TILES == 2 and sum(w for _, w in FF_SPLITS) == d_ff
    assert seq % step_t == 0 and TT % HT == 0
    const = lambda i: (0, 0)
    kern = functools.partial(_outproj_ffn_kernel, d_model=d, d_ff=d_ff, conv_w=cw)
    return pl.pallas_call(
        kern,
        grid=(seq // step_t,),
        in_specs=[
            pl.BlockSpec((bsz, step_t, d), lambda i: (0, i, 0)),
            pl.BlockSpec((bsz, step_t, d_rnn), lambda i: (0, i, 0)),
            pl.BlockSpec((bsz, step_t, d_att), lambda i: (0, i, 0)),
            _single(mod.shape, lambda i: (0, 0, 0)),
            _single((1, d), const),
            _single((1, d), const),
            _single(w_out_b.shape, const),
            _single(w_up_b.shape, const),
            _single(fcw.shape, const),
            _single((1, 2 * d_ff), const),
            _single(w_down_b.shape, const),
        ],
        out_specs=pl.BlockSpec((bsz, step_t, d), lambda i: (0, i, 0)),
        out_shape=jax.ShapeDtypeStruct((bsz, seq, d), F32),
        scratch_shapes=[pltpu.VMEM(((cw - 1) * SUBLANES, 2 * d_ff), F32)],
        compiler_params=pltpu.CompilerParams(dimension_semantics=("arbitrary",),
                                             vmem_limit_bytes=VMEM_LIMIT_BYTES),
        name="outproj_ffn",
    )(x, yr, ya, mod, n2g.reshape(1, d), fg.reshape(1, d), w_out_b, w_up_b, fcw,
      fcb.reshape(1, 2 * d_ff), w_down_b)


def kernel(x, c, ada_w, ada_b, norm1_g, w_in, rnn_conv_w, rnn_conv_b, rg_wa, rg_ba, rg_wx, rg_bx, rg_lambda, rel_bias, w_out, norm2_g, w_up, ffn_conv_w, ffn_conv_b, w_down, final_g):
    assert ada_w.shape[0] == 1
    l = 0
    mod = _ada(c, ada_w[l], ada_b[l])
    qkv, yr, (w_out_b, w_up_b, w_down_b) = _inproj_rglru(
        x, mod, norm1_g[l], w_in[l], rnn_conv_w[l], rnn_conv_b[l], rg_wa[l], rg_wx[l],
        rg_ba[l], rg_bx[l], rg_lambda[l], (w_out[l], w_up[l], w_down[l]))
    rbp = jnp.pad(rel_bias[l], ((0, 0), (0, RBPAD - rel_bias.shape[2])))
    ya = _attention(qkv, rbp)
    return _outproj_ffn(x, yr, ya, mod, norm2_g[l], final_g, w_out_b, w_up_b,
                        ffn_conv_w[l], ffn_conv_b[l], w_down_b)
```

```python
import functools

import jax
import jax.numpy as jnp
from jax import lax
from jax.experimental import pallas as pl
from jax.experimental.pallas import tpu as pltpu

F32 = jnp.float32
BF16 = jnp.bfloat16

CHUNK = 64
LOOKBACK = 8
REL_CLIP = 128
HEAD_DIM = 64
RG_C = 8.0
EPS = 1e-6
NEG_INF = -1e30
LOG2E = 1.4426950408889634
F32_TINY = 1.1754943508222875e-38

SUBLANES = 8
LANES = 128
MXU_DIM = 256
VMEM_LIMIT_BYTES = 56 * 1024 * 1024

TT = 64
RT = 8
PW = 256
T1 = 128
CHUNKS_PER_TILE = 4
TQ = CHUNKS_PER_TILE * CHUNK
HEADS_PER_VREG = LANES // HEAD_DIM
HIST = LOOKBACK * CHUNK
NBUF = HIST + TQ
WIN = (LOOKBACK + 2) * CHUNK
GPAD = 768
RBPAD = 384
FF_SPLITS = ((0, 1024), (1024, 1024), (2048, 768))
K3_TILES = 2
HT = 16


def _single(block_shape, index_map):
    return pl.BlockSpec(block_shape, index_map, pipeline_mode=pl.Buffered(1))


def _ada_kernel(c_ref, w_ref, b_ref, o_ref):
    c = c_ref[...]
    sc = c * jax.nn.sigmoid(c)
    w = w_ref[...]
    w_hi = w.astype(BF16)
    w_lo = (w - w_hi.astype(F32)).astype(BF16)
    sc_hi = sc.astype(BF16)
    sc_lo = (sc - sc_hi.astype(F32)).astype(BF16)
    acc = jnp.dot(sc_hi, w_hi, preferred_element_type=F32)
    acc = acc + jnp.dot(sc_lo, w_hi, preferred_element_type=F32)
    acc = acc + jnp.dot(sc_hi, w_lo, preferred_element_type=F32)
    o_ref[...] = (acc + b_ref[...]).reshape(o_ref.shape)


def _ada(c, ada_w, ada_b):
    bsz, d = c.shape
    n = ada_w.shape[1]
    bn = 1024
    return pl.pallas_call(
        _ada_kernel,
        grid=(n // bn,),
        in_specs=[pl.BlockSpec((bsz, d), lambda j: (0, 0)),
                  pl.BlockSpec((d, bn), lambda j: (0, j)),
                  pl.BlockSpec((1, bn), lambda j: (0, j))],
        out_specs=pl.BlockSpec((bsz, 1, bn), lambda j: (0, 0, j)),
        out_shape=jax.ShapeDtypeStruct((bsz, 1, n), F32),
        name="ada_mod",
    )(c, ada_w, ada_b.reshape(1, n))


def _gelu_from_half(hx):
    c = (2.0 / jnp.pi) ** 0.5
    u = hx * (2.0 * c + (8.0 * c * 0.044715) * (hx * hx))
    return hx + hx * jnp.tanh(u)


def _inproj_rglru_kernel(x_ref, mod_ref, n1g_ref, win_ref, cw_ref, cb_ref, wa_ref, wx_ref,
                         ba_ref, bx_ref, lam_ref, *rest, d_model, d_rnn, d_att, conv_w, n_cast):
    cast_in = rest[:n_cast]
    qkv_ref, yr_ref = rest[n_cast:n_cast + 2]
    cast_out = rest[n_cast + 2:2 * n_cast + 2]
    tail_s, hc, win_s, wg_s = rest[2 * n_cast + 2:]
    i = pl.program_id(0)

    for src, dst in zip(cast_in, cast_out):
        dst[...] = src[...].astype(BF16)
    halo = (conv_w - 1) * SUBLANES
    prow = RT * SUBLANES
    half = d_rnn // 2

    @pl.when(i == 0)
    def _():
        tail_s[...] = jnp.zeros_like(tail_s)
        hc[...] = jnp.zeros_like(hc)
        for c0 in range(0, win_ref.shape[1], PW):
            wcol = win_ref[:, c0:c0 + PW]
            if 2 * d_rnn <= c0 < 2 * d_rnn + d_att:
                wcol = wcol * (LOG2E * HEAD_DIM ** -0.5)
            elif d_rnn <= c0 < 2 * d_rnn:
                wcol = wcol * 0.5
            win_s[:, c0:c0 + PW] = wcol.astype(BF16)
        wg_s[...] = jnp.zeros_like(wg_s)
        nb, bw = wa_ref.shape[0], wa_ref.shape[1]
        per_half = nb // 2
        for n in range(nb):
            hf, k = divmod(n, per_half)
            rs = slice(k * bw, (k + 1) * bw)
            wg_s[hf, rs, k * bw:(k + 1) * bw] = wa_ref[n].astype(BF16)
            wg_s[hf, rs, half + k * bw:half + (k + 1) * bw] = wx_ref[n].astype(BF16)

    gain1 = n1g_ref[...] * (1.0 + mod_ref[:, :, d_model:2 * d_model])
    lam = lam_ref[...]
    log_sig = jnp.minimum(lam, 0.0) - jnp.log1p(jnp.exp(-jnp.abs(lam)))
    neg_rate_half = (-0.5 * RG_C) * log_sig
    rate2_half = -LOG2E * neg_rate_half
    cw_half = 0.5 * cw_ref[...]
    cb_half = 0.5 * cb_ref[...]
    ba_half = 0.5 * ba_ref[...]
    bx_half = 0.5 * bx_ref[...]
    sh1 = mod_ref[:, :, 0:d_model]
    trow = TT * SUBLANES
    n_rec = TT // RT

    def normed_lhs(t):
        x = x_ref[:, t * TT:(t + 1) * TT, :]
        ms = jnp.mean(x * x, axis=-1, keepdims=True)
        h = x * lax.rsqrt(ms + EPS) * gain1 + sh1
        return h.reshape(trow, d_model).astype(BF16)

    def project_cols(hb, lo, hi):
        part = jnp.dot(hb, win_s[:, lo:hi], preferred_element_type=F32)
        return part.reshape(SUBLANES, TT, hi - lo)

    def branch(hb, j):
        return [project_cols(hb, j * d_rnn + p, j * d_rnn + p + PW) for p in range(0, d_rnn, PW)]

    def emit_qkv(t, hb, p):
        o_q = 2 * d_rnn
        part = project_cols(hb, o_q + p * PW, o_q + (p + 1) * PW)
        qkv_ref[:, t * TT:(t + 1) * TT, p * PW:(p + 1) * PW] = part.astype(BF16)

    def recur_piece(t, xr_p, gr_p, c, tail, hcur):
        ts = slice(c * RT, (c + 1) * RT)
        xr = jnp.concatenate([jnp.swapaxes(p[:, ts, :], 0, 1).reshape(prow, PW) for p in xr_p], axis=1)
        ext = jnp.concatenate([tail, xr], axis=0)
        xh = cb_half
        for k in range(conv_w):
            xh = xh + ext[k * SUBLANES:k * SUBLANES + prow, :] * cw_half[k:k + 1, :]
        xhb = xh.astype(BF16)
        g0 = jnp.dot(xhb[:, 0:half], wg_s[0], preferred_element_type=F32)
        g1 = jnp.dot(xhb[:, half:d_rnn], wg_s[1], preferred_element_type=F32)
        th_r = jnp.tanh(jnp.concatenate([g0[:, 0:half], g1[:, 0:half]], axis=1) + ba_half)
        th_i = jnp.tanh(jnp.concatenate([g0[:, half:], g1[:, half:]], axis=1) + bx_half)
        a = jnp.exp2(th_r * rate2_half + rate2_half)
        w = jnp.tanh(th_r * neg_rate_half + neg_rate_half) * (a * a + 1.0)
        mult = w * lax.rsqrt(jnp.maximum(w, F32_TINY))
        a3 = a.reshape(RT, SUBLANES, d_rnn)
        b3 = (mult * (th_i * xh + xh)).reshape(RT, SUBLANES, d_rnn)
        hs = []
        for s in range(RT):
            hcur = a3[s] * hcur + b3[s]
            hs.append(hcur)
        gate = jnp.concatenate([_gelu_from_half(p[:, ts, :]) for p in gr_p], axis=2)
        yr = jnp.swapaxes(jnp.stack(hs, axis=0), 0, 1) * gate
        yr_ref[:, t * TT + c * RT:t * TT + (c + 1) * RT, :] = yr.astype(BF16)
        return ext[prow:prow + halo, :], hcur

    n_tiles = T1 // TT
    n_qkv = (3 * d_att) // PW
    tail, hcur = tail_s[...], hc[...]
    hb = normed_lhs(0)
    xr_p, gr_p = branch(hb, 0), branch(hb, 1)
    for t in range(n_tiles):
        nxt = t + 1 < n_tiles
        ahead = {}
        for c in range(max(n_rec, n_qkv)):
            if c < n_qkv:
                emit_qkv(t, hb, c)
            if c < n_rec:
                tail, hcur = recur_piece(t, xr_p, gr_p, c, tail, hcur)
            if nxt and c == n_rec - 3:
                ahead["hb"] = normed_lhs(t + 1)
            if nxt and c == n_rec - 2:
                ahead["xr"] = branch(ahead["hb"], 0)
            if nxt and c == n_rec - 1:
                ahead["gr"] = branch(ahead["hb"], 1)
        if nxt:
            hb, xr_p, gr_p = ahead["hb"], ahead["xr"], ahead["gr"]
    tail_s[...] = tail
    hc[...] = hcur


def _inproj_rglru(x, mod, n1g, w_in, conv_w, conv_b, wa, wx, ba, bx, lam, later_weights):
    bsz, seq, d = x.shape
    assert bsz == SUBLANES and seq % T1 == 0
    d_in = w_in.shape[1]
    d_rnn = conv_w.shape[1]
    d_att = (d_in - 2 * d_rnn) // 3
    assert d_att == d_rnn and T1 % TT == 0 and TT % RT == 0
    cw = conv_w.shape[0]
    const = lambda i: (0, 0)
    n_steps = seq // T1
    assert all(w.shape[0] % (2 * SUBLANES * n_steps) == 0 for w in later_weights)
    cast_specs = [pl.BlockSpec((w.shape[0] // n_steps, w.shape[1]), lambda i: (i, 0))
                  for w in later_weights]
    kern = functools.partial(_inproj_rglru_kernel, d_model=d, d_rnn=d_rnn, d_att=d_att, conv_w=cw,
                             n_cast=len(later_weights))
    outs = pl.pallas_call(
        kern,
        grid=(n_steps,),
        in_specs=[
            pl.BlockSpec((bsz, T1, d), lambda i: (0, i, 0)),
            _single(mod.shape, lambda i: (0, 0, 0)),
            _single((1, d), const),
            _single(w_in.shape, const),
            _single(conv_w.shape, const),
            _single((1, d_rnn), const),
            _single(wa.shape, lambda i: (0, 0, 0)),
            _single(wx.shape, lambda i: (0, 0, 0)),
            _single((1, d_rnn), const),
            _single((1, d_rnn), const),
            _single((1, d_rnn), const),
        ] + cast_specs,
        out_specs=[pl.BlockSpec((bsz, T1, 3 * d_att), lambda i: (0, i, 0)),
                   pl.BlockSpec((bsz, T1, d_rnn), lambda i: (0, i, 0))] + cast_specs,
        out_shape=[jax.ShapeDtypeStruct((bsz, seq, 3 * d_att), BF16),
                   jax.ShapeDtypeStruct((bsz, seq, d_rnn), BF16)]
        + [jax.ShapeDtypeStruct(w.shape, BF16) for w in later_weights],
        scratch_shapes=[
            pltpu.VMEM(((cw - 1) * SUBLANES, d_rnn), F32),
            pltpu.VMEM((SUBLANES, d_rnn), F32),
            pltpu.VMEM(w_in.shape, BF16),
            pltpu.VMEM((2, d_rnn // 2, d_rnn), BF16),
        ],
        compiler_params=pltpu.CompilerParams(dimension_semantics=("arbitrary",),
                                             vmem_limit_bytes=VMEM_LIMIT_BYTES),
        name="inproj_rglru",
    )(x, mod, n1g.reshape(1, d), w_in, conv_w, conv_b.reshape(1, d_rnn), wa, wx,
      ba.reshape(1, d_rnn), bx.reshape(1, d_rnn), lam.reshape(1, d_rnn), *later_weights)
    return outs[0], outs[1], outs[2:]


def _build_bias_tables(rbp_ref, bias_s, n_heads):
    r_io = lax.broadcasted_iota(jnp.int32, (RBPAD, GPAD), 0)
    j_io = lax.broadcasted_iota(jnp.int32, (RBPAD, GPAD), 1)
    idx = jnp.clip(HIST + CHUNK - 1 - j_io, -REL_CLIP, REL_CLIP) + REL_CLIP
    sel = (r_io == idx).astype(F32)
    g = jnp.dot(rbp_ref[...], sel, precision=lax.Precision.HIGHEST, preferred_element_type=F32)
    g = (g - g[:, 0:1]) * LOG2E
    col = lax.broadcasted_iota(jnp.int32, (CHUNK, WIN), 1)
    band = (LOOKBACK + 1) * CHUNK
    for h in range(n_heads):
        gh = jnp.broadcast_to(g[h:h + 1, :], (CHUNK, GPAD))
        even = pltpu.roll(gh, GPAD - (CHUNK - 1), 1, stride=1, stride_axis=0)[:, 0:WIN]
        odd = pltpu.roll(gh, 1, 1, stride=1, stride_axis=0)[:, 0:WIN]
        bias_s[h, 0] = jnp.where(col < band, even, NEG_INF)
        bias_s[h, 1] = jnp.where(col >= CHUNK, odd, NEG_INF)


ZERO_BIAS_COLS = ((0, HIST - REL_CLIP), (LANES, HIST - REL_CLIP))


def _attn_tile(q_ref, k_refs, v_refs, o_ref, bias_s, skip, first_valid, n_heads):
    lane = lax.broadcasted_iota(jnp.int32, (1, LANES), 1)
    n_kt = len(k_refs)
    base = skip * TQ
    n_p = n_kt - skip

    def masked_q(h):
        pair, hh = divmod(h, HEADS_PER_VREG)
        qp = q_ref[:, pair * LANES:(pair + 1) * LANES]
        in_head = jnp.logical_and(lane >= hh * HEAD_DIM, lane < (hh + 1) * HEAD_DIM)
        return jnp.where(in_head, qp, jnp.zeros_like(qp))

    def score_piece(h, qm, j):
        pair = h // HEADS_PER_VREG
        return lax.dot_general(qm, k_refs[skip + j][:, pair * LANES:(pair + 1) * LANES],
                               (((1,), (1,)), ((), ())), preferred_element_type=F32)

    def softmax_piece(h, s_all, c):
        w0 = (c // 2) * LANES
        lo = max(base - w0, 0)
        raw = s_all[c * CHUNK:(c + 1) * CHUNK, w0 + lo - base:w0 + WIN - base]
        z0, z1 = ZERO_BIAS_COLS[c % 2]
        z0 = max(z0, lo)
        parts = []
        if z0 > lo:
            parts.append(raw[:, 0:z0 - lo] + bias_s[h, c % 2, :, lo:z0])
        if z1 > z0:
            parts.append(raw[:, z0 - lo:z1 - lo])
        hi0 = max(z1, lo)
        parts.append(raw[:, hi0 - lo:WIN - lo] + bias_s[h, c % 2, :, hi0:WIN])
        s = jnp.concatenate(parts, axis=1)
        if first_valid is not None:
            col = lax.broadcasted_iota(jnp.int32, (1, WIN - lo), 1) + (w0 + lo)
            s = jnp.where(col >= first_valid, s, NEG_INF)
        m = jnp.max(s, axis=-1, keepdims=True)
        p = jnp.exp2(s - m)
        inv = 1.0 / jnp.sum(p, axis=-1, keepdims=True)
        row = [p.astype(BF16)]
        if w0 + lo > base:
            row.insert(0, jnp.zeros((CHUNK, w0 + lo - base), BF16))
        if w0 + WIN < NBUF:
            row.append(jnp.zeros((CHUNK, NBUF - w0 - WIN), BF16))
        return inv, jnp.concatenate(row, axis=1)

    def value_piece(h, pmat, j):
        pair = h // HEADS_PER_VREG
        return jnp.dot(pmat[:, j * TQ:(j + 1) * TQ],
                       v_refs[skip + j][:, pair * LANES:(pair + 1) * LANES],
                       preferred_element_type=F32)

    def finish(h, o, inv, o_prev):
        pair, hh = divmod(h, HEADS_PER_VREG)
        o = o * inv
        if hh == 0:
            return o
        in_head = jnp.logical_and(lane >= hh * HEAD_DIM, lane < (hh + 1) * HEAD_DIM)
        o_ref[:, pair * LANES:(pair + 1) * LANES] = jnp.where(in_head, o, o_prev).astype(BF16)
        return None

    assert 0 < n_p <= CHUNKS_PER_TILE and HEADS_PER_VREG == 2
    qm = masked_q(0)
    s_next = jnp.concatenate([score_piece(0, qm, j) for j in range(n_p)], axis=1)
    o_pair = None
    prev = None
    for h in range(n_heads + 1):
        cur = h < n_heads
        nxt = h + 1 < n_heads
        s_cur = s_next
        if nxt:
            qm = masked_q(h + 1)
        pieces, invs, prow, o_acc = [], [], [], None
        for c in range(CHUNKS_PER_TILE):
            if nxt and c < n_p:
                pieces.append(score_piece(h + 1, qm, c))
            if prev is not None and c < n_p:
                part = value_piece(prev[0], prev[1], c)
                o_acc = part if o_acc is None else o_acc + part
            if cur:
                inv, pr = softmax_piece(h, s_cur, c)
                invs.append(inv)
                prow.append(pr)
        if prev is not None:
            o_pair = finish(prev[0], o_acc, prev[2], o_pair)
        if nxt:
            s_next = jnp.concatenate(pieces, axis=1)
        prev = (h, jnp.concatenate(prow, axis=0), jnp.concatenate(invs, axis=0)) if cur else None


def _attn_kernel(*refs, n_heads):
    n_win = NBUF // TQ
    q_ref = refs[0]
    k_refs = refs[1:1 + n_win]
    v_refs = refs[1 + n_win:1 + 2 * n_win]
    rbp_ref, o_ref, bias_s = refs[1 + 2 * n_win:]
    b = pl.program_id(0)
    t = pl.program_id(1)

    @pl.when(jnp.logical_and(b == 0, t == 0))
    def _():
        _build_bias_tables(rbp_ref, bias_s, n_heads)

    @pl.when(t == 0)
    def _():
        _attn_tile(q_ref, k_refs, v_refs, o_ref, bias_s, n_win - 1, None, n_heads)

    @pl.when(jnp.logical_and(t > 0, t < n_win - 1))
    def _():
        _attn_tile(q_ref, k_refs, v_refs, o_ref, bias_s, 0, TQ * (n_win - 1 - t), n_heads)

    @pl.when(t >= n_win - 1)
    def _():
        _attn_tile(q_ref, k_refs, v_refs, o_ref, bias_s, 0, None, n_heads)


def _attention(qkv, rbp):
    bsz, seq, d3 = qkv.shape
    d_att = d3 // 3
    n_heads = d_att // HEAD_DIM
    n_win = NBUF // TQ
    assert TQ == MXU_DIM and seq % TQ == 0 and NBUF % TQ == 0 and n_heads % HEADS_PER_VREG == 0
    kern = functools.partial(_attn_kernel, n_heads=n_heads)

    def window(col, j):
        return pl.BlockSpec((None, TQ, d_att),
                            lambda b, t: (b, jnp.maximum(t - (n_win - 1 - j), 0), col))

    return pl.pallas_call(
        kern,
        grid=(bsz, seq // TQ),
        in_specs=([pl.BlockSpec((None, TQ, d_att), lambda b, t: (b, t, 0))]
                  + [window(1, j) for j in range(n_win)]
                  + [window(2, j) for j in range(n_win)]
                  + [_single(rbp.shape, lambda b, t: (0, 0))]),
        out_specs=pl.BlockSpec((None, TQ, d_att), lambda b, t: (b, t, 0)),
        out_shape=jax.ShapeDtypeStruct((bsz, seq, d_att), BF16),
        scratch_shapes=[pltpu.VMEM((n_heads, 2, CHUNK, WIN), F32)],
        compiler_params=pltpu.CompilerParams(dimension_semantics=("arbitrary", "arbitrary"),
                                             vmem_limit_bytes=VMEM_LIMIT_BYTES),
        name="chunk_attn",
    )(*([qkv] * (1 + 2 * n_win)), rbp)


def _outproj_ffn_kernel(x_ref, yr_ref, ya_ref, mod_ref, n2g_ref, fg_ref, wo_ref,
                        wu_ref, fcw_ref, fcb_ref, wd_ref, o_ref, halo_s,
                        *, d_model, d_ff, conv_w):
    i = pl.program_id(0)
    rows = TT * SUBLANES
    halo = (conv_w - 1) * SUBLANES
    n_hp = TT // HT

    @pl.when(i == 0)
    def _():
        halo_s[...] = jnp.zeros_like(halo_s)

    g1 = mod_ref[:, :, 2 * d_model:3 * d_model]
    sh2 = mod_ref[:, :, 3 * d_model:4 * d_model]
    gain2 = n2g_ref[...] * (1.0 + mod_ref[:, :, 4 * d_model:5 * d_model])
    g2 = mod_ref[:, :, 5 * d_model:6 * d_model]

    def out_proj(t):
        ts = slice(t * TT, (t + 1) * TT)
        yr = yr_ref[:, ts, :].reshape(rows, yr_ref.shape[2])
        ya = ya_ref[:, ts, :].reshape(rows, ya_ref.shape[2])
        d_rnn = yr_ref.shape[2]
        y = (jnp.dot(yr, wo_ref[0:d_rnn, :], preferred_element_type=F32)
             + jnp.dot(ya, wo_ref[d_rnn:, :], preferred_element_type=F32))
        return y.reshape(SUBLANES, TT, d_model)

    def head_piece(t, c, y3):
        cs = slice(c * HT, (c + 1) * HT)
        x1 = x_ref[:, t * TT + c * HT:t * TT + (c + 1) * HT, :] + g1 * y3[:, cs, :]
        ms = jnp.mean(x1 * x1, axis=-1, keepdims=True)
        hn = x1 * lax.rsqrt(ms + EPS) * gain2 + sh2
        return x1, jnp.swapaxes(hn, 0, 1).reshape(HT * SUBLANES, d_model).astype(BF16)

    def tail_piece(t, c, z, x1):
        zc = z[c * HT * SUBLANES:(c + 1) * HT * SUBLANES, :].reshape(HT, SUBLANES, d_model)
        x2 = x1 + g2 * jnp.swapaxes(zc, 0, 1)
        ms2 = jnp.mean(x2 * x2, axis=-1, keepdims=True)
        o_ref[:, t * TT + c * HT:t * TT + (c + 1) * HT, :] = x2 * lax.rsqrt(ms2 + EPS) * fg_ref[...]

    def up_piece(hb, cols):
        return jnp.dot(hb, wu_ref[:, cols], preferred_element_type=F32)

    def conv_piece(up, tail, cols, scale):
        ext = jnp.concatenate([tail, up], axis=0)
        u = scale * fcb_ref[:, cols]
        for k in range(conv_w):
            u = u + ext[k * SUBLANES:k * SUBLANES + rows, :] * (scale * fcw_ref[k:k + 1, cols])
        return u, ext[rows:rows + halo, :]

    def ffn(hb, tails, side):
        side = list(side)
        z = None
        new_tails = []
        for j, (c0, w) in enumerate(FF_SPLITS):
            cg = slice(c0, c0 + w)
            cv = slice(d_ff + c0, d_ff + c0 + w)
            up_g = up_piece(hb, cg)
            if side:
                side.pop(0)()
            up_v = up_piece(hb, cv)
            if side:
                side.pop(0)()
            hg, tg = conv_piece(up_g, tails[j][0], cg, 0.5)
            uv, tv = conv_piece(up_v, tails[j][1], cv, 1.0)
            new_tails.append((tg, tv))
            act = ((hg + hg * jnp.tanh(hg)) * uv).astype(BF16)
            zc = jnp.dot(act, wd_ref[c0:c0 + w, :], preferred_element_type=F32)
            z = zc if z is None else z + zc
        assert not side
        return z, new_tails

    tails = [(halo_s[:, c0:c0 + w], halo_s[:, d_ff + c0:d_ff + c0 + w]) for c0, w in FF_SPLITS]
    y3 = out_proj(0)
    hp = [head_piece(0, c, y3) for c in range(n_hp)]
    x1_0 = [p[0] for p in hp]
    hb_0 = jnp.concatenate([p[1] for p in hp], axis=0)

    state = {}

    def side_out_proj():
        state["y3"] = out_proj(1)

    def side_head(c):
        def run():
            state.setdefault("hp", []).append(head_piece(1, c, state["y3"]))
        return run

    z_0, tails = ffn(hb_0, tails, [side_out_proj] + [side_head(c) for c in range(n_hp)])
    x1_1 = [p[0] for p in state["hp"]]
    hb_1 = jnp.concatenate([p[1] for p in state["hp"]], axis=0)

    def side_tail(c):
        return lambda: tail_piece(0, c, z_0, x1_0[c])

    z_1, tails = ffn(hb_1, tails, [side_tail(c) for c in range(n_hp)])
    for c in range(n_hp):
        tail_piece(1, c, z_1, x1_1[c])
    for (c0, w), (tg, tv) in zip(FF_SPLITS, tails):
        halo_s[:, c0:c0 + w] = tg
        halo_s[:, d_ff + c0:d_ff + c0 + w] = tv


def _outproj_ffn(x, yr, ya, mod, n2g, fg, w_out_b, w_up_b, fcw, fcb, w_down_b):
    bsz, seq, d = x.shape
    d_ff = w_down_b.shape[0]
    cw = fcw.shape[0]
    d_rnn = yr.shape[2]
    d_att = ya.shape[2]
    step_t = K3_TILES * TT
    assert K3_TILES == 2 and sum(w for _, w in FF_SPLITS) == d_ff
    assert seq % step_t == 0 and TT % HT == 0
    const = lambda i: (0, 0)
    kern = functools.partial(_outproj_ffn_kernel, d_model=d, d_ff=d_ff, conv_w=cw)
    return pl.pallas_call(
        kern,
        grid=(seq // step_t,),
        in_specs=[
            pl.BlockSpec((bsz, step_t, d), lambda i: (0, i, 0)),
            pl.BlockSpec((bsz, step_t, d_rnn), lambda i: (0, i, 0)),
            pl.BlockSpec((bsz, step_t, d_att), lambda i: (0, i, 0)),
            _single(mod.shape, lambda i: (0, 0, 0)),
            _single((1, d), const),
            _single((1, d), const),
            _single(w_out_b.shape, const),
            _single(w_up_b.shape, const),
            _single(fcw.shape, const),
            _single((1, 2 * d_ff), const),
            _single(w_down_b.shape, const),
        ],
        out_specs=pl.BlockSpec((bsz, step_t, d), lambda i: (0, i, 0)),
        out_shape=jax.ShapeDtypeStruct((bsz, seq, d), F32),
        scratch_shapes=[pltpu.VMEM(((cw - 1) * SUBLANES, 2 * d_ff), F32)],
        compiler_params=pltpu.CompilerParams(dimension_semantics=("arbitrary",),
                                             vmem_limit_bytes=VMEM_LIMIT_BYTES),
        name="outproj_ffn",
    )(x, yr, ya, mod, n2g.reshape(1, d), fg.reshape(1, d), w_out_b, w_up_b, fcw,
      fcb.reshape(1, 2 * d_ff), w_down_b)


def kernel(x, c, ada_w, ada_b, norm1_g, w_in, rnn_conv_w, rnn_conv_b, rg_wa, rg_ba, rg_wx, rg_bx, rg_lambda, rel_bias, w_out, norm2_g, w_up, ffn_conv_w, ffn_conv_b, w_down, final_g):
    assert ada_w.shape[0] == 1
    l = 0
    mod = _ada(c, ada_w[l], ada_b[l])
    qkv, yr, (w_out_b, w_up_b, w_down_b) = _inproj_rglru(
        x, mod, norm1_g[l], w_in[l], rnn_conv_w[l], rnn_conv_b[l], rg_wa[l], rg_wx[l],
        rg_ba[l], rg_bx[l], rg_lambda[l], (w_out[l], w_up[l], w_down[l]))
    rbp = jnp.pad(rel_bias[l], ((0, 0), (0, RBPAD - rel_bias.shape[2])))
    ya = _attention(qkv, rbp)
    return _outproj_ffn(x, yr, ya, mod, norm2_g[l], final_g, w_out_b, w_up_b,
                        ffn_conv_w[l], ffn_conv_b[l], w_down_b)
```

```python
import functools

import jax
import jax.numpy as jnp
from jax import lax
from jax.experimental import pallas as pl
from jax.experimental.pallas import tpu as pltpu

F32 = jnp.float32
BF16 = jnp.bfloat16

CHUNK = 64
LOOKBACK = 8
REL_CLIP = 128
HEAD_DIM = 64
RG_C = 8.0
EPS = 1e-6
NEG_INF = -1e30
LOG2E = 1.4426950408889634
F32_TINY = 1.1754943508222875e-38

SUBLANES = 8
LANES = 128
MXU_DIM = 256
VMEM_LIMIT_BYTES = 56 * 1024 * 1024

TT = 64
RT = 8
PW = 256
T1 = 128
CHUNKS_PER_TILE = 4
TQ = CHUNKS_PER_TILE * CHUNK
HEADS_PER_VREG = LANES // HEAD_DIM
HIST = LOOKBACK * CHUNK
NBUF = HIST + TQ
WIN = (LOOKBACK + 2) * CHUNK
GPAD = 768
RBPAD = 384
FF_SPLITS = ((0, 768), (768, 1024), (1792, 1024))
K3_TILES = 2
HT = 16


def _single(block_shape, index_map):
    return pl.BlockSpec(block_shape, index_map, pipeline_mode=pl.Buffered(1))


def _ada_kernel(c_ref, w_ref, b_ref, o_ref):
    c = c_ref[...]
    sc = c * jax.nn.sigmoid(c)
    w = w_ref[...]
    w_hi = w.astype(BF16)
    w_lo = (w - w_hi.astype(F32)).astype(BF16)
    sc_hi = sc.astype(BF16)
    sc_lo = (sc - sc_hi.astype(F32)).astype(BF16)
    acc = jnp.dot(sc_hi, w_hi, preferred_element_type=F32)
    acc = acc + jnp.dot(sc_lo, w_hi, preferred_element_type=F32)
    acc = acc + jnp.dot(sc_hi, w_lo, preferred_element_type=F32)
    o_ref[...] = (acc + b_ref[...]).reshape(o_ref.shape)


def _ada(c, ada_w, ada_b):
    bsz, d = c.shape
    n = ada_w.shape[1]
    bn = 1024
    return pl.pallas_call(
        _ada_kernel,
        grid=(n // bn,),
        in_specs=[pl.BlockSpec((bsz, d), lambda j: (0, 0)),
                  pl.BlockSpec((d, bn), lambda j: (0, j)),
                  pl.BlockSpec((1, bn), lambda j: (0, j))],
        out_specs=pl.BlockSpec((bsz, 1, bn), lambda j: (0, 0, j)),
        out_shape=jax.ShapeDtypeStruct((bsz, 1, n), F32),
        name="ada_mod",
    )(c, ada_w, ada_b.reshape(1, n))


def _gelu_from_half(hx):
    c = (2.0 / jnp.pi) ** 0.5
    u = hx * (2.0 * c + (8.0 * c * 0.044715) * (hx * hx))
    return hx + hx * jnp.tanh(u)


def _inproj_rglru_kernel(x_ref, mod_ref, n1g_ref, win_ref, cw_ref, cb_ref, wa_ref, wx_ref,
                         ba_ref, bx_ref, lam_ref, *rest, d_model, d_rnn, d_att, conv_w, n_cast):
    cast_in = rest[:n_cast]
    qkv_ref, yr_ref = rest[n_cast:n_cast + 2]
    cast_out = rest[n_cast + 2:2 * n_cast + 2]
    tail_s, hc, win_s, wg_s = rest[2 * n_cast + 2:]
    i = pl.program_id(0)

    for src, dst in zip(cast_in, cast_out):
        dst[...] = src[...].astype(BF16)
    halo = (conv_w - 1) * SUBLANES
    prow = RT * SUBLANES
    half = d_rnn // 2

    @pl.when(i == 0)
    def _():
        tail_s[...] = jnp.zeros_like(tail_s)
        hc[...] = jnp.zeros_like(hc)
        for c0 in range(0, win_ref.shape[1], PW):
            wcol = win_ref[:, c0:c0 + PW]
            if 2 * d_rnn <= c0 < 2 * d_rnn + d_att:
                wcol = wcol * (LOG2E * HEAD_DIM ** -0.5)
            elif d_rnn <= c0 < 2 * d_rnn:
                wcol = wcol * 0.5
            win_s[:, c0:c0 + PW] = wcol.astype(BF16)
        wg_s[...] = jnp.zeros_like(wg_s)
        nb, bw = wa_ref.shape[0], wa_ref.shape[1]
        per_half = nb // 2
        for n in range(nb):
            hf, k = divmod(n, per_half)
            rs = slice(k * bw, (k + 1) * bw)
            wg_s[hf, rs, k * bw:(k + 1) * bw] = wa_ref[n].astype(BF16)
            wg_s[hf, rs, half + k * bw:half + (k + 1) * bw] = wx_ref[n].astype(BF16)

    gain1 = n1g_ref[...] * (1.0 + mod_ref[:, :, d_model:2 * d_model])
    lam = lam_ref[...]
    log_sig = jnp.minimum(lam, 0.0) - jnp.log1p(jnp.exp(-jnp.abs(lam)))
    neg_rate_half = (-0.5 * RG_C) * log_sig
    rate2_half = -LOG2E * neg_rate_half
    cw_half = 0.5 * cw_ref[...]
    cb_half = 0.5 * cb_ref[...]
    ba_half = 0.5 * ba_ref[...]
    bx_half = 0.5 * bx_ref[...]
    sh1 = mod_ref[:, :, 0:d_model]
    trow = TT * SUBLANES
    n_rec = TT // RT

    def normed_lhs(t):
        x = x_ref[:, t * TT:(t + 1) * TT, :]
        ms = jnp.mean(x * x, axis=-1, keepdims=True)
        h = x * lax.rsqrt(ms + EPS) * gain1 + sh1
        return h.reshape(trow, d_model).astype(BF16)

    def project_cols(hb, lo, hi):
        part = jnp.dot(hb, win_s[:, lo:hi], preferred_element_type=F32)
        return part.reshape(SUBLANES, TT, hi - lo)

    def branch(hb, j):
        return [project_cols(hb, j * d_rnn + p, j * d_rnn + p + PW) for p in range(0, d_rnn, PW)]

    def emit_qkv(t, hb, p):
        o_q = 2 * d_rnn
        part = project_cols(hb, o_q + p * PW, o_q + (p + 1) * PW)
        qkv_ref[:, t * TT:(t + 1) * TT, p * PW:(p + 1) * PW] = part.astype(BF16)

    def recur_piece(t, xr_p, gr_p, c, tail, hcur):
        ts = slice(c * RT, (c + 1) * RT)
        xr = jnp.concatenate([jnp.swapaxes(p[:, ts, :], 0, 1).reshape(prow, PW) for p in xr_p], axis=1)
        ext = jnp.concatenate([tail, xr], axis=0)
        xh = cb_half
        for k in range(conv_w):
            xh = xh + ext[k * SUBLANES:k * SUBLANES + prow, :] * cw_half[k:k + 1, :]
        xhb = xh.astype(BF16)
        g0 = jnp.dot(xhb[:, 0:half], wg_s[0], preferred_element_type=F32)
        g1 = jnp.dot(xhb[:, half:d_rnn], wg_s[1], preferred_element_type=F32)
        th_r = jnp.tanh(jnp.concatenate([g0[:, 0:half], g1[:, 0:half]], axis=1) + ba_half)
        th_i = jnp.tanh(jnp.concatenate([g0[:, half:], g1[:, half:]], axis=1) + bx_half)
        a = jnp.exp2(th_r * rate2_half + rate2_half)
        w = jnp.tanh(th_r * neg_rate_half + neg_rate_half) * (a * a + 1.0)
        mult = w * lax.rsqrt(jnp.maximum(w, F32_TINY))
        a3 = a.reshape(RT, SUBLANES, d_rnn)
        b3 = (mult * (th_i * xh + xh)).reshape(RT, SUBLANES, d_rnn)
        hs = []
        for s in range(RT):
            hcur = a3[s] * hcur + b3[s]
            hs.append(hcur)
        gate = jnp.concatenate([_gelu_from_half(p[:, ts, :]) for p in gr_p], axis=2)
        yr = jnp.swapaxes(jnp.stack(hs, axis=0), 0, 1) * gate
        yr_ref[:, t * TT + c * RT:t * TT + (c + 1) * RT, :] = yr.astype(BF16)
        return ext[prow:prow + halo, :], hcur

    n_tiles = T1 // TT
    n_qkv = (3 * d_att) // PW
    tail, hcur = tail_s[...], hc[...]
    hb = normed_lhs(0)
    xr_p, gr_p = branch(hb, 0), branch(hb, 1)
    for t in range(n_tiles):
        nxt = t + 1 < n_tiles
        ahead = {}
        for c in range(max(n_rec, n_qkv)):
            if c < n_qkv:
                emit_qkv(t, hb, c)
            if c < n_rec:
                tail, hcur = recur_piece(t, xr_p, gr_p, c, tail, hcur)
            if nxt and c == n_rec - 3:
                ahead["hb"] = normed_lhs(t + 1)
            if nxt and c == n_rec - 2:
                ahead["xr"] = branch(ahead["hb"], 0)
            if nxt and c == n_rec - 1:
                ahead["gr"] = branch(ahead["hb"], 1)
        if nxt:
            hb, xr_p, gr_p = ahead["hb"], ahead["xr"], ahead["gr"]
    tail_s[...] = tail
    hc[...] = hcur


def _inproj_rglru(x, mod, n1g, w_in, conv_w, conv_b, wa, wx, ba, bx, lam, later_weights):
    bsz, seq, d = x.shape
    assert bsz == SUBLANES and seq % T1 == 0
    d_in = w_in.shape[1]
    d_rnn = conv_w.shape[1]
    d_att = (d_in - 2 * d_rnn) // 3
    assert d_att == d_rnn and T1 % TT == 0 and TT % RT == 0
    cw = conv_w.shape[0]
    const = lambda i: (0, 0)
    n_steps = seq // T1
    assert all(w.shape[0] % (2 * SUBLANES * n_steps) == 0 for w in later_weights)
    cast_specs = [pl.BlockSpec((w.shape[0] // n_steps, w.shape[1]), lambda i: (i, 0))
                  for w in later_weights]
    kern = functools.partial(_inproj_rglru_kernel, d_model=d, d_rnn=d_rnn, d_att=d_att, conv_w=cw,
                             n_cast=len(later_weights))
    outs = pl.pallas_call(
        kern,
        grid=(n_steps,),
        in_specs=[
            pl.BlockSpec((bsz, T1, d), lambda i: (0, i, 0)),
            _single(mod.shape, lambda i: (0, 0, 0)),
            _single((1, d), const),
            _single(w_in.shape, const),
            _single(conv_w.shape, const),
            _single((1, d_rnn), const),
            _single(wa.shape, lambda i: (0, 0, 0)),
            _single(wx.shape, lambda i: (0, 0, 0)),
            _single((1, d_rnn), const),
            _single((1, d_rnn), const),
            _single((1, d_rnn), const),
        ] + cast_specs,
        out_specs=[pl.BlockSpec((bsz, T1, 3 * d_att), lambda i: (0, i, 0)),
                   pl.BlockSpec((bsz, T1, d_rnn), lambda i: (0, i, 0))] + cast_specs,
        out_shape=[jax.ShapeDtypeStruct((bsz, seq, 3 * d_att), BF16),
                   jax.ShapeDtypeStruct((bsz, seq, d_rnn), BF16)]
        + [jax.ShapeDtypeStruct(w.shape, BF16) for w in later_weights],
        scratch_shapes=[
            pltpu.VMEM(((cw - 1) * SUBLANES, d_rnn), F32),
            pltpu.VMEM((SUBLANES, d_rnn), F32),
            pltpu.VMEM(w_in.shape, BF16),
            pltpu.VMEM((2, d_rnn // 2, d_rnn), BF16),
        ],
        compiler_params=pltpu.CompilerParams(dimension_semantics=("arbitrary",),
                                             vmem_limit_bytes=VMEM_LIMIT_BYTES),
        name="inproj_rglru",
    )(x, mod, n1g.reshape(1, d), w_in, conv_w, conv_b.reshape(1, d_rnn), wa, wx,
      ba.reshape(1, d_rnn), bx.reshape(1, d_rnn), lam.reshape(1, d_rnn), *later_weights)
    return outs[0], outs[1], outs[2:]


def _build_bias_tables(rbp_ref, bias_s, n_heads):
    r_io = lax.broadcasted_iota(jnp.int32, (RBPAD, GPAD), 0)
    j_io = lax.broadcasted_iota(jnp.int32, (RBPAD, GPAD), 1)
    idx = jnp.clip(HIST + CHUNK - 1 - j_io, -REL_CLIP, REL_CLIP) + REL_CLIP
    sel = (r_io == idx).astype(F32)
    g = jnp.dot(rbp_ref[...], sel, precision=lax.Precision.HIGHEST, preferred_element_type=F32)
    g = (g - g[:, 0:1]) * LOG2E
    col = lax.broadcasted_iota(jnp.int32, (CHUNK, WIN), 1)
    band = (LOOKBACK + 1) * CHUNK
    for h in range(n_heads):
        gh = jnp.broadcast_to(g[h:h + 1, :], (CHUNK, GPAD))
        even = pltpu.roll(gh, GPAD - (CHUNK - 1), 1, stride=1, stride_axis=0)[:, 0:WIN]
        odd = pltpu.roll(gh, 1, 1, stride=1, stride_axis=0)[:, 0:WIN]
        bias_s[h, 0] = jnp.where(col < band, even, NEG_INF)
        bias_s[h, 1] = jnp.where(col >= CHUNK, odd, NEG_INF)


ZERO_BIAS_COLS = ((0, HIST - REL_CLIP), (LANES, HIST - REL_CLIP))


def _attn_tile(q_ref, k_refs, v_refs, o_ref, bias_s, skip, first_valid, n_heads):
    lane = lax.broadcasted_iota(jnp.int32, (1, LANES), 1)
    n_kt = len(k_refs)
    base = skip * TQ
    n_p = n_kt - skip

    def masked_q(h):
        pair, hh = divmod(h, HEADS_PER_VREG)
        qp = q_ref[:, pair * LANES:(pair + 1) * LANES]
        in_head = jnp.logical_and(lane >= hh * HEAD_DIM, lane < (hh + 1) * HEAD_DIM)
        return jnp.where(in_head, qp, jnp.zeros_like(qp))

    def score_piece(h, qm, j):
        pair = h // HEADS_PER_VREG
        return lax.dot_general(qm, k_refs[skip + j][:, pair * LANES:(pair + 1) * LANES],
                               (((1,), (1,)), ((), ())), preferred_element_type=F32)

    def softmax_piece(h, s_all, c):
        w0 = (c // 2) * LANES
        lo = max(base - w0, 0)
        raw = s_all[c * CHUNK:(c + 1) * CHUNK, w0 + lo - base:w0 + WIN - base]
        z0, z1 = ZERO_BIAS_COLS[c % 2]
        z0 = max(z0, lo)
        parts = []
        if z0 > lo:
            parts.append(raw[:, 0:z0 - lo] + bias_s[h, c % 2, :, lo:z0])
        if z1 > z0:
            parts.append(raw[:, z0 - lo:z1 - lo])
        hi0 = max(z1, lo)
        parts.append(raw[:, hi0 - lo:WIN - lo] + bias_s[h, c % 2, :, hi0:WIN])
        s = jnp.concatenate(parts, axis=1)
        if first_valid is not None:
            col = lax.broadcasted_iota(jnp.int32, (1, WIN - lo), 1) + (w0 + lo)
            s = jnp.where(col >= first_valid, s, NEG_INF)
        m = jnp.max(s, axis=-1, keepdims=True)
        p = jnp.exp2(s - m)
        inv = 1.0 / jnp.sum(p, axis=-1, keepdims=True)
        row = [p.astype(BF16)]
        if w0 + lo > base:
            row.insert(0, jnp.zeros((CHUNK, w0 + lo - base), BF16))
        if w0 + WIN < NBUF:
            row.append(jnp.zeros((CHUNK, NBUF - w0 - WIN), BF16))
        return inv, jnp.concatenate(row, axis=1)

    def value_piece(h, pmat, j):
        pair = h // HEADS_PER_VREG
        return jnp.dot(pmat[:, j * TQ:(j + 1) * TQ],
                       v_refs[skip + j][:, pair * LANES:(pair + 1) * LANES],
                       preferred_element_type=F32)

    def finish(h, o, inv, o_prev):
        pair, hh = divmod(h, HEADS_PER_VREG)
        o = o * inv
        if hh == 0:
            return o
        in_head = jnp.logical_and(lane >= hh * HEAD_DIM, lane < (hh + 1) * HEAD_DIM)
        o_ref[:, pair * LANES:(pair + 1) * LANES] = jnp.where(in_head, o, o_prev).astype(BF16)
        return None

    assert 0 < n_p <= CHUNKS_PER_TILE and HEADS_PER_VREG == 2
    qm = masked_q(0)
    s_next = jnp.concatenate([score_piece(0, qm, j) for j in range(n_p)], axis=1)
    o_pair = None
    prev = None
    for h in range(n_heads + 1):
        cur = h < n_heads
        nxt = h + 1 < n_heads
        s_cur = s_next
        if nxt:
            qm = masked_q(h + 1)
        pieces, invs, prow, o_acc = [], [], [], None
        for c in range(CHUNKS_PER_TILE):
            if nxt and c < n_p:
                pieces.append(score_piece(h + 1, qm, c))
            if prev is not None and c < n_p:
                part = value_piece(prev[0], prev[1], c)
                o_acc = part if o_acc is None else o_acc + part
            if cur:
                inv, pr = softmax_piece(h, s_cur, c)
                invs.append(inv)
                prow.append(pr)
        if prev is not None:
            o_pair = finish(prev[0], o_acc, prev[2], o_pair)
        if nxt:
            s_next = jnp.concatenate(pieces, axis=1)
        prev = (h, jnp.concatenate(prow, axis=0), jnp.concatenate(invs, axis=0)) if cur else None


def _attn_kernel(*refs, n_heads):
    n_win = NBUF // TQ
    q_ref = refs[0]
    k_refs = refs[1:1 + n_win]
    v_refs = refs[1 + n_win:1 + 2 * n_win]
    rbp_ref, o_ref, bias_s = refs[1 + 2 * n_win:]
    b = pl.program_id(0)
    t = pl.program_id(1)

    @pl.when(jnp.logical_and(b == 0, t == 0))
    def _():
        _build_bias_tables(rbp_ref, bias_s, n_heads)

    @pl.when(t == 0)
    def _():
        _attn_tile(q_ref, k_refs, v_refs, o_ref, bias_s, n_win - 1, None, n_heads)

    @pl.when(jnp.logical_and(t > 0, t < n_win - 1))
    def _():
        _attn_tile(q_ref, k_refs, v_refs, o_ref, bias_s, 0, TQ * (n_win - 1 - t), n_heads)

    @pl.when(t >= n_win - 1)
    def _():
        _attn_tile(q_ref, k_refs, v_refs, o_ref, bias_s, 0, None, n_heads)


def _attention(qkv, rbp):
    bsz, seq, d3 = qkv.shape
    d_att = d3 // 3
    n_heads = d_att // HEAD_DIM
    n_win = NBUF // TQ
    assert TQ == MXU_DIM and seq % TQ == 0 and NBUF % TQ == 0 and n_heads % HEADS_PER_VREG == 0
    kern = functools.partial(_attn_kernel, n_heads=n_heads)

    def window(col, j):
        return pl.BlockSpec((None, TQ, d_att),
                            lambda b, t: (b, jnp.maximum(t - (n_win - 1 - j), 0), col))

    return pl.pallas_call(
        kern,
        grid=(bsz, seq // TQ),
        in_specs=([pl.BlockSpec((None, TQ, d_att), lambda b, t: (b, t, 0))]
                  + [window(1, j) for j in range(n_win)]
                  + [window(2, j) for j in range(n_win)]
                  + [_single(rbp.shape, lambda b, t: (0, 0))]),
        out_specs=pl.BlockSpec((None, TQ, d_att), lambda b, t: (b, t, 0)),
        out_shape=jax.ShapeDtypeStruct((bsz, seq, d_att), BF16),
        scratch_shapes=[pltpu.VMEM((n_heads, 2, CHUNK, WIN), F32)],
        compiler_params=pltpu.CompilerParams(dimension_semantics=("arbitrary", "arbitrary"),
                                             vmem_limit_bytes=VMEM_LIMIT_BYTES),
        name="chunk_attn",
    )(*([qkv] * (1 + 2 * n_win)), rbp)


def _outproj_ffn_kernel(x_ref, yr_ref, ya_ref, mod_ref, n2g_ref, fg_ref, wo_ref,
                        wu_ref, fcw_ref, fcb_ref, wd_ref, o_ref, halo_s,
                        *, d_model, d_ff, conv_w):
    i = pl.program_id(0)
    rows = TT * SUBLANES
    halo = (conv_w - 1) * SUBLANES
    n_hp = TT // HT

    @pl.when(i == 0)
    def _():
        halo_s[...] = jnp.zeros_like(halo_s)

    g1 = mod_ref[:, :, 2 * d_model:3 * d_model]
    sh2 = mod_ref[:, :, 3 * d_model:4 * d_model]
    gain2 = n2g_ref[...] * (1.0 + mod_ref[:, :, 4 * d_model:5 * d_model])
    g2 = mod_ref[:, :, 5 * d_model:6 * d_model]

    def out_proj(t):
        ts = slice(t * TT, (t + 1) * TT)
        yr = yr_ref[:, ts, :].reshape(rows, yr_ref.shape[2])
        ya = ya_ref[:, ts, :].reshape(rows, ya_ref.shape[2])
        d_rnn = yr_ref.shape[2]
        y = (jnp.dot(yr, wo_ref[0:d_rnn, :], preferred_element_type=F32)
             + jnp.dot(ya, wo_ref[d_rnn:, :], preferred_element_type=F32))
        return y.reshape(SUBLANES, TT, d_model)

    def head_piece(t, c, y3):
        cs = slice(c * HT, (c + 1) * HT)
        x1 = x_ref[:, t * TT + c * HT:t * TT + (c + 1) * HT, :] + g1 * y3[:, cs, :]
        ms = jnp.mean(x1 * x1, axis=-1, keepdims=True)
        hn = x1 * lax.rsqrt(ms + EPS) * gain2 + sh2
        return x1, jnp.swapaxes(hn, 0, 1).reshape(HT * SUBLANES, d_model).astype(BF16)

    def tail_piece(t, c, z, x1):
        zc = z[c * HT * SUBLANES:(c + 1) * HT * SUBLANES, :].reshape(HT, SUBLANES, d_model)
        x2 = x1 + g2 * jnp.swapaxes(zc, 0, 1)
        ms2 = jnp.mean(x2 * x2, axis=-1, keepdims=True)
        o_ref[:, t * TT + c * HT:t * TT + (c + 1) * HT, :] = x2 * lax.rsqrt(ms2 + EPS) * fg_ref[...]

    def up_piece(hb, cols):
        return jnp.dot(hb, wu_ref[:, cols], preferred_element_type=F32)

    def conv_piece(up, tail, cols, scale):
        ext = jnp.concatenate([tail, up], axis=0)
        u = scale * fcb_ref[:, cols]
        for k in range(conv_w):
            u = u + ext[k * SUBLANES:k * SUBLANES + rows, :] * (scale * fcw_ref[k:k + 1, cols])
        return u, ext[rows:rows + halo, :]

    def ffn(hb, tails, side):
        side = list(side)
        z = None
        new_tails = []
        for j, (c0, w) in enumerate(FF_SPLITS):
            cg = slice(c0, c0 + w)
            cv = slice(d_ff + c0, d_ff + c0 + w)
            up_g = up_piece(hb, cg)
            if side:
                side.pop(0)()
            up_v = up_piece(hb, cv)
            if side:
                side.pop(0)()
            hg, tg = conv_piece(up_g, tails[j][0], cg, 0.5)
            uv, tv = conv_piece(up_v, tails[j][1], cv, 1.0)
            new_tails.append((tg, tv))
            act = ((hg + hg * jnp.tanh(hg)) * uv).astype(BF16)
            zc = jnp.dot(act, wd_ref[c0:c0 + w, :], preferred_element_type=F32)
            z = zc if z is None else z + zc
        assert not side
        return z, new_tails

    tails = [(halo_s[:, c0:c0 + w], halo_s[:, d_ff + c0:d_ff + c0 + w]) for c0, w in FF_SPLITS]
    y3 = out_proj(0)
    hp = [head_piece(0, c, y3) for c in range(n_hp)]
    x1_0 = [p[0] for p in hp]
    hb_0 = jnp.concatenate([p[1] for p in hp], axis=0)

    state = {}

    def side_out_proj():
        state["y3"] = out_proj(1)

    def side_head(c):
        def run():
            state.setdefault("hp", []).append(head_piece(1, c, state["y3"]))
        return run

    z_0, tails = ffn(hb_0, tails, [side_out_proj] + [side_head(c) for c in range(n_hp)])
    x1_1 = [p[0] for p in state["hp"]]
    hb_1 = jnp.concatenate([p[1] for p in state["hp"]], axis=0)

    def side_tail(c):
        return lambda: tail_piece(0, c, z_0, x1_0[c])

    z_1, tails = ffn(hb_1, tails, [side_tail(c) for c in range(n_hp)])
    for c in range(n_hp):
        tail_piece(1, c, z_1, x1_1[c])
    for (c0, w), (tg, tv) in zip(FF_SPLITS, tails):
        halo_s[:, c0:c0 + w] = tg
        halo_s[:, d_ff + c0:d_ff + c0 + w] = tv


def _outproj_ffn(x, yr, ya, mod, n2g, fg, w_out_b, w_up_b, fcw, fcb, w_down_b):
    bsz, seq, d = x.shape
    d_ff = w_down_b.shape[0]
    cw = fcw.shape[0]
    d_rnn = yr.shape[2]
    d_att = ya.shape[2]
    step_t = K3_TILES * TT
    assert K3_TILES == 2 and sum(w for _, w in FF_SPLITS) == d_ff
    assert seq % step_t == 0 and TT % HT == 0
    const = lambda i: (0, 0)
    kern = functools.partial(_outproj_ffn_kernel, d_model=d, d_ff=d_ff, conv_w=cw)
    return pl.pallas_call(
        kern,
        grid=(seq // step_t,),
        in_specs=[
            pl.BlockSpec((bsz, step_t, d), lambda i: (0, i, 0)),
            pl.BlockSpec((bsz, step_t, d_rnn), lambda i: (0, i, 0)),
            pl.BlockSpec((bsz, step_t, d_att), lambda i: (0, i, 0)),
            _single(mod.shape, lambda i: (0, 0, 0)),
            _single((1, d), const),
            _single((1, d), const),
            _single(w_out_b.shape, const),
            _single(w_up_b.shape, const),
            _single(fcw.shape, const),
            _single((1, 2 * d_ff), const),
            _single(w_down_b.shape, const),
        ],
        out_specs=pl.BlockSpec((bsz, step_t, d), lambda i: (0, i, 0)),
        out_shape=jax.ShapeDtypeStruct((bsz, seq, d), F32),
        scratch_shapes=[pltpu.VMEM(((cw - 1) * SUBLANES, 2 * d_ff), F32)],
        compiler_params=pltpu.CompilerParams(dimension_semantics=("arbitrary",),
                                             vmem_limit_bytes=VMEM_LIMIT_BYTES),
        name="outproj_ffn",
    )(x, yr, ya, mod, n2g.reshape(1, d), fg.reshape(1, d), w_out_b, w_up_b, fcw,
      fcb.reshape(1, 2 * d_ff), w_down_b)


def kernel(x, c, ada_w, ada_b, norm1_g, w_in, rnn_conv_w, rnn_conv_b, rg_wa, rg_ba, rg_wx, rg_bx, rg_lambda, rel_bias, w_out, norm2_g, w_up, ffn_conv_w, ffn_conv_b, w_down, final_g):
    assert ada_w.shape[0] == 1
    l = 0
    mod = _ada(c, ada_w[l], ada_b[l])
    qkv, yr, (w_out_b, w_up_b, w_down_b) = _inproj_rglru(
        x, mod, norm1_g[l], w_in[l], rnn_conv_w[l], rnn_conv_b[l], rg_wa[l], rg_wx[l],
        rg_ba[l], rg_bx[l], rg_lambda[l], (w_out[l], w_up[l], w_down[l]))
    rbp = jnp.pad(rel_bias[l], ((0, 0), (0, RBPAD - rel_bias.shape[2])))
    ya = _attention(qkv, rbp)
    return _outproj_ffn(x, yr, ya, mod, norm2_g[l], final_g, w_out_b, w_up_b,
                        ffn_conv_w[l], ffn_conv_b[l], w_down_b)
```

```python
import functools

import jax
import jax.numpy as jnp
from jax import lax
from jax.experimental import pallas as pl
from jax.experimental.pallas import tpu as pltpu

F32 = jnp.float32
BF16 = jnp.bfloat16

CHUNK = 64
LOOKBACK = 8
REL_CLIP = 128
HEAD_DIM = 64
RG_C = 8.0
EPS = 1e-6
NEG_INF = -1e30
LOG2E = 1.4426950408889634
F32_TINY = 1.1754943508222875e-38

SUBLANES = 8
LANES = 128
MXU_DIM = 256
VMEM_LIMIT_BYTES = 56 * 1024 * 1024

TT = 64
RT = 8
PW = 256
T1 = 128
CHUNKS_PER_TILE = 4
TQ = CHUNKS_PER_TILE * CHUNK
HEADS_PER_VREG = LANES // HEAD_DIM
HIST = LOOKBACK * CHUNK
NBUF = HIST + TQ
WIN = (LOOKBACK + 2) * CHUNK
GPAD = 768
RBPAD = 384
FF_SPLITS = ((0, 256), (256, 1280), (1536, 1280))
K3_TILES = 2
HT = 16


def _single(block_shape, index_map):
    return pl.BlockSpec(block_shape, index_map, pipeline_mode=pl.Buffered(1))


def _ada_kernel(c_ref, w_ref, b_ref, o_ref):
    c = c_ref[...]
    sc = c * jax.nn.sigmoid(c)
    w = w_ref[...]
    w_hi = w.astype(BF16)
    w_lo = (w - w_hi.astype(F32)).astype(BF16)
    sc_hi = sc.astype(BF16)
    sc_lo = (sc - sc_hi.astype(F32)).astype(BF16)
    acc = jnp.dot(sc_hi, w_hi, preferred_element_type=F32)
    acc = acc + jnp.dot(sc_lo, w_hi, preferred_element_type=F32)
    acc = acc + jnp.dot(sc_hi, w_lo, preferred_element_type=F32)
    o_ref[...] = (acc + b_ref[...]).reshape(o_ref.shape)


def _ada(c, ada_w, ada_b):
    bsz, d = c.shape
    n = ada_w.shape[1]
    bn = 1024
    return pl.pallas_call(
        _ada_kernel,
        grid=(n // bn,),
        in_specs=[pl.BlockSpec((bsz, d), lambda j: (0, 0)),
                  pl.BlockSpec((d, bn), lambda j: (0, j)),
                  pl.BlockSpec((1, bn), lambda j: (0, j))],
        out_specs=pl.BlockSpec((bsz, 1, bn), lambda j: (0, 0, j)),
        out_shape=jax.ShapeDtypeStruct((bsz, 1, n), F32),
        name="ada_mod",
    )(c, ada_w, ada_b.reshape(1, n))


def _gelu_from_half(hx):
    c = (2.0 / jnp.pi) ** 0.5
    u = hx * (2.0 * c + (8.0 * c * 0.044715) * (hx * hx))
    return hx + hx * jnp.tanh(u)


def _inproj_rglru_kernel(x_ref, mod_ref, n1g_ref, win_ref, cw_ref, cb_ref, wa_ref, wx_ref,
                         ba_ref, bx_ref, lam_ref, *rest, d_model, d_rnn, d_att, conv_w, n_cast):
    cast_in = rest[:n_cast]
    qkv_ref, yr_ref = rest[n_cast:n_cast + 2]
    cast_out = rest[n_cast + 2:2 * n_cast + 2]
    tail_s, hc, win_s, wg_s = rest[2 * n_cast + 2:]
    i = pl.program_id(0)

    for src, dst in zip(cast_in, cast_out):
        dst[...] = src[...].astype(BF16)
    halo = (conv_w - 1) * SUBLANES
    prow = RT * SUBLANES
    half = d_rnn // 2

    @pl.when(i == 0)
    def _():
        tail_s[...] = jnp.zeros_like(tail_s)
        hc[...] = jnp.zeros_like(hc)
        for c0 in range(0, win_ref.shape[1], PW):
            wcol = win_ref[:, c0:c0 + PW]
            if 2 * d_rnn <= c0 < 2 * d_rnn + d_att:
                wcol = wcol * (LOG2E * HEAD_DIM ** -0.5)
            elif d_rnn <= c0 < 2 * d_rnn:
                wcol = wcol * 0.5
            win_s[:, c0:c0 + PW] = wcol.astype(BF16)
        wg_s[...] = jnp.zeros_like(wg_s)
        nb, bw = wa_ref.shape[0], wa_ref.shape[1]
        per_half = nb // 2
        for n in range(nb):
            hf, k = divmod(n, per_half)
            rs = slice(k * bw, (k + 1) * bw)
            wg_s[hf, rs, k * bw:(k + 1) * bw] = wa_ref[n].astype(BF16)
            wg_s[hf, rs, half + k * bw:half + (k + 1) * bw] = wx_ref[n].astype(BF16)

    gain1 = n1g_ref[...] * (1.0 + mod_ref[:, :, d_model:2 * d_model])
    lam = lam_ref[...]
    log_sig = jnp.minimum(lam, 0.0) - jnp.log1p(jnp.exp(-jnp.abs(lam)))
    neg_rate_half = (-0.5 * RG_C) * log_sig
    rate2_half = -LOG2E * neg_rate_half
    cw_half = 0.5 * cw_ref[...]
    cb_half = 0.5 * cb_ref[...]
    ba_half = 0.5 * ba_ref[...]
    bx_half = 0.5 * bx_ref[...]
    sh1 = mod_ref[:, :, 0:d_model]
    trow = TT * SUBLANES
    n_rec = TT // RT

    def normed_lhs(t):
        x = x_ref[:, t * TT:(t + 1) * TT, :]
        ms = jnp.mean(x * x, axis=-1, keepdims=True)
        h = x * lax.rsqrt(ms + EPS) * gain1 + sh1
        return h.reshape(trow, d_model).astype(BF16)

    def project_cols(hb, lo, hi):
        part = jnp.dot(hb, win_s[:, lo:hi], preferred_element_type=F32)
        return part.reshape(SUBLANES, TT, hi - lo)

    def branch(hb, j):
        return [project_cols(hb, j * d_rnn + p, j * d_rnn + p + PW) for p in range(0, d_rnn, PW)]

    def emit_qkv(t, hb, p):
        o_q = 2 * d_rnn
        part = project_cols(hb, o_q + p * PW, o_q + (p + 1) * PW)
        qkv_ref[:, t * TT:(t + 1) * TT, p * PW:(p + 1) * PW] = part.astype(BF16)

    def recur_piece(t, xr_p, gr_p, c, tail, hcur):
        ts = slice(c * RT, (c + 1) * RT)
        xr = jnp.concatenate([jnp.swapaxes(p[:, ts, :], 0, 1).reshape(prow, PW) for p in xr_p], axis=1)
        ext = jnp.concatenate([tail, xr], axis=0)
        xh = cb_half
        for k in range(conv_w):
            xh = xh + ext[k * SUBLANES:k * SUBLANES + prow, :] * cw_half[k:k + 1, :]
        xhb = xh.astype(BF16)
        g0 = jnp.dot(xhb[:, 0:half], wg_s[0], preferred_element_type=F32)
        g1 = jnp.dot(xhb[:, half:d_rnn], wg_s[1], preferred_element_type=F32)
        th_r = jnp.tanh(jnp.concatenate([g0[:, 0:half], g1[:, 0:half]], axis=1) + ba_half)
        th_i = jnp.tanh(jnp.concatenate([g0[:, half:], g1[:, half:]], axis=1) + bx_half)
        a = jnp.exp2(th_r * rate2_half + rate2_half)
        w = jnp.tanh(th_r * neg_rate_half + neg_rate_half) * (a * a + 1.0)
        mult = w * lax.rsqrt(jnp.maximum(w, F32_TINY))
        a3 = a.reshape(RT, SUBLANES, d_rnn)
        b3 = (mult * (th_i * xh + xh)).reshape(RT, SUBLANES, d_rnn)
        hs = []
        for s in range(RT):
            hcur = a3[s] * hcur + b3[s]
            hs.append(hcur)
        gate = jnp.concatenate([_gelu_from_half(p[:, ts, :]) for p in gr_p], axis=2)
        yr = jnp.swapaxes(jnp.stack(hs, axis=0), 0, 1) * gate
        yr_ref[:, t * TT + c * RT:t * TT + (c + 1) * RT, :] = yr.astype(BF16)
        return ext[prow:prow + halo, :], hcur

    n_tiles = T1 // TT
    n_qkv = (3 * d_att) // PW
    tail, hcur = tail_s[...], hc[...]
    hb = normed_lhs(0)
    xr_p, gr_p = branch(hb, 0), branch(hb, 1)
    for t in range(n_tiles):
        nxt = t + 1 < n_tiles
        ahead = {}
        for c in range(max(n_rec, n_qkv)):
            if c < n_qkv:
                emit_qkv(t, hb, c)
            if c < n_rec:
                tail, hcur = recur_piece(t, xr_p, gr_p, c, tail, hcur)
            if nxt and c == n_rec - 3:
                ahead["hb"] = normed_lhs(t + 1)
            if nxt and c == n_rec - 2:
                ahead["xr"] = branch(ahead["hb"], 0)
            if nxt and c == n_rec - 1:
                ahead["gr"] = branch(ahead["hb"], 1)
        if nxt:
            hb, xr_p, gr_p = ahead["hb"], ahead["xr"], ahead["gr"]
    tail_s[...] = tail
    hc[...] = hcur


def _inproj_rglru(x, mod, n1g, w_in, conv_w, conv_b, wa, wx, ba, bx, lam, later_weights):
    bsz, seq, d = x.shape
    assert bsz == SUBLANES and seq % T1 == 0
    d_in = w_in.shape[1]
    d_rnn = conv_w.shape[1]
    d_att = (d_in - 2 * d_rnn) // 3
    assert d_att == d_rnn and T1 % TT == 0 and TT % RT == 0
    cw = conv_w.shape[0]
    const = lambda i: (0, 0)
    n_steps = seq // T1
    assert all(w.shape[0] % (2 * SUBLANES * n_steps) == 0 for w in later_weights)
    cast_specs = [pl.BlockSpec((w.shape[0] // n_steps, w.shape[1]), lambda i: (i, 0))
                  for w in later_weights]
    kern = functools.partial(_inproj_rglru_kernel, d_model=d, d_rnn=d_rnn, d_att=d_att, conv_w=cw,
                             n_cast=len(later_weights))
    outs = pl.pallas_call(
        kern,
        grid=(n_steps,),
        in_specs=[
            pl.BlockSpec((bsz, T1, d), lambda i: (0, i, 0)),
            _single(mod.shape, lambda i: (0, 0, 0)),
            _single((1, d), const),
            _single(w_in.shape, const),
            _single(conv_w.shape, const),
            _single((1, d_rnn), const),
            _single(wa.shape, lambda i: (0, 0, 0)),
            _single(wx.shape, lambda i: (0, 0, 0)),
            _single((1, d_rnn), const),
            _single((1, d_rnn), const),
            _single((1, d_rnn), const),
        ] + cast_specs,
        out_specs=[pl.BlockSpec((bsz, T1, 3 * d_att), lambda i: (0, i, 0)),
                   pl.BlockSpec((bsz, T1, d_rnn), lambda i: (0, i, 0))] + cast_specs,
        out_shape=[jax.ShapeDtypeStruct((bsz, seq, 3 * d_att), BF16),
                   jax.ShapeDtypeStruct((bsz, seq, d_rnn), BF16)]
        + [jax.ShapeDtypeStruct(w.shape, BF16) for w in later_weights],
        scratch_shapes=[
            pltpu.VMEM(((cw - 1) * SUBLANES, d_rnn), F32),
            pltpu.VMEM((SUBLANES, d_rnn), F32),
            pltpu.VMEM(w_in.shape, BF16),
            pltpu.VMEM((2, d_rnn // 2, d_rnn), BF16),
        ],
        compiler_params=pltpu.CompilerParams(dimension_semantics=("arbitrary",),
                                             vmem_limit_bytes=VMEM_LIMIT_BYTES),
        name="inproj_rglru",
    )(x, mod, n1g.reshape(1, d), w_in, conv_w, conv_b.reshape(1, d_rnn), wa, wx,
      ba.reshape(1, d_rnn), bx.reshape(1, d_rnn), lam.reshape(1, d_rnn), *later_weights)
    return outs[0], outs[1], outs[2:]


def _build_bias_tables(rbp_ref, bias_s, n_heads):
    r_io = lax.broadcasted_iota(jnp.int32, (RBPAD, GPAD), 0)
    j_io = lax.broadcasted_iota(jnp.int32, (RBPAD, GPAD), 1)
    idx = jnp.clip(HIST + CHUNK - 1 - j_io, -REL_CLIP, REL_CLIP) + REL_CLIP
    sel = (r_io == idx).astype(F32)
    g = jnp.dot(rbp_ref[...], sel, precision=lax.Precision.HIGHEST, preferred_element_type=F32)
    g = (g - g[:, 0:1]) * LOG2E
    col = lax.broadcasted_iota(jnp.int32, (CHUNK, WIN), 1)
    band = (LOOKBACK + 1) * CHUNK
    for h in range(n_heads):
        gh = jnp.broadcast_to(g[h:h + 1, :], (CHUNK, GPAD))
        even = pltpu.roll(gh, GPAD - (CHUNK - 1), 1, stride=1, stride_axis=0)[:, 0:WIN]
        odd = pltpu.roll(gh, 1, 1, stride=1, stride_axis=0)[:, 0:WIN]
        bias_s[h, 0] = jnp.where(col < band, even, NEG_INF)
        bias_s[h, 1] = jnp.where(col >= CHUNK, odd, NEG_INF)


ZERO_BIAS_COLS = ((0, HIST - REL_CLIP), (LANES, HIST - REL_CLIP))


def _attn_tile(q_ref, k_refs, v_refs, o_ref, bias_s, skip, first_valid, n_heads):
    lane = lax.broadcasted_iota(jnp.int32, (1, LANES), 1)
    n_kt = len(k_refs)
    base = skip * TQ
    n_p = n_kt - skip

    def masked_q(h):
        pair, hh = divmod(h, HEADS_PER_VREG)
        qp = q_ref[:, pair * LANES:(pair + 1) * LANES]
        in_head = jnp.logical_and(lane >= hh * HEAD_DIM, lane < (hh + 1) * HEAD_DIM)
        return jnp.where(in_head, qp, jnp.zeros_like(qp))

    def score_piece(h, qm, j):
        pair = h // HEADS_PER_VREG
        return lax.dot_general(qm, k_refs[skip + j][:, pair * LANES:(pair + 1) * LANES],
                               (((1,), (1,)), ((), ())), preferred_element_type=F32)

    def softmax_piece(h, s_all, c):
        w0 = (c // 2) * LANES
        lo = max(base - w0, 0)
        raw = s_all[c * CHUNK:(c + 1) * CHUNK, w0 + lo - base:w0 + WIN - base]
        z0, z1 = ZERO_BIAS_COLS[c % 2]
        z0 = max(z0, lo)
        parts = []
        if z0 > lo:
            parts.append(raw[:, 0:z0 - lo] + bias_s[h, c % 2, :, lo:z0])
        if z1 > z0:
            parts.append(raw[:, z0 - lo:z1 - lo])
        hi0 = max(z1, lo)
        parts.append(raw[:, hi0 - lo:WIN - lo] + bias_s[h, c % 2, :, hi0:WIN])
        s = jnp.concatenate(parts, axis=1)
        if first_valid is not None:
            col = lax.broadcasted_iota(jnp.int32, (1, WIN - lo), 1) + (w0 + lo)
            s = jnp.where(col >= first_valid, s, NEG_INF)
        m = jnp.max(s, axis=-1, keepdims=True)
        p = jnp.exp2(s - m)
        inv = 1.0 / jnp.sum(p, axis=-1, keepdims=True)
        row = [p.astype(BF16)]
        if w0 + lo > base:
            row.insert(0, jnp.zeros((CHUNK, w0 + lo - base), BF16))
        if w0 + WIN < NBUF:
            row.append(jnp.zeros((CHUNK, NBUF - w0 - WIN), BF16))
        return inv, jnp.concatenate(row, axis=1)

    def value_piece(h, pmat, j):
        pair = h // HEADS_PER_VREG
        return jnp.dot(pmat[:, j * TQ:(j + 1) * TQ],
                       v_refs[skip + j][:, pair * LANES:(pair + 1) * LANES],
                       preferred_element_type=F32)

    def finish(h, o, inv, o_prev):
        pair, hh = divmod(h, HEADS_PER_VREG)
        o = o * inv
        if hh == 0:
            return o
        in_head = jnp.logical_and(lane >= hh * HEAD_DIM, lane < (hh + 1) * HEAD_DIM)
        o_ref[:, pair * LANES:(pair + 1) * LANES] = jnp.where(in_head, o, o_prev).astype(BF16)
        return None

    assert 0 < n_p <= CHUNKS_PER_TILE and HEADS_PER_VREG == 2
    qm = masked_q(0)
    s_next = jnp.concatenate([score_piece(0, qm, j) for j in range(n_p)], axis=1)
    o_pair = None
    prev = None
    for h in range(n_heads + 1):
        cur = h < n_heads
        nxt = h + 1 < n_heads
        s_cur = s_next
        if nxt:
            qm = masked_q(h + 1)
        pieces, invs, prow, o_acc = [], [], [], None
        for c in range(CHUNKS_PER_TILE):
            if nxt and c < n_p:
                pieces.append(score_piece(h + 1, qm, c))
            if prev is not None and c < n_p:
                part = value_piece(prev[0], prev[1], c)
                o_acc = part if o_acc is None else o_acc + part
            if cur:
                inv, pr = softmax_piece(h, s_cur, c)
                invs.append(inv)
                prow.append(pr)
        if prev is not None:
            o_pair = finish(prev[0], o_acc, prev[2], o_pair)
        if nxt:
            s_next = jnp.concatenate(pieces, axis=1)
        prev = (h, jnp.concatenate(prow, axis=0), jnp.concatenate(invs, axis=0)) if cur else None


def _attn_kernel(*refs, n_heads):
    n_win = NBUF // TQ
    q_ref = refs[0]
    k_refs = refs[1:1 + n_win]
    v_refs = refs[1 + n_win:1 + 2 * n_win]
    rbp_ref, o_ref, bias_s = refs[1 + 2 * n_win:]
    b = pl.program_id(0)
    t = pl.program_id(1)

    @pl.when(jnp.logical_and(b == 0, t == 0))
    def _():
        _build_bias_tables(rbp_ref, bias_s, n_heads)

    @pl.when(t == 0)
    def _():
        _attn_tile(q_ref, k_refs, v_refs, o_ref, bias_s, n_win - 1, None, n_heads)

    @pl.when(jnp.logical_and(t > 0, t < n_win - 1))
    def _():
        _attn_tile(q_ref, k_refs, v_refs, o_ref, bias_s, 0, TQ * (n_win - 1 - t), n_heads)

    @pl.when(t >= n_win - 1)
    def _():
        _attn_tile(q_ref, k_refs, v_refs, o_ref, bias_s, 0, None, n_heads)


def _attention(qkv, rbp):
    bsz, seq, d3 = qkv.shape
    d_att = d3 // 3
    n_heads = d_att // HEAD_DIM
    n_win = NBUF // TQ
    assert TQ == MXU_DIM and seq % TQ == 0 and NBUF % TQ == 0 and n_heads % HEADS_PER_VREG == 0
    kern = functools.partial(_attn_kernel, n_heads=n_heads)

    def window(col, j):
        return pl.BlockSpec((None, TQ, d_att),
                            lambda b, t: (b, jnp.maximum(t - (n_win - 1 - j), 0), col))

    return pl.pallas_call(
        kern,
        grid=(bsz, seq // TQ),
        in_specs=([pl.BlockSpec((None, TQ, d_att), lambda b, t: (b, t, 0))]
                  + [window(1, j) for j in range(n_win)]
                  + [window(2, j) for j in range(n_win)]
                  + [_single(rbp.shape, lambda b, t: (0, 0))]),
        out_specs=pl.BlockSpec((None, TQ, d_att), lambda b, t: (b, t, 0)),
        out_shape=jax.ShapeDtypeStruct((bsz, seq, d_att), BF16),
        scratch_shapes=[pltpu.VMEM((n_heads, 2, CHUNK, WIN), F32)],
        compiler_params=pltpu.CompilerParams(dimension_semantics=("arbitrary", "arbitrary"),
                                             vmem_limit_bytes=VMEM_LIMIT_BYTES),
        name="chunk_attn",
    )(*([qkv] * (1 + 2 * n_win)), rbp)


def _outproj_ffn_kernel(x_ref, yr_ref, ya_ref, mod_ref, n2g_ref, fg_ref, wo_ref,
                        wu_ref, fcw_ref, fcb_ref, wd_ref, o_ref, halo_s,
                        *, d_model, d_ff, conv_w):
    i = pl.program_id(0)
    rows = TT * SUBLANES
    halo = (conv_w - 1) * SUBLANES
    n_hp = TT // HT

    @pl.when(i == 0)
    def _():
        halo_s[...] = jnp.zeros_like(halo_s)

    g1 = mod_ref[:, :, 2 * d_model:3 * d_model]
    sh2 = mod_ref[:, :, 3 * d_model:4 * d_model]
    gain2 = n2g_ref[...] * (1.0 + mod_ref[:, :, 4 * d_model:5 * d_model])
    g2 = mod_ref[:, :, 5 * d_model:6 * d_model]

    def out_proj(t):
        ts = slice(t * TT, (t + 1) * TT)
        yr = yr_ref[:, ts, :].reshape(rows, yr_ref.shape[2])
        ya = ya_ref[:, ts, :].reshape(rows, ya_ref.shape[2])
        d_rnn = yr_ref.shape[2]
        y = (jnp.dot(yr, wo_ref[0:d_rnn, :], preferred_element_type=F32)
             + jnp.dot(ya, wo_ref[d_rnn:, :], preferred_element_type=F32))
        return y.reshape(SUBLANES, TT, d_model)

    def head_piece(t, c, y3):
        cs = slice(c * HT, (c + 1) * HT)
        x1 = x_ref[:, t * TT + c * HT:t * TT + (c + 1) * HT, :] + g1 * y3[:, cs, :]
        ms = jnp.mean(x1 * x1, axis=-1, keepdims=True)
        hn = x1 * lax.rsqrt(ms + EPS) * gain2 + sh2
        return x1, jnp.swapaxes(hn, 0, 1).reshape(HT * SUBLANES, d_model).astype(BF16)

    def tail_piece(t, c, z, x1):
        zc = z[c * HT * SUBLANES:(c + 1) * HT * SUBLANES, :].reshape(HT, SUBLANES, d_model)
        x2 = x1 + g2 * jnp.swapaxes(zc, 0, 1)
        ms2 = jnp.mean(x2 * x2, axis=-1, keepdims=True)
        o_ref[:, t * TT + c * HT:t * TT + (c + 1) * HT, :] = x2 * lax.rsqrt(ms2 + EPS) * fg_ref[...]

    def up_piece(hb, cols):
        return jnp.dot(hb, wu_ref[:, cols], preferred_element_type=F32)

    def conv_piece(up, tail, cols, scale):
        ext = jnp.concatenate([tail, up], axis=0)
        u = scale * fcb_ref[:, cols]
        for k in range(conv_w):
            u = u + ext[k * SUBLANES:k * SUBLANES + rows, :] * (scale * fcw_ref[k:k + 1, cols])
        return u, ext[rows:rows + halo, :]

    def ffn(hb, tails, side):
        side = list(side)
        z = None
        new_tails = []
        for j, (c0, w) in enumerate(FF_SPLITS):
            cg = slice(c0, c0 + w)
            cv = slice(d_ff + c0, d_ff + c0 + w)
            up_g = up_piece(hb, cg)
            if side:
                side.pop(0)()
            up_v = up_piece(hb, cv)
            if side:
                side.pop(0)()
            hg, tg = conv_piece(up_g, tails[j][0], cg, 0.5)
            uv, tv = conv_piece(up_v, tails[j][1], cv, 1.0)
            new_tails.append((tg, tv))
            act = ((hg + hg * jnp.tanh(hg)) * uv).astype(BF16)
            zc = jnp.dot(act, wd_ref[c0:c0 + w, :], preferred_element_type=F32)
            z = zc if z is None else z + zc
        assert not side
        return z, new_tails

    tails = [(halo_s[:, c0:c0 + w], halo_s[:, d_ff + c0:d_ff + c0 + w]) for c0, w in FF_SPLITS]
    y3 = out_proj(0)
    hp = [head_piece(0, c, y3) for c in range(n_hp)]
    x1_0 = [p[0] for p in hp]
    hb_0 = jnp.concatenate([p[1] for p in hp], axis=0)

    state = {}

    def side_out_proj():
        state["y3"] = out_proj(1)

    def side_head(c):
        def run():
            state.setdefault("hp", []).append(head_piece(1, c, state["y3"]))
        return run

    z_0, tails = ffn(hb_0, tails, [side_out_proj] + [side_head(c) for c in range(n_hp)])
    x1_1 = [p[0] for p in state["hp"]]
    hb_1 = jnp.concatenate([p[1] for p in state["hp"]], axis=0)

    def side_tail(c):
        return lambda: tail_piece(0, c, z_0, x1_0[c])

    z_1, tails = ffn(hb_1, tails, [side_tail(c) for c in range(n_hp)])
    for c in range(n_hp):
        tail_piece(1, c, z_1, x1_1[c])
    for (c0, w), (tg, tv) in zip(FF_SPLITS, tails):
        halo_s[:, c0:c0 + w] = tg
        halo_s[:, d_ff + c0:d_ff + c0 + w] = tv


def _outproj_ffn(x, yr, ya, mod, n2g, fg, w_out_b, w_up_b, fcw, fcb, w_down_b):
    bsz, seq, d = x.shape
    d_ff = w_down_b.shape[0]
    cw = fcw.shape[0]
    d_rnn = yr.shape[2]
    d_att = ya.shape[2]
    step_t = K3_TILES * TT
    assert K3_TILES == 2 and sum(w for _, w in FF_SPLITS) == d_ff
    assert seq % step_t == 0 and TT % HT == 0
    const = lambda i: (0, 0)
    kern = functools.partial(_outproj_ffn_kernel, d_model=d, d_ff=d_ff, conv_w=cw)
    return pl.pallas_call(
        kern,
        grid=(seq // step_t,),
        in_specs=[
            pl.BlockSpec((bsz, step_t, d), lambda i: (0, i, 0)),
            pl.BlockSpec((bsz, step_t, d_rnn), lambda i: (0, i, 0)),
            pl.BlockSpec((bsz, step_t, d_att), lambda i: (0, i, 0)),
            _single(mod.shape, lambda i: (0, 0, 0)),
            _single((1, d), const),
            _single((1, d), const),
            _single(w_out_b.shape, const),
            _single(w_up_b.shape, const),
            _single(fcw.shape, const),
            _single((1, 2 * d_ff), const),
            _single(w_down_b.shape, const),
        ],
        out_specs=pl.BlockSpec((bsz, step_t, d), lambda i: (0, i, 0)),
        out_shape=jax.ShapeDtypeStruct((bsz, seq, d), F32),
        scratch_shapes=[pltpu.VMEM(((cw - 1) * SUBLANES, 2 * d_ff), F32)],
        compiler_params=pltpu.CompilerParams(dimension_semantics=("arbitrary",),
                                             vmem_limit_bytes=VMEM_LIMIT_BYTES),
        name="outproj_ffn",
    )(x, yr, ya, mod, n2g.reshape(1, d), fg.reshape(1, d), w_out_b, w_up_b, fcw,
      fcb.reshape(1, 2 * d_ff), w_down_b)


def kernel(x, c, ada_w, ada_b, norm1_g, w_in, rnn_conv_w, rnn_conv_b, rg_wa, rg_ba, rg_wx, rg_bx, rg_lambda, rel_bias, w_out, norm2_g, w_up, ffn_conv_w, ffn_conv_b, w_down, final_g):
    assert ada_w.shape[0] == 1
    l = 0
    mod = _ada(c, ada_w[l], ada_b[l])
    qkv, yr, (w_out_b, w_up_b, w_down_b) = _inproj_rglru(
        x, mod, norm1_g[l], w_in[l], rnn_conv_w[l], rnn_conv_b[l], rg_wa[l], rg_wx[l],
        rg_ba[l], rg_bx[l], rg_lambda[l], (w_out[l], w_up[l], w_down[l]))
    rbp = jnp.pad(rel_bias[l], ((0, 0), (0, RBPAD - rel_bias.shape[2])))
    ya = _attention(qkv, rbp)
    return _outproj_ffn(x, yr, ya, mod, norm2_g[l], final_g, w_out_b, w_up_b,
                        ffn_conv_w[l], ffn_conv_b[l], w_down_b)
```
